```python
import jax, jax.numpy as jnp
from jax import lax
import numpy as np

D_MODEL = 1024
BATCH = 16
SEQ = 256
DEPTH = 2
DEC_BATCH = 4
DEC_SEQ = 4096
PAST_LEN = 512

GRID_W = 64
N_EVEN = (DEPTH + 1) // 2
N_ODD = DEPTH // 2
EPS = 1e-6
POOL_WIDTH = D_MODEL // 2
POOL_GROUPS = 4
POOL_GROUP_DIM = POOL_WIDTH // POOL_GROUPS
POOL_WINDOWS = (2, 4, 8, 16)
CONV_WIDTH = D_MODEL // 2
CONV_K = 3
EVEN_IN = POOL_WIDTH + 3 * CONV_WIDTH
EVEN_OUT = POOL_WIDTH + CONV_WIDTH
HEAD_DIM = 128
N_HEADS = D_MODEL // HEAD_DIM
N_KV_HEADS = N_HEADS // 4
QKV_WIDTH = (N_HEADS + 2 * N_KV_HEADS) * HEAD_DIM
Q_BLOCK = 128
ROPE_THETA = 10000.0
AXIS_DIM = HEAD_DIM // 2
D_FF = ((8 * D_MODEL // 3 + 255) // 256) * 256

kernel_name = "hybrid_pool_conv_gqa_diffusion_step"


def rms_norm(x, g):
    xf = x.astype(jnp.float32)
    y = xf * lax.rsqrt(jnp.mean(xf * xf, axis=-1, keepdims=True) + EPS)
    return (y * g.astype(jnp.float32)).astype(x.dtype)


def ada_params(cvec, w_ada, b_ada):
    m = jax.nn.silu(cvec) @ w_ada + b_ada
    return jnp.split(m[:, None, :], 6, axis=-1)


def modulate(x, g, shift, scale):
    return rms_norm(x, g) * (1 + scale) + shift


def window_mean(x, w):
    L = x.shape[1]
    cs = jnp.pad(jnp.cumsum(x.astype(jnp.float32), axis=1), ((0, 0), (1, 0), (0, 0)))
    t = jnp.arange(L)
    lo = jnp.clip(t - w // 2, 0, L)
    hi = jnp.clip(t + w // 2, 0, L)
    cnt = (hi - lo).astype(jnp.float32)[None, :, None]
    return ((cs[:, hi] - cs[:, lo]) / cnt).astype(x.dtype)


def pool_mixer(u, w_grp, scale):
    B, L, _ = u.shape
    ug = u.reshape(B, L, POOL_GROUPS, POOL_GROUP_DIM)
    pooled = jnp.stack([window_mean(ug[:, :, i], w) for i, w in enumerate(POOL_WINDOWS)], axis=2) - ug
    y = jnp.einsum('blgc,gcd->blgd', pooled, w_grp).reshape(B, L, POOL_WIDTH)
    return y * scale


def short_conv_mixer(bg, cg, v, conv_w, conv_b):
    z = cg * v
    L = z.shape[1]
    zp = jnp.pad(z, ((0, 0), (CONV_K // 2, CONV_K // 2), (0, 0)))
    conv = sum(zp[:, k:k + L] * conv_w[k] for k in range(CONV_K)) + conv_b
    return bg * conv


def even_mixer(h, w_in, pool_w, pool_scale, conv_w, conv_b, w_out):
    p = h @ w_in
    u, bg, cg, v = jnp.split(p, [POOL_WIDTH, POOL_WIDTH + CONV_WIDTH, POOL_WIDTH + 2 * CONV_WIDTH], axis=-1)
    ya = pool_mixer(u, pool_w, pool_scale)
    yb = short_conv_mixer(bg, cg, v, conv_w, conv_b)
    return jnp.concatenate([ya, yb], axis=-1) @ w_out


def rope_2d_tables(L):
    rows = L // GRID_W
    r = jnp.repeat(jnp.arange(rows), GRID_W).astype(jnp.float32)
    col = jnp.tile(jnp.arange(GRID_W), rows).astype(jnp.float32)
    inv = ROPE_THETA ** (-jnp.arange(0, AXIS_DIM, 2, dtype=jnp.float32) / AXIS_DIM)
    ar = r[:, None] * inv
    ac = col[:, None] * inv
    ang = jnp.concatenate([ar, ar, ac, ac], axis=-1)
    return jnp.cos(ang), jnp.sin(ang)


def _rot_half(a):
    a1, a2 = jnp.split(a, 2, axis=-1)
    return jnp.concatenate([-a2, a1], axis=-1)


def apply_rope_2d(x, cos, sin):
    xr, xc = jnp.split(x, 2, axis=-1)
    xrot = jnp.concatenate([_rot_half(xr), _rot_half(xc)], axis=-1)
    return (x * cos[None, :, None, :] + xrot * sin[None, :, None, :]).astype(x.dtype)


def qkv_proj(h, w_qkv, q_gain, k_gain):
    B, L, _ = h.shape
    p = h @ w_qkv
    q, k, v = jnp.split(p, [N_HEADS * HEAD_DIM, (N_HEADS + N_KV_HEADS) * HEAD_DIM], axis=-1)
    q = rms_norm(q.reshape(B, L, N_HEADS, HEAD_DIM), q_gain)
    k = rms_norm(k.reshape(B, L, N_KV_HEADS, HEAD_DIM), k_gain)
    v = v.reshape(B, L, N_KV_HEADS, HEAD_DIM)
    return q, k, v


def blocked_attention(q, k, v):
    B, Lq, _, _ = q.shape
    nb = Lq // Q_BLOCK
    G = N_HEADS // N_KV_HEADS
    qb = q.reshape(B, nb, Q_BLOCK, N_KV_HEADS, G, HEAD_DIM).transpose(1, 0, 2, 3, 4, 5)
    sm_scale = HEAD_DIM ** -0.5

    def one_block(qi):
        s = jnp.einsum('bqkgd,bskd->bkgqs', qi, k, preferred_element_type=jnp.float32) * sm_scale
        p = jax.nn.softmax(s, axis=-1).astype(v.dtype)
        return jnp.einsum('bkgqs,bskd->bqkgd', p, v)

    o = lax.map(one_block, qb)
    return o.transpose(1, 0, 2, 3, 4, 5).reshape(B, Lq, N_HEADS * HEAD_DIM)


def swiglu(h, w_in, w_out):
    a, b = jnp.split(h @ w_in, 2, axis=-1)
    return (jax.nn.silu(a) * b) @ w_out


def setup_inputs(seed: int = 0) -> dict:
    key = jax.random.key(seed)
    ks = jax.random.split(key, 24)
    nrm = lambda k, shape, s: jax.random.normal(k, shape, jnp.float32) * s
    return {
        "x_prompt": nrm(ks[0], (BATCH, SEQ, D_MODEL), 1.0),
        "x_sample": nrm(ks[1], (DEC_BATCH, DEC_SEQ, D_MODEL), 1.0),
        "cache_k": nrm(ks[2], (DEC_BATCH, N_ODD, PAST_LEN, N_KV_HEADS, HEAD_DIM), 1.0),
        "cache_v": nrm(ks[3], (DEC_BATCH, N_ODD, PAST_LEN, N_KV_HEADS, HEAD_DIM), 1.0),
        "c": nrm(ks[4], (DEC_BATCH, D_MODEL), 1.0),
        "c_ctx": nrm(ks[5], (D_MODEL,), 1.0),
        "w_ada": nrm(ks[6], (DEPTH, D_MODEL, 6 * D_MODEL), 0.5 * D_MODEL ** -0.5),
        "b_ada": nrm(ks[7], (DEPTH, 6 * D_MODEL), 0.02),
        "norm_g": 1.0 + nrm(ks[8], (DEPTH, 2, D_MODEL), 0.02),
        "w_in_even": nrm(ks[9], (N_EVEN, D_MODEL, EVEN_IN), D_MODEL ** -0.5),
        "pool_w": nrm(ks[10], (N_EVEN, POOL_GROUPS, POOL_GROUP_DIM, POOL_GROUP_DIM), POOL_GROUP_DIM ** -0.5),
        "pool_scale": 1.0 + nrm(ks[11], (N_EVEN, POOL_WIDTH), 0.02),
        "conv_w": nrm(ks[12], (N_EVEN, CONV_K, CONV_WIDTH), CONV_K ** -0.5),
        "conv_b": nrm(ks[13], (N_EVEN, CONV_WIDTH), 0.02),
        "w_out_even": nrm(ks[14], (N_EVEN, EVEN_OUT, D_MODEL), EVEN_OUT ** -0.5),
        "w_qkv": nrm(ks[15], (N_ODD, D_MODEL, QKV_WIDTH), D_MODEL ** -0.5),
        "q_gain": 1.0 + nrm(ks[16], (N_ODD, HEAD_DIM), 0.02),
        "k_gain": 1.0 + nrm(ks[17], (N_ODD, HEAD_DIM), 0.02),
        "w_o": nrm(ks[18], (N_ODD, N_HEADS * HEAD_DIM, D_MODEL), (N_HEADS * HEAD_DIM) ** -0.5),
        "w_ffn_in": nrm(ks[19], (DEPTH, D_MODEL, 2 * D_FF), D_MODEL ** -0.5),
        "w_ffn_out": nrm(ks[20], (DEPTH, D_FF, D_MODEL), D_FF ** -0.5),
        "final_g": 1.0 + nrm(ks[21], (D_MODEL,), 0.02),
    }


def reference(x_prompt, x_sample, cache_k, cache_v, c, c_ctx, w_ada, b_ada, norm_g,
              w_in_even, pool_w, pool_scale, conv_w, conv_b, w_out_even,
              w_qkv, q_gain, k_gain, w_o, w_ffn_in, w_ffn_out, final_g):
    x = x_prompt
    ctx_cond = c_ctx[None, :]
    new_k, new_v = [], []
    for l in range(DEPTH):
        i = l // 2
        sh1, sc1, g1, sh2, sc2, g2 = ada_params(ctx_cond, w_ada[l], b_ada[l])
        h = modulate(x, norm_g[l, 0], sh1, sc1)
        if l % 2 == 0:
            y = even_mixer(h, w_in_even[i], pool_w[i], pool_scale[i], conv_w[i], conv_b[i], w_out_even[i])
        else:
            q, k, v = qkv_proj(h, w_qkv[i], q_gain[i], k_gain[i])
            y = blocked_attention(q, k, v) @ w_o[i]
            new_k.append(k)
            new_v.append(v)
        x = x + g1 * y
        h = modulate(x, norm_g[l, 1], sh2, sc2)
        x = x + g2 * swiglu(h, w_ffn_in[l], w_ffn_out[l])
    y_prompt = rms_norm(x, final_g)
    new_cache_k = jnp.stack(new_k, axis=1)
    new_cache_v = jnp.stack(new_v, axis=1)

    x = x_sample
    cos, sin = rope_2d_tables(x_sample.shape[1])
    for l in range(DEPTH):
        i = l // 2
        sh1, sc1, g1, sh2, sc2, g2 = ada_params(c, w_ada[l], b_ada[l])
        h = modulate(x, norm_g[l, 0], sh1, sc1)
        if l % 2 == 0:
            y = even_mixer(h, w_in_even[i], pool_w[i], pool_scale[i], conv_w[i], conv_b[i], w_out_even[i])
        else:
            q, k, v = qkv_proj(h, w_qkv[i], q_gain[i], k_gain[i])
            q = apply_rope_2d(q, cos, sin)
            k = apply_rope_2d(k, cos, sin)
            k_all = jnp.concatenate([cache_k[:, i].astype(k.dtype), k], axis=1)
            v_all = jnp.concatenate([cache_v[:, i].astype(v.dtype), v], axis=1)
            y = blocked_attention(q, k_all, v_all) @ w_o[i]
        x = x + g1 * y
        h = modulate(x, norm_g[l, 1], sh2, sc2)
        x = x + g2 * swiglu(h, w_ffn_in[l], w_ffn_out[l])
    y_sample = rms_norm(x, final_g)

    return (y_prompt, y_sample, new_cache_k, new_cache_v)
```

```python
import functools

import jax
import jax.numpy as jnp
from jax import lax
from jax.experimental import pallas as pl
from jax.experimental.pallas import tpu as pltpu

D_MODEL = 1024
DEPTH = 2
GRID_W = 64
EPS = 1e-6
POOL_WIDTH = D_MODEL // 2
POOL_GROUPS = 4
POOL_GROUP_DIM = POOL_WIDTH // POOL_GROUPS
POOL_WINDOWS = (2, 4, 8, 16)
CONV_WIDTH = D_MODEL // 2
EVEN_IN = POOL_WIDTH + 3 * CONV_WIDTH
HEAD_DIM = 128
N_HEADS = D_MODEL // HEAD_DIM
N_KV_HEADS = N_HEADS // 4
KV_WIDTH = N_KV_HEADS * HEAD_DIM
QKV_WIDTH = (N_HEADS + 2 * N_KV_HEADS) * HEAD_DIM
ROPE_THETA = 10000.0
AXIS_DIM = HEAD_DIM // 2
D_FF = ((8 * D_MODEL // 3 + 255) // 256) * 256
SM_SCALE = HEAD_DIM ** -0.5

COND_ROWS = 8
HALO = 16
ADA_BLOCK_N = 1536
FF_CHUNKS = ((0, 1024), (1024, 1024), (2048, 768))
VMEM_LIMIT_BYTES = 56 * 1024 * 1024

BF16 = jnp.bfloat16
F32 = jnp.float32


def _dot(a, b):
    return jnp.dot(a, b, preferred_element_type=F32)


def _dot_nt(a, b):
    return lax.dot_general(a, b, (((1,), (1,)), ((), ())), preferred_element_type=F32)


def _silu(a):
    return a * (1.0 / (1.0 + jnp.exp(-a)))


def _rms(x):
    return x * lax.rsqrt(jnp.mean(x * x, axis=-1, keepdims=True) + EPS)


def _modulate(x, g, shift, scale):
    return (_rms(x) * g) * (1.0 + scale) + shift


def _params(sem):
    return pltpu.CompilerParams(dimension_semantics=sem, vmem_limit_bytes=VMEM_LIMIT_BYTES)


def _const_spec(shape):
    n = len(shape)
    return pl.BlockSpec(shape, lambda *_: (0,) * n, pipeline_mode=pl.Buffered(1))


def _ada_kernel(cond_ref, w_ref, b_ref, o_ref):
    s = _silu(cond_ref[...]).astype(BF16)
    o_ref[...] = _dot(s, w_ref[...].astype(BF16)) + b_ref[...]


def _ada(cond, w_ada, b_ada):
    n_out = w_ada.shape[-1]
    return pl.pallas_call(
        _ada_kernel,
        out_shape=jax.ShapeDtypeStruct((DEPTH, COND_ROWS, n_out), F32),
        grid=(DEPTH, n_out // ADA_BLOCK_N),
        in_specs=[
            pl.BlockSpec((COND_ROWS, D_MODEL), lambda l, j: (0, 0)),
            pl.BlockSpec((None, D_MODEL, ADA_BLOCK_N), lambda l, j: (l, 0, j)),
            pl.BlockSpec((None, 1, ADA_BLOCK_N), lambda l, j: (l, 0, j)),
        ],
        out_specs=pl.BlockSpec((None, COND_ROWS, ADA_BLOCK_N), lambda l, j: (l, 0, j)),
        compiler_params=_params(("arbitrary", "arbitrary")),
        name="ada",
    )(cond, w_ada, b_ada.reshape(DEPTH, 1, n_out))


def _mod_spec(layer, row0, row_stride):
    return pl.BlockSpec((None, None, 6, D_MODEL),
                        lambda s, t: (layer, row0 + row_stride * s, 0, 0))


def _even_kernel(x_ref, xp_ref, xn_ref, mod_ref, g_ref, w_in_ref, pool_w_ref, pool_scale_ref,
                 conv_w_ref, conv_b_ref, w_out_ref, o_ref, *, tile, seq_len):
    t = pl.program_id(1)
    n_t = pl.num_programs(1)
    ext = tile + 2 * HALO
    mod = mod_ref[...]
    shift, scale, gate = mod[0:1], mod[1:2], mod[2:3]
    g = g_ref[...]
    x = x_ref[...]
    h = jnp.concatenate(
        [_modulate(xp_ref[...], g, shift, scale).astype(BF16),
         _modulate(x, g, shift, scale).astype(BF16),
         _modulate(xn_ref[...], g, shift, scale).astype(BF16)], axis=0)
    p = _dot(h, w_in_ref[...])

    row = lax.broadcasted_iota(jnp.int32, (ext, 1), 0)
    first_valid = jnp.where(t > 0, 0, HALO)
    end_valid = jnp.where(t < n_t - 1, ext, HALO + tile)
    valid = jnp.logical_and(row >= first_valid, row < end_valid)
    pos = t * tile + row[HALO:HALO + tile] - HALO

    def shifted(a, s):
        return pltpu.roll(a, s % ext, 0)

    u = jnp.where(valid, p[:, :POOL_WIDTH], 0.0)
    ya = []
    for i, w in enumerate(POOL_WINDOWS):
        ug = u[:, i * POOL_GROUP_DIM:(i + 1) * POOL_GROUP_DIM]
        win = ug + shifted(ug, 1)
        half = 1
        while 2 * half < w:
            win = shifted(win, half) + shifted(win, -half)
            half *= 2
        cnt = (jnp.clip(pos + w // 2, 0, seq_len) - jnp.clip(pos - w // 2, 0, seq_len)).astype(F32)
        pooled = win[HALO:HALO + tile] / cnt - ug[HALO:HALO + tile]
        ya.append(_dot(pooled.astype(BF16), pool_w_ref[i]))
    ya = jnp.concatenate(ya, axis=-1) * pool_scale_ref[...]

    c0 = POOL_WIDTH
    bg = p[HALO:HALO + tile, c0:c0 + CONV_WIDTH]
    z = jnp.where(valid, p[:, c0 + CONV_WIDTH:c0 + 2 * CONV_WIDTH] * p[:, c0 + 2 * CONV_WIDTH:], 0.0)
    cw = conv_w_ref[...]
    conv = (shifted(z, 1) * cw[0:1] + z * cw[1:2] + shifted(z, -1) * cw[2:3])[HALO:HALO + tile]
    yb = bg * (conv + conv_b_ref[...])

    y = _dot(jnp.concatenate([ya, yb], axis=-1).astype(BF16), w_out_ref[...])
    o_ref[...] = x + gate * y


def _even_sublayer(x, mod, layer, row0, row_stride, g, w_in, pool_w, pool_scale, conv_w, conv_b,
                   w_out, tile):
    n_seq, seq_len, _ = x.shape
    n_t = seq_len // tile
    hb = tile // HALO
    n_hb = seq_len // HALO
    kern = functools.partial(_even_kernel, tile=tile, seq_len=seq_len)
    return pl.pallas_call(
        kern,
        out_shape=jax.ShapeDtypeStruct(x.shape, F32),
        grid=(n_seq, n_t),
        in_specs=[
            pl.BlockSpec((None, tile, D_MODEL), lambda s, t: (s, t, 0)),
            pl.BlockSpec((None, HALO, D_MODEL), lambda s, t: (s, jnp.maximum(t * hb - 1, 0), 0)),
            pl.BlockSpec((None, HALO, D_MODEL),
                         lambda s, t: (s, jnp.minimum((t + 1) * hb, n_hb - 1), 0)),
            _mod_spec(layer, row0, row_stride),
            _const_spec((1, D_MODEL)),
            _const_spec((D_MODEL, EVEN_IN)),
            _const_spec((POOL_GROUPS, POOL_GROUP_DIM, POOL_GROUP_DIM)),
            _const_spec((1, POOL_WIDTH)),
            _const_spec((3, CONV_WIDTH)),
            _const_spec((1, CONV_WIDTH)),
            _const_spec((POOL_WIDTH + CONV_WIDTH, D_MODEL)),
        ],
        out_specs=pl.BlockSpec((None, tile, D_MODEL), lambda s, t: (s, t, 0)),
        compiler_params=_params(("arbitrary", "arbitrary")),
        name="even_mixer",
    )(x, x, x, mod, g, w_in, pool_w, pool_scale, conv_w, conv_b, w_out)


def _ffn_kernel(x_ref, mod_ref, g_ref, w_in_ref, w_out_ref, fg_ref, o_ref, *, final_norm):
    mod = mod_ref[...]
    shift, scale, gate = mod[3:4], mod[4:5], mod[5:6]
    x = x_ref[...]
    h = _modulate(x, g_ref[...], shift, scale).astype(BF16)
    acc = None
    for c0, cw in FF_CHUNKS:
        a = _dot(h, w_in_ref[:, c0:c0 + cw])
        b = _dot(h, w_in_ref[:, D_FF + c0:D_FF + c0 + cw])
        part = _dot((_silu(a) * b).astype(BF16), w_out_ref[c0:c0 + cw, :])
        acc = part if acc is None else acc + part
    out = x + gate * acc
    if final_norm:
        out = _rms(out) * fg_ref[...]
    o_ref[...] = out


def _ffn_sublayer(x, mod, layer, row0, row_stride, g, w_in, w_out, final_g, final_norm, tile):
    n_seq, seq_len, _ = x.shape
    kern = functools.partial(_ffn_kernel, final_norm=final_norm)
    return pl.pallas_call(
        kern,
        out_shape=jax.ShapeDtypeStruct(x.shape, F32),
        grid=(n_seq, seq_len // tile),
        in_specs=[
            pl.BlockSpec((None, tile, D_MODEL), lambda s, t: (s, t, 0)),
            _mod_spec(layer, row0, row_stride),
            _const_spec((1, D_MODEL)),
            _const_spec((D_MODEL, 2 * D_FF)),
            _const_spec((D_FF, D_MODEL)),
            _const_spec((1, D_MODEL)),
        ],
        out_specs=pl.BlockSpec((None, tile, D_MODEL), lambda s, t: (s, t, 0)),
        compiler_params=_params(("arbitrary", "arbitrary")),
        name="ffn",
    )(x, mod, g, w_in, w_out, final_g)


def _rope(a, cos, sin_lo, sin_hi):
    return (a * cos + pltpu.roll(a, HEAD_DIM - AXIS_DIM // 2, 1) * sin_lo
            + pltpu.roll(a, AXIS_DIM // 2, 1) * sin_hi)


def _qkv_kernel(x_ref, mod_ref, g_ref, w_ref, qg_ref, kg_ref, *rest, rope):
    if rope:
        cos_ref, sin_lo_ref, sin_hi_ref, q_ref, k_ref, v_ref = rest
        cos, sin_lo, sin_hi = cos_ref[...], sin_lo_ref[...], sin_hi_ref[...]
    else:
        q_ref, k_ref, v_ref = rest
    mod = mod_ref[...]
    shift, scale = mod[0:1], mod[1:2]
    h = _modulate(x_ref[...], g_ref[...], shift, scale).astype(BF16)
    p = _dot(h, w_ref[...])
    qg, kg = qg_ref[...], kg_ref[...]
    for hd in range(N_HEADS):
        a = _rms(p[:, hd * HEAD_DIM:(hd + 1) * HEAD_DIM]) * qg
        if rope:
            a = _rope(a, cos, sin_lo, sin_hi)
        q_ref[:, hd * HEAD_DIM:(hd + 1) * HEAD_DIM] = (a * SM_SCALE).astype(q_ref.dtype)
    k0 = N_HEADS * HEAD_DIM
    for hd in range(N_KV_HEADS):
        a = _rms(p[:, k0 + hd * HEAD_DIM:k0 + (hd + 1) * HEAD_DIM]) * kg
        if rope:
            a = _rope(a, cos, sin_lo, sin_hi)
        k_ref[:, hd * HEAD_DIM:(hd + 1) * HEAD_DIM] = a.astype(k_ref.dtype)
    v_ref[...] = p[:, k0 + KV_WIDTH:].astype(v_ref.dtype)


def _qkv_proj(x, mod, layer, row0, row_stride, g, w_qkv, q_gain, k_gain, rope_tables, kv_dtype,
              tile):
    n_seq, seq_len, _ = x.shape
    rope = rope_tables is not None
    in_specs = [
        pl.BlockSpec((None, tile, D_MODEL), lambda s, t: (s, t, 0)),
        _mod_spec(layer, row0, row_stride),
        _const_spec((1, D_MODEL)),
        _const_spec((D_MODEL, QKV_WIDTH)),
        _const_spec((1, HEAD_DIM)),
        _const_spec((1, HEAD_DIM)),
    ]
    args = [x, mod, g, w_qkv, q_gain, k_gain]
    if rope:
        in_specs += [pl.BlockSpec((tile, HEAD_DIM), lambda s, t: (t, 0))] * 3
        args += list(rope_tables)
    return pl.pallas_call(
        functools.partial(_qkv_kernel, rope=rope),
        out_shape=(jax.ShapeDtypeStruct((n_seq, seq_len, D_MODEL), BF16),
                   jax.ShapeDtypeStruct((n_seq, seq_len, KV_WIDTH), kv_dtype),
                   jax.ShapeDtypeStruct((n_seq, seq_len, KV_WIDTH), kv_dtype)),
        grid=(n_seq, seq_len // tile),
        in_specs=in_specs,
        out_specs=(pl.BlockSpec((None, tile, D_MODEL), lambda s, t: (s, t, 0)),
                   pl.BlockSpec((None, tile, KV_WIDTH), lambda s, t: (s, t, 0)),
                   pl.BlockSpec((None, tile, KV_WIDTH), lambda s, t: (s, t, 0))),
        compiler_params=_params(("arbitrary", "arbitrary")),
        name="qkv_proj",
    )(*args)


def _attn_kernel(q_ref, k_ref, v_ref, *rest, cached):
    if cached:
        ck_ref, cv_ref, x_ref, mod_ref, w_o_ref, o_ref = rest
        ck = ck_ref[...].astype(BF16)
        cv = cv_ref[...].astype(BF16)
    else:
        x_ref, mod_ref, w_o_ref, o_ref = rest
    k = k_ref[...].astype(BF16)
    v = v_ref[...].astype(BF16)
    outs = []
    for hd in range(N_HEADS):
        kv = hd // (N_HEADS // N_KV_HEADS)
        cols = slice(kv * HEAD_DIM, (kv + 1) * HEAD_DIM)
        qh = q_ref[:, hd * HEAD_DIM:(hd + 1) * HEAD_DIM]
        s_new = _dot_nt(qh, k[:, cols])
        m = jnp.max(s_new, axis=-1, keepdims=True)
        if cached:
            s_old = _dot_nt(qh, ck[:, cols])
            m = jnp.maximum(m, jnp.max(s_old, axis=-1, keepdims=True))
            p_old = jnp.exp(s_old - m)
        p_new = jnp.exp(s_new - m)
        denom = jnp.sum(p_new, axis=-1, keepdims=True)
        o = _dot(p_new.astype(BF16), v[:, cols])
        if cached:
            denom = denom + jnp.sum(p_old, axis=-1, keepdims=True)
            o = o + _dot(p_old.astype(BF16), cv[:, cols])
        outs.append((o / denom).astype(BF16))
    y = _dot(jnp.concatenate(outs, axis=-1), w_o_ref[...])
    o_ref[...] = x_ref[...] + mod_ref[2:3] * y


def _attn_sublayer(q, k, v, cache, x, mod, layer, row0, row_stride, w_o, tile):
    n_seq, seq_len, _ = x.shape
    cached = cache is not None
    in_specs = [
        pl.BlockSpec((None, tile, D_MODEL), lambda s, t: (s, t, 0)),
        pl.BlockSpec((None, seq_len, KV_WIDTH), lambda s, t: (s, 0, 0)),
        pl.BlockSpec((None, seq_len, KV_WIDTH), lambda s, t: (s, 0, 0)),
    ]
    args = [q, k, v]
    if cached:
        past = cache[0].shape[1]
        in_specs += [pl.BlockSpec((None, past, KV_WIDTH), lambda s, t: (s, 0, 0))] * 2
        args += list(cache)
    in_specs += [
        pl.BlockSpec((None, tile, D_MODEL), lambda s, t: (s, t, 0)),
        _mod_spec(layer, row0, row_stride),
        _const_spec((D_MODEL, D_MODEL)),
    ]
    args += [x, mod, w_o]
    return pl.pallas_call(
        functools.partial(_attn_kernel, cached=cached),
        out_shape=jax.ShapeDtypeStruct(x.shape, F32),
        grid=(n_seq, seq_len // tile),
        in_specs=in_specs,
        out_specs=pl.BlockSpec((None, tile, D_MODEL), lambda s, t: (s, t, 0)),
        compiler_params=_params(("arbitrary", "arbitrary")),
        name="attention",
    )(*args)


def _rope_tables(seq_len):
    rows = seq_len // GRID_W
    r = jnp.repeat(jnp.arange(rows), GRID_W).astype(F32)
    col = jnp.tile(jnp.arange(GRID_W), rows).astype(F32)
    inv = ROPE_THETA ** (-jnp.arange(0, AXIS_DIM, 2, dtype=F32) / AXIS_DIM)
    ar = r[:, None] * inv
    ac = col[:, None] * inv
    ang = jnp.concatenate([ar, ar, ac, ac], axis=-1)
    cos, sin = jnp.cos(ang), jnp.sin(ang)
    lo = (jnp.arange(HEAD_DIM) % AXIS_DIM) < AXIS_DIM // 2
    return cos, jnp.where(lo, -sin, 0.0), jnp.where(lo, 0.0, sin)


def _run_group(x, mod, row0, row_stride, wts, cache, rope_tables, kv_dtype, tile, attn_tile):
    new_kv = []
    for l in range(DEPTH):
        i = l // 2
        g1 = wts["norm_g"][l, 0].reshape(1, D_MODEL)
        g2 = wts["norm_g"][l, 1].reshape(1, D_MODEL)
        if l % 2 == 0:
            x = _even_sublayer(x, mod, l, row0, row_stride, g1, wts["w_in_even"][i],
                               wts["pool_w"][i], wts["pool_scale"][i].reshape(1, POOL_WIDTH),
                               wts["conv_w"][i], wts["conv_b"][i].reshape(1, CONV_WIDTH),
                               wts["w_out_even"][i], tile)
        else:
            q, k, v = _qkv_proj(x, mod, l, row0, row_stride, g1, wts["w_qkv"][i],
                                wts["q_gain"][i].reshape(1, HEAD_DIM),
                                wts["k_gain"][i].reshape(1, HEAD_DIM), rope_tables, kv_dtype, tile)
            new_kv.append((k, v))
            layer_cache = None if cache is None else (cache[0][:, i], cache[1][:, i])
            x = _attn_sublayer(q, k, v, layer_cache, x, mod, l, row0, row_stride, wts["w_o"][i],
                               attn_tile)
        x = _ffn_sublayer(x, mod, l, row0, row_stride, g2, wts["w_ffn_in"][l], wts["w_ffn_out"][l],
                          wts["final_g"].reshape(1, D_MODEL), l == DEPTH - 1, tile)
    return x, new_kv


def kernel(x_prompt, x_sample, cache_k, cache_v, c, c_ctx, w_ada, b_ada, norm_g, w_in_even, pool_w,
           pool_scale, conv_w, conv_b, w_out_even, w_qkv, q_gain, k_gain, w_o, w_ffn_in, w_ffn_out,
           final_g):
    n_ctx, ctx_len, _ = x_prompt.shape
    n_lat, lat_len, _ = x_sample.shape
    assert 1 + n_lat <= COND_ROWS

    cond = jnp.concatenate(
        [c_ctx[None, :], c, jnp.zeros((COND_ROWS - 1 - n_lat, D_MODEL), F32)], axis=0)
    mod = _ada(cond, w_ada, b_ada).reshape(DEPTH, COND_ROWS, 6, D_MODEL)

    wts = dict(
        norm_g=norm_g, final_g=final_g, pool_scale=pool_scale, conv_w=conv_w, conv_b=conv_b,
        q_gain=q_gain, k_gain=k_gain,
        w_in_even=w_in_even.astype(BF16), pool_w=pool_w.astype(BF16),
        w_out_even=w_out_even.astype(BF16), w_qkv=w_qkv.astype(BF16), w_o=w_o.astype(BF16),
        w_ffn_in=w_ffn_in.astype(BF16), w_ffn_out=w_ffn_out.astype(BF16),
    )

    y_prompt, ctx_kv = _run_group(x_prompt, mod, 0, 0, wts, None, None, F32,
                                  tile=ctx_len, attn_tile=ctx_len)
    new_cache_k = jnp.stack([k.reshape(n_ctx, ctx_len, N_KV_HEADS, HEAD_DIM) for k, _ in ctx_kv], 1)
    new_cache_v = jnp.stack([v.reshape(n_ctx, ctx_len, N_KV_HEADS, HEAD_DIM) for _, v in ctx_kv], 1)

    past = cache_k.shape[2]
    cache = (cache_k.reshape(n_lat, DEPTH // 2, past, KV_WIDTH),
             cache_v.reshape(n_lat, DEPTH // 2, past, KV_WIDTH))
    y_sample, _ = _run_group(x_sample, mod, 1, 1, wts, cache, _rope_tables(lat_len), BF16,
                             tile=512, attn_tile=256)
    return y_prompt, y_sample, new_cache_k, new_cache_v
```

```python
import functools
import math

import jax
import jax.numpy as jnp
from jax import lax
from jax.experimental import pallas as pl
from jax.experimental.pallas import tpu as pltpu

D_MODEL = 1024
DEPTH = 2
GRID_W = 64
EPS = 1e-6
POOL_WIDTH = D_MODEL // 2
POOL_GROUPS = 4
POOL_GROUP_DIM = POOL_WIDTH // POOL_GROUPS
POOL_WINDOWS = (2, 4, 8, 16)
CONV_WIDTH = D_MODEL // 2
EVEN_IN = POOL_WIDTH + 3 * CONV_WIDTH
HEAD_DIM = 128
N_HEADS = D_MODEL // HEAD_DIM
N_KV_HEADS = N_HEADS // 4
KV_WIDTH = N_KV_HEADS * HEAD_DIM
QKV_WIDTH = (N_HEADS + 2 * N_KV_HEADS) * HEAD_DIM
ROPE_THETA = 10000.0
AXIS_DIM = HEAD_DIM // 2
D_FF = ((8 * D_MODEL // 3 + 255) // 256) * 256
SM_SCALE = HEAD_DIM ** -0.5
LOG2E = math.log2(math.e)

COND_ROWS = 8
HALO = 16
ADA_BLOCK_N = 1536
FF_CHUNKS = ((0, 1024), (1024, 1024), (2048, 768))
VMEM_LIMIT_BYTES = 56 * 1024 * 1024

BF16 = jnp.bfloat16
F32 = jnp.float32


def _dot(a, b):
    return jnp.dot(a, b, preferred_element_type=F32)


def _dot_nt(a, b):
    return lax.dot_general(a, b, (((1,), (1,)), ((), ())), preferred_element_type=F32)


def _silu(a):
    return a * (1.0 / (1.0 + jnp.exp(-a)))


def _rms(x):
    return x * lax.rsqrt(jnp.mean(x * x, axis=-1, keepdims=True) + EPS)


def _modulate(x, g, shift, scale):
    return (_rms(x) * g) * (1.0 + scale) + shift


def _params(sem):
    return pltpu.CompilerParams(dimension_semantics=sem, vmem_limit_bytes=VMEM_LIMIT_BYTES)


def _const_spec(shape):
    n = len(shape)
    return pl.BlockSpec(shape, lambda *_: (0,) * n, pipeline_mode=pl.Buffered(1))


def _ada_kernel(cond_ref, w_ref, b_ref, o_ref):
    s = _silu(cond_ref[...]).astype(BF16)
    o_ref[...] = _dot(s, w_ref[...].astype(BF16)) + b_ref[...]


def _ada(cond, w_ada, b_ada):
    n_out = w_ada.shape[-1]
    return pl.pallas_call(
        _ada_kernel,
        out_shape=jax.ShapeDtypeStruct((DEPTH, COND_ROWS, n_out), F32),
        grid=(DEPTH, n_out // ADA_BLOCK_N),
        in_specs=[
            pl.BlockSpec((COND_ROWS, D_MODEL), lambda l, j: (0, 0)),
            pl.BlockSpec((None, D_MODEL, ADA_BLOCK_N), lambda l, j: (l, 0, j)),
            pl.BlockSpec((None, 1, ADA_BLOCK_N), lambda l, j: (l, 0, j)),
        ],
        out_specs=pl.BlockSpec((None, COND_ROWS, ADA_BLOCK_N), lambda l, j: (l, 0, j)),
        compiler_params=_params(("arbitrary", "arbitrary")),
        name="ada",
    )(cond, w_ada, b_ada.reshape(DEPTH, 1, n_out))


def _mod_spec(layer, row0, row_stride):
    return pl.BlockSpec((None, None, 6, D_MODEL),
                        lambda s, t: (layer, row0 + row_stride * s, 0, 0))


def _even_kernel(x_ref, xp_ref, xn_ref, mod_ref, g_ref, w_in_ref, pool_w_ref, pool_scale_ref,
                 conv_w_ref, conv_b_ref, w_out_ref, o_ref, *, tile, seq_len):
    t = pl.program_id(1)
    n_t = pl.num_programs(1)
    ext = tile + 2 * HALO
    mod = mod_ref[...]
    shift, scale, gate = mod[0:1], mod[1:2], mod[2:3]
    g = g_ref[...]
    x = x_ref[...]
    h = jnp.concatenate(
        [_modulate(xp_ref[...], g, shift, scale).astype(BF16),
         _modulate(x, g, shift, scale).astype(BF16),
         _modulate(xn_ref[...], g, shift, scale).astype(BF16)], axis=0)
    p = _dot(h, w_in_ref[...])

    row = lax.broadcasted_iota(jnp.int32, (ext, 1), 0)
    first_valid = jnp.where(t > 0, 0, HALO)
    end_valid = jnp.where(t < n_t - 1, ext, HALO + tile)
    valid = jnp.logical_and(row >= first_valid, row < end_valid)
    pos = t * tile + row[HALO:HALO + tile] - HALO

    def shifted(a, s):
        return pltpu.roll(a, s % ext, 0)

    u = jnp.where(valid, p[:, :POOL_WIDTH], 0.0)
    ya = []
    for i, w in enumerate(POOL_WINDOWS):
        ug = u[:, i * POOL_GROUP_DIM:(i + 1) * POOL_GROUP_DIM]
        win = ug + shifted(ug, 1)
        half = 1
        while 2 * half < w:
            win = shifted(win, half) + shifted(win, -half)
            half *= 2
        cnt = (jnp.clip(pos + w // 2, 0, seq_len) - jnp.clip(pos - w // 2, 0, seq_len)).astype(F32)
        pooled = win[HALO:HALO + tile] / cnt - ug[HALO:HALO + tile]
        ya.append(_dot(pooled.astype(BF16), pool_w_ref[i]))
    ya = jnp.concatenate(ya, axis=-1) * pool_scale_ref[...]

    c0 = POOL_WIDTH
    bg = p[HALO:HALO + tile, c0:c0 + CONV_WIDTH]
    z = jnp.where(valid, p[:, c0 + CONV_WIDTH:c0 + 2 * CONV_WIDTH] * p[:, c0 + 2 * CONV_WIDTH:], 0.0)
    cw = conv_w_ref[...]
    conv = (shifted(z, 1) * cw[0:1] + z * cw[1:2] + shifted(z, -1) * cw[2:3])[HALO:HALO + tile]
    yb = bg * (conv + conv_b_ref[...])

    y = _dot(jnp.concatenate([ya, yb], axis=-1).astype(BF16), w_out_ref[...])
    o_ref[...] = x + gate * y


def _even_sublayer(x, mod, layer, row0, row_stride, g, w_in, pool_w, pool_scale, conv_w, conv_b,
                   w_out, tile):
    n_seq, seq_len, _ = x.shape
    n_t = seq_len // tile
    hb = tile // HALO
    n_hb = seq_len // HALO
    kern = functools.partial(_even_kernel, tile=tile, seq_len=seq_len)
    return pl.pallas_call(
        kern,
        out_shape=jax.ShapeDtypeStruct(x.shape, F32),
        grid=(n_seq, n_t),
        in_specs=[
            pl.BlockSpec((None, tile, D_MODEL), lambda s, t: (s, t, 0)),
            pl.BlockSpec((None, HALO, D_MODEL), lambda s, t: (s, jnp.maximum(t * hb - 1, 0), 0)),
            pl.BlockSpec((None, HALO, D_MODEL),
                         lambda s, t: (s, jnp.minimum((t + 1) * hb, n_hb - 1), 0)),
            _mod_spec(layer, row0, row_stride),
            _const_spec((1, D_MODEL)),
            _const_spec((D_MODEL, EVEN_IN)),
            _const_spec((POOL_GROUPS, POOL_GROUP_DIM, POOL_GROUP_DIM)),
            _const_spec((1, POOL_WIDTH)),
            _const_spec((3, CONV_WIDTH)),
            _const_spec((1, CONV_WIDTH)),
            _const_spec((POOL_WIDTH + CONV_WIDTH, D_MODEL)),
        ],
        out_specs=pl.BlockSpec((None, tile, D_MODEL), lambda s, t: (s, t, 0)),
        compiler_params=_params(("arbitrary", "arbitrary")),
        name="even_mixer",
    )(x, x, x, mod, g, w_in, pool_w, pool_scale, conv_w, conv_b, w_out)


def _ffn_kernel(x_ref, mod_ref, g_ref, w_in_ref, w_out_ref, fg_ref, o_ref, *, final_norm):
    mod = mod_ref[...]
    shift, scale, gate = mod[3:4], mod[4:5], mod[5:6]
    x = x_ref[...]
    h = _modulate(x, g_ref[...], shift, scale).astype(BF16)
    acc = None
    for c0, cw in FF_CHUNKS:
        a = _dot(h, w_in_ref[:, c0:c0 + cw])
        b = _dot(h, w_in_ref[:, D_FF + c0:D_FF + c0 + cw])
        part = _dot((_silu(a) * b).astype(BF16), w_out_ref[c0:c0 + cw, :])
        acc = part if acc is None else acc + part
    out = x + gate * acc
    if final_norm:
        out = _rms(out) * fg_ref[...]
    o_ref[...] = out


def _ffn_sublayer(x, mod, layer, row0, row_stride, g, w_in, w_out, final_g, final_norm, tile):
    n_seq, seq_len, _ = x.shape
    kern = functools.partial(_ffn_kernel, final_norm=final_norm)
    return pl.pallas_call(
        kern,
        out_shape=jax.ShapeDtypeStruct(x.shape, F32),
        grid=(n_seq, seq_len // tile),
        in_specs=[
            pl.BlockSpec((None, tile, D_MODEL), lambda s, t: (s, t, 0)),
            _mod_spec(layer, row0, row_stride),
            _const_spec((1, D_MODEL)),
            _const_spec((D_MODEL, 2 * D_FF)),
            _const_spec((D_FF, D_MODEL)),
            _const_spec((1, D_MODEL)),
        ],
        out_specs=pl.BlockSpec((None, tile, D_MODEL), lambda s, t: (s, t, 0)),
        compiler_params=_params(("arbitrary", "arbitrary")),
        name="ffn",
    )(x, mod, g, w_in, w_out, final_g)


def _rope(a, cos, sin_lo, sin_hi):
    return (a * cos + pltpu.roll(a, HEAD_DIM - AXIS_DIM // 2, 1) * sin_lo
            + pltpu.roll(a, AXIS_DIM // 2, 1) * sin_hi)


def _qkv_kernel(*refs, rope, cached, keep_f32):
    it = iter(refs)
    x_ref, mod_ref, g_ref, w_ref, qg_ref, kg_ref = (next(it) for _ in range(6))
    rope_refs = [next(it) for _ in range(3)] if rope else None
    cache_refs = [next(it) for _ in range(2)] if cached else None
    q_ref, kh_ref, vt_ref = (next(it) for _ in range(3))
    f32_refs = [next(it) for _ in range(2)] if keep_f32 else None

    def project():
        mod = mod_ref[...]
        shift, scale = mod[0:1], mod[1:2]
        h = _modulate(x_ref[...], g_ref[...], shift, scale).astype(BF16)
        p = _dot(h, w_ref[...])
        if rope:
            tables = [r[...] for r in rope_refs]
        qg, kg = qg_ref[...], kg_ref[...]
        for hd in range(N_HEADS):
            a = _rms(p[:, hd * HEAD_DIM:(hd + 1) * HEAD_DIM]) * qg
            if rope:
                a = _rope(a, *tables)
            q_ref[hd] = (a * (SM_SCALE * LOG2E)).astype(BF16)
        k0 = N_HEADS * HEAD_DIM
        v0 = k0 + KV_WIDTH
        for kv in range(N_KV_HEADS):
            a = _rms(p[:, k0 + kv * HEAD_DIM:k0 + (kv + 1) * HEAD_DIM]) * kg
            if rope:
                a = _rope(a, *tables)
            v = p[:, v0 + kv * HEAD_DIM:v0 + (kv + 1) * HEAD_DIM]
            kh_ref[kv] = a.astype(BF16)
            vt_ref[kv] = v.T.astype(BF16)
            if keep_f32:
                f32_refs[0][:, kv * HEAD_DIM:(kv + 1) * HEAD_DIM] = a
                f32_refs[1][:, kv * HEAD_DIM:(kv + 1) * HEAD_DIM] = v

    if not cached:
        project()
        return

    t = pl.program_id(1)

    @pl.when(t == 0)
    def _():
        ck, cv = cache_refs[0][...], cache_refs[1][...]
        for kv in range(N_KV_HEADS):
            kh_ref[kv] = ck[:, kv * HEAD_DIM:(kv + 1) * HEAD_DIM].astype(BF16)
            vt_ref[kv] = cv[:, kv * HEAD_DIM:(kv + 1) * HEAD_DIM].T.astype(BF16)

    pl.when(t > 0)(project)


def _qkv_proj(x, mod, layer, row0, row_stride, g, w_qkv, q_gain, k_gain, rope_tables, cache,
              keep_f32, tile):
    n_seq, seq_len, _ = x.shape
    rope = rope_tables is not None
    cached = cache is not None
    lead = 0
    if cached:
        assert cache[0].shape[1] == tile
        lead = 1
    n_keys = seq_len + lead * tile
    tok = lambda t: jnp.maximum(t - lead, 0)
    in_specs = [
        pl.BlockSpec((None, tile, D_MODEL), lambda s, t: (s, tok(t), 0)),
        _mod_spec(layer, row0, row_stride),
        _const_spec((1, D_MODEL)),
        _const_spec((D_MODEL, QKV_WIDTH)),
        _const_spec((1, HEAD_DIM)),
        _const_spec((1, HEAD_DIM)),
    ]
    args = [x, mod, g, w_qkv, q_gain, k_gain]
    if rope:
        in_specs += [pl.BlockSpec((tile, HEAD_DIM), lambda s, t: (tok(t), 0))] * 3
        args += list(rope_tables)
    if cached:
        in_specs += [pl.BlockSpec((None, tile, KV_WIDTH), lambda s, t: (s, 0, 0))] * 2
        args += list(cache)
    out_shape = [jax.ShapeDtypeStruct((n_seq, N_HEADS, seq_len, HEAD_DIM), BF16),
                 jax.ShapeDtypeStruct((n_seq, N_KV_HEADS, n_keys, HEAD_DIM), BF16),
                 jax.ShapeDtypeStruct((n_seq, N_KV_HEADS, HEAD_DIM, n_keys), BF16)]
    out_specs = [pl.BlockSpec((None, N_HEADS, tile, HEAD_DIM), lambda s, t: (s, 0, tok(t), 0)),
                 pl.BlockSpec((None, N_KV_HEADS, tile, HEAD_DIM), lambda s, t: (s, 0, t, 0)),
                 pl.BlockSpec((None, N_KV_HEADS, HEAD_DIM, tile), lambda s, t: (s, 0, 0, t))]
    if keep_f32:
        assert not cached
        out_shape += [jax.ShapeDtypeStruct((n_seq, seq_len, KV_WIDTH), F32)] * 2
        out_specs += [pl.BlockSpec((None, tile, KV_WIDTH), lambda s, t: (s, t, 0))] * 2
    return pl.pallas_call(
        functools.partial(_qkv_kernel, rope=rope, cached=cached, keep_f32=keep_f32),
        out_shape=tuple(out_shape),
        grid=(n_seq, seq_len // tile + lead),
        in_specs=in_specs,
        out_specs=tuple(out_specs),
        compiler_params=_params(("arbitrary", "arbitrary")),
        name="qkv_proj",
    )(*args)


def _attn_kernel(q_ref, kh_ref, vt_ref, x_ref, mod_ref, w_o_ref, o_ref, s_scr):
    group = N_HEADS // N_KV_HEADS

    def scores(hd):
        s = _dot_nt(kh_ref[hd // group], q_ref[hd])
        s_scr[hd % 2] = s
        return jnp.max(s, axis=0, keepdims=True)

    outs = []
    m = scores(0)
    for hd in range(N_HEADS):
        m_next = scores(hd + 1) if hd + 1 < N_HEADS else None
        p = jnp.exp2(s_scr[hd % 2] - m)
        denom = jnp.sum(p, axis=0, keepdims=True)
        o_t = _dot(vt_ref[hd // group], p.astype(BF16)) * (1.0 / denom)
        outs.append(o_t.T.astype(BF16))
        m = m_next
    y = _dot(jnp.concatenate(outs, axis=-1), w_o_ref[...])
    o_ref[...] = x_ref[...] + mod_ref[2:3] * y


def _attn_sublayer(q, kh, vt, x, mod, layer, row0, row_stride, w_o, tile):
    n_seq, seq_len, _ = x.shape
    n_keys = kh.shape[2]
    return pl.pallas_call(
        _attn_kernel,
        out_shape=jax.ShapeDtypeStruct(x.shape, F32),
        grid=(n_seq, seq_len // tile),
        in_specs=[
            pl.BlockSpec((None, N_HEADS, tile, HEAD_DIM), lambda s, t: (s, 0, t, 0)),
            pl.BlockSpec((None, N_KV_HEADS, n_keys, HEAD_DIM), lambda s, t: (s, 0, 0, 0)),
            pl.BlockSpec((None, N_KV_HEADS, HEAD_DIM, n_keys), lambda s, t: (s, 0, 0, 0)),
            pl.BlockSpec((None, tile, D_MODEL), lambda s, t: (s, t, 0)),
            _mod_spec(layer, row0, row_stride),
            _const_spec((D_MODEL, D_MODEL)),
        ],
        out_specs=pl.BlockSpec((None, tile, D_MODEL), lambda s, t: (s, t, 0)),
        scratch_shapes=[pltpu.VMEM((2, n_keys, tile), F32)],
        compiler_params=_params(("arbitrary", "arbitrary")),
        name="attention",
    )(q, kh, vt, x, mod, w_o)


def _rope_tables(seq_len):
    rows = seq_len // GRID_W
    r = jnp.repeat(jnp.arange(rows), GRID_W).astype(F32)
    col = jnp.tile(jnp.arange(GRID_W), rows).astype(F32)
    inv = ROPE_THETA ** (-jnp.arange(0, AXIS_DIM, 2, dtype=F32) / AXIS_DIM)
    ar = r[:, None] * inv
    ac = col[:, None] * inv
    ang = jnp.concatenate([ar, ar, ac, ac], axis=-1)
    cos, sin = jnp.cos(ang), jnp.sin(ang)
    lo = (jnp.arange(HEAD_DIM) % AXIS_DIM) < AXIS_DIM // 2
    return cos, jnp.where(lo, -sin, 0.0), jnp.where(lo, 0.0, sin)


def _run_group(x, mod, row0, row_stride, wts, cache, rope_tables, tile, attn_tile):
    new_kv = []
    for l in range(DEPTH):
        i = l // 2
        g1 = wts["norm_g"][l, 0].reshape(1, D_MODEL)
        g2 = wts["norm_g"][l, 1].reshape(1, D_MODEL)
        if l % 2 == 0:
            x = _even_sublayer(x, mod, l, row0, row_stride, g1, wts["w_in_even"][i],
                               wts["pool_w"][i], wts["pool_scale"][i].reshape(1, POOL_WIDTH),
                               wts["conv_w"][i], wts["conv_b"][i].reshape(1, CONV_WIDTH),
                               wts["w_out_even"][i], tile)
        else:
            layer_cache = None if cache is None else (cache[0][:, i], cache[1][:, i])
            res = _qkv_proj(x, mod, l, row0, row_stride, g1, wts["w_qkv"][i],
                            wts["q_gain"][i].reshape(1, HEAD_DIM),
                            wts["k_gain"][i].reshape(1, HEAD_DIM), rope_tables, layer_cache,
                            cache is None, tile)
            q, kh, vt = res[:3]
            if cache is None:
                new_kv.append(res[3:])
            x = _attn_sublayer(q, kh, vt, x, mod, l, row0, row_stride, wts["w_o"][i], attn_tile)
        x = _ffn_sublayer(x, mod, l, row0, row_stride, g2, wts["w_ffn_in"][l], wts["w_ffn_out"][l],
                          wts["final_g"].reshape(1, D_MODEL), l == DEPTH - 1, tile)
    return x, new_kv


def kernel(x_prompt, x_sample, cache_k, cache_v, c, c_ctx, w_ada, b_ada, norm_g, w_in_even, pool_w,
           pool_scale, conv_w, conv_b, w_out_even, w_qkv, q_gain, k_gain, w_o, w_ffn_in, w_ffn_out,
           final_g):
    n_ctx, ctx_len, _ = x_prompt.shape
    n_lat, lat_len, _ = x_sample.shape
    assert 1 + n_lat <= COND_ROWS

    cond = jnp.concatenate(
        [c_ctx[None, :], c, jnp.zeros((COND_ROWS - 1 - n_lat, D_MODEL), F32)], axis=0)
    mod = _ada(cond, w_ada, b_ada).reshape(DEPTH, COND_ROWS, 6, D_MODEL)

    wts = dict(
        norm_g=norm_g, final_g=final_g, pool_scale=pool_scale, conv_w=conv_w, conv_b=conv_b,
        q_gain=q_gain, k_gain=k_gain,
        w_in_even=w_in_even.astype(BF16), pool_w=pool_w.astype(BF16),
        w_out_even=w_out_even.astype(BF16), w_qkv=w_qkv.astype(BF16), w_o=w_o.astype(BF16),
        w_ffn_in=w_ffn_in.astype(BF16), w_ffn_out=w_ffn_out.astype(BF16),
    )

    y_prompt, ctx_kv = _run_group(x_prompt, mod, 0, 0, wts, None, None,
                                  tile=ctx_len, attn_tile=ctx_len)
    new_cache_k = jnp.stack([k.reshape(n_ctx, ctx_len, N_KV_HEADS, HEAD_DIM) for k, _ in ctx_kv], 1)
    new_cache_v = jnp.stack([v.reshape(n_ctx, ctx_len, N_KV_HEADS, HEAD_DIM) for _, v in ctx_kv], 1)

    past = cache_k.shape[2]
    cache = (cache_k.reshape(n_lat, DEPTH // 2, past, KV_WIDTH),
             cache_v.reshape(n_lat, DEPTH // 2, past, KV_WIDTH))
    y_sample, _ = _run_group(x_sample, mod, 1, 1, wts, cache, _rope_tables(lat_len),
                             tile=512, attn_tile=256)
    return y_prompt, y_sample, new_cache_k, new_cache_v
```

```python
import functools
import math

import jax
import jax.numpy as jnp
from jax import lax
from jax.experimental import pallas as pl
from jax.experimental.pallas import tpu as pltpu

D_MODEL = 1024
DEPTH = 2
GRID_W = 64
EPS = 1e-6
POOL_WIDTH = D_MODEL // 2
POOL_GROUPS = 4
POOL_GROUP_DIM = POOL_WIDTH // POOL_GROUPS
POOL_WINDOWS = (2, 4, 8, 16)
CONV_WIDTH = D_MODEL // 2
EVEN_IN = POOL_WIDTH + 3 * CONV_WIDTH
HEAD_DIM = 128
N_HEADS = D_MODEL // HEAD_DIM
N_KV_HEADS = N_HEADS // 4
KV_WIDTH = N_KV_HEADS * HEAD_DIM
QKV_WIDTH = (N_HEADS + 2 * N_KV_HEADS) * HEAD_DIM
ROPE_THETA = 10000.0
AXIS_DIM = HEAD_DIM // 2
D_FF = ((8 * D_MODEL // 3 + 255) // 256) * 256
SM_SCALE = HEAD_DIM ** -0.5
LOG2E = math.log2(math.e)

COND_ROWS = 8
HALO = 16
ADA_BLOCK_N = 1536
FF_CHUNKS = ((0, 1024), (1024, 1024), (2048, 768))
ATTN_KEY_CHUNK = 512
ATTN_LOOKAHEAD = 4
ATTN_FLAGS = None
VMEM_LIMIT_BYTES = 56 * 1024 * 1024

BF16 = jnp.bfloat16
F32 = jnp.float32


def _dot(a, b):
    return jnp.dot(a, b, preferred_element_type=F32)


def _dot_nt(a, b):
    return lax.dot_general(a, b, (((1,), (1,)), ((), ())), preferred_element_type=F32)


def _silu(a):
    return a * (1.0 / (1.0 + jnp.exp(-a)))


def _rms(x):
    return x * lax.rsqrt(jnp.mean(x * x, axis=-1, keepdims=True) + EPS)


def _modulate(x, g, shift, scale):
    return (_rms(x) * g) * (1.0 + scale) + shift


def _params(sem, flags=None):
    return pltpu.CompilerParams(dimension_semantics=sem, vmem_limit_bytes=VMEM_LIMIT_BYTES,
                                flags=flags)


def _const_spec(shape):
    n = len(shape)
    return pl.BlockSpec(shape, lambda *_: (0,) * n, pipeline_mode=pl.Buffered(1))


def _ada_kernel(cond_ref, w_ref, b_ref, o_ref):
    s = _silu(cond_ref[...]).astype(BF16)
    o_ref[...] = _dot(s, w_ref[...].astype(BF16)) + b_ref[...]


def _ada(cond, w_ada, b_ada):
    n_out = w_ada.shape[-1]
    return pl.pallas_call(
        _ada_kernel,
        out_shape=jax.ShapeDtypeStruct((DEPTH, COND_ROWS, n_out), F32),
        grid=(DEPTH, n_out // ADA_BLOCK_N),
        in_specs=[
            pl.BlockSpec((COND_ROWS, D_MODEL), lambda l, j: (0, 0)),
            pl.BlockSpec((None, D_MODEL, ADA_BLOCK_N), lambda l, j: (l, 0, j)),
            pl.BlockSpec((None, 1, ADA_BLOCK_N), lambda l, j: (l, 0, j)),
        ],
        out_specs=pl.BlockSpec((None, COND_ROWS, ADA_BLOCK_N), lambda l, j: (l, 0, j)),
        compiler_params=_params(("arbitrary", "arbitrary")),
        name="ada",
    )(cond, w_ada, b_ada.reshape(DEPTH, 1, n_out))


def _mod_spec(layer, row0, row_stride):
    return pl.BlockSpec((None, None, 6, D_MODEL),
                        lambda s, t: (layer, row0 + row_stride * s, 0, 0))


def _even_kernel(x_ref, xp_ref, xn_ref, mod_ref, g_ref, w_in_ref, pool_w_ref, pool_scale_ref,
                 conv_w_ref, conv_b_ref, w_out_ref, o_ref, *, tile, seq_len):
    t = pl.program_id(1)
    n_t = pl.num_programs(1)
    ext = tile + 2 * HALO
    mod = mod_ref[...]
    shift, scale, gate = mod[0:1], mod[1:2], mod[2:3]
    g = g_ref[...]
    x = x_ref[...]
    h = jnp.concatenate(
        [_modulate(xp_ref[...], g, shift, scale).astype(BF16),
         _modulate(x, g, shift, scale).astype(BF16),
         _modulate(xn_ref[...], g, shift, scale).astype(BF16)], axis=0)
    p = _dot(h, w_in_ref[...])

    row = lax.broadcasted_iota(jnp.int32, (ext, 1), 0)
    first_valid = jnp.where(t > 0, 0, HALO)
    end_valid = jnp.where(t < n_t - 1, ext, HALO + tile)
    valid = jnp.logical_and(row >= first_valid, row < end_valid)
    pos = t * tile + row[HALO:HALO + tile] - HALO

    def shifted(a, s):
        return pltpu.roll(a, s % ext, 0)

    u = jnp.where(valid, p[:, :POOL_WIDTH], 0.0)
    ya = []
    for i, w in enumerate(POOL_WINDOWS):
        ug = u[:, i * POOL_GROUP_DIM:(i + 1) * POOL_GROUP_DIM]
        win = ug + shifted(ug, 1)
        half = 1
        while 2 * half < w:
            win = shifted(win, half) + shifted(win, -half)
            half *= 2
        cnt = (jnp.clip(pos + w // 2, 0, seq_len) - jnp.clip(pos - w // 2, 0, seq_len)).astype(F32)
        pooled = win[HALO:HALO + tile] / cnt - ug[HALO:HALO + tile]
        ya.append(_dot(pooled.astype(BF16), pool_w_ref[i]))
    ya = jnp.concatenate(ya, axis=-1) * pool_scale_ref[...]

    c0 = POOL_WIDTH
    bg = p[HALO:HALO + tile, c0:c0 + CONV_WIDTH]
    z = jnp.where(valid, p[:, c0 + CONV_WIDTH:c0 + 2 * CONV_WIDTH] * p[:, c0 + 2 * CONV_WIDTH:], 0.0)
    cw = conv_w_ref[...]
    conv = (shifted(z, 1) * cw[0:1] + z * cw[1:2] + shifted(z, -1) * cw[2:3])[HALO:HALO + tile]
    yb = bg * (conv + conv_b_ref[...])

    y = _dot(jnp.concatenate([ya, yb], axis=-1).astype(BF16), w_out_ref[...])
    o_ref[...] = x + gate * y


def _even_sublayer(x, mod, layer, row0, row_stride, g, w_in, pool_w, pool_scale, conv_w, conv_b,
                   w_out, tile):
    n_seq, seq_len, _ = x.shape
    n_t = seq_len // tile
    hb = tile // HALO
    n_hb = seq_len // HALO
    kern = functools.partial(_even_kernel, tile=tile, seq_len=seq_len)
    return pl.pallas_call(
        kern,
        out_shape=jax.ShapeDtypeStruct(x.shape, F32),
        grid=(n_seq, n_t),
        in_specs=[
            pl.BlockSpec((None, tile, D_MODEL), lambda s, t: (s, t, 0)),
            pl.BlockSpec((None, HALO, D_MODEL), lambda s, t: (s, jnp.maximum(t * hb - 1, 0), 0)),
            pl.BlockSpec((None, HALO, D_MODEL),
                         lambda s, t: (s, jnp.minimum((t + 1) * hb, n_hb - 1), 0)),
            _mod_spec(layer, row0, row_stride),
            _const_spec((1, D_MODEL)),
            _const_spec((D_MODEL, EVEN_IN)),
            _const_spec((POOL_GROUPS, POOL_GROUP_DIM, POOL_GROUP_DIM)),
            _const_spec((1, POOL_WIDTH)),
            _const_spec((3, CONV_WIDTH)),
            _const_spec((1, CONV_WIDTH)),
            _const_spec((POOL_WIDTH + CONV_WIDTH, D_MODEL)),
        ],
        out_specs=pl.BlockSpec((None, tile, D_MODEL), lambda s, t: (s, t, 0)),
        compiler_params=_params(("arbitrary", "arbitrary")),
        name="even_mixer",
    )(x, x, x, mod, g, w_in, pool_w, pool_scale, conv_w, conv_b, w_out)


def _ffn_kernel(x_ref, mod_ref, g_ref, w_in_ref, w_out_ref, fg_ref, o_ref, *, final_norm):
    mod = mod_ref[...]
    shift, scale, gate = mod[3:4], mod[4:5], mod[5:6]
    x = x_ref[...]
    h = _modulate(x, g_ref[...], shift, scale).astype(BF16)
    acc = None
    for c0, cw in FF_CHUNKS:
        a = _dot(h, w_in_ref[:, c0:c0 + cw])
        b = _dot(h, w_in_ref[:, D_FF + c0:D_FF + c0 + cw])
        part = _dot((_silu(a) * b).astype(BF16), w_out_ref[c0:c0 + cw, :])
        acc = part if acc is None else acc + part
    out = x + gate * acc
    if final_norm:
        out = _rms(out) * fg_ref[...]
    o_ref[...] = out


def _ffn_sublayer(x, mod, layer, row0, row_stride, g, w_in, w_out, final_g, final_norm, tile):
    n_seq, seq_len, _ = x.shape
    kern = functools.partial(_ffn_kernel, final_norm=final_norm)
    return pl.pallas_call(
        kern,
        out_shape=jax.ShapeDtypeStruct(x.shape, F32),
        grid=(n_seq, seq_len // tile),
        in_specs=[
            pl.BlockSpec((None, tile, D_MODEL), lambda s, t: (s, t, 0)),
            _mod_spec(layer, row0, row_stride),
            _const_spec((1, D_MODEL)),
            _const_spec((D_MODEL, 2 * D_FF)),
            _const_spec((D_FF, D_MODEL)),
            _const_spec((1, D_MODEL)),
        ],
        out_specs=pl.BlockSpec((None, tile, D_MODEL), lambda s, t: (s, t, 0)),
        compiler_params=_params(("arbitrary", "arbitrary")),
        name="ffn",
    )(x, mod, g, w_in, w_out, final_g)


def _rope(a, cos, sin_lo, sin_hi):
    return (a * cos + pltpu.roll(a, HEAD_DIM - AXIS_DIM // 2, 1) * sin_lo
            + pltpu.roll(a, AXIS_DIM // 2, 1) * sin_hi)


def _qkv_kernel(*refs, rope, cached, keep_f32):
    it = iter(refs)
    x_ref, mod_ref, g_ref, w_ref, qg_ref, kg_ref = (next(it) for _ in range(6))
    rope_refs = [next(it) for _ in range(3)] if rope else None
    cache_refs = [next(it) for _ in range(2)] if cached else None
    q_ref, kh_ref, vt_ref = (next(it) for _ in range(3))
    f32_refs = [next(it) for _ in range(2)] if keep_f32 else None

    def project():
        mod = mod_ref[...]
        shift, scale = mod[0:1], mod[1:2]
        h = _modulate(x_ref[...], g_ref[...], shift, scale).astype(BF16)
        p = _dot(h, w_ref[...])
        if rope:
            tables = [r[...] for r in rope_refs]
        qg, kg = qg_ref[...], kg_ref[...]
        for hd in range(N_HEADS):
            a = _rms(p[:, hd * HEAD_DIM:(hd + 1) * HEAD_DIM]) * qg
            if rope:
                a = _rope(a, *tables)
            q_ref[hd] = (a * (SM_SCALE * LOG2E)).astype(BF16)
        k0 = N_HEADS * HEAD_DIM
        v0 = k0 + KV_WIDTH
        for kv in range(N_KV_HEADS):
            a = _rms(p[:, k0 + kv * HEAD_DIM:k0 + (kv + 1) * HEAD_DIM]) * kg
            if rope:
                a = _rope(a, *tables)
            v = p[:, v0 + kv * HEAD_DIM:v0 + (kv + 1) * HEAD_DIM]
            kh_ref[kv] = a.astype(BF16)
            vt_ref[kv] = v.T.astype(BF16)
            if keep_f32:
                f32_refs[0][:, kv * HEAD_DIM:(kv + 1) * HEAD_DIM] = a
                f32_refs[1][:, kv * HEAD_DIM:(kv + 1) * HEAD_DIM] = v

    if not cached:
        project()
        return

    t = pl.program_id(1)

    @pl.when(t == 0)
    def _():
        ck, cv = cache_refs[0][...], cache_refs[1][...]
        for kv in range(N_KV_HEADS):
            kh_ref[kv] = ck[:, kv * HEAD_DIM:(kv + 1) * HEAD_DIM].astype(BF16)
            vt_ref[kv] = cv[:, kv * HEAD_DIM:(kv + 1) * HEAD_DIM].T.astype(BF16)

    pl.when(t > 0)(project)


def _qkv_proj(x, mod, layer, row0, row_stride, g, w_qkv, q_gain, k_gain, rope_tables, cache,
              keep_f32, tile):
    n_seq, seq_len, _ = x.shape
    rope = rope_tables is not None
    cached = cache is not None
    lead = 0
    if cached:
        assert cache[0].shape[1] == tile
        lead = 1
    n_keys = seq_len + lead * tile
    tok = lambda t: jnp.maximum(t - lead, 0)
    in_specs = [
        pl.BlockSpec((None, tile, D_MODEL), lambda s, t: (s, tok(t), 0)),
        _mod_spec(layer, row0, row_stride),
        _const_spec((1, D_MODEL)),
        _const_spec((D_MODEL, QKV_WIDTH)),
        _const_spec((1, HEAD_DIM)),
        _const_spec((1, HEAD_DIM)),
    ]
    args = [x, mod, g, w_qkv, q_gain, k_gain]
    if rope:
        in_specs += [pl.BlockSpec((tile, HEAD_DIM), lambda s, t: (tok(t), 0))] * 3
        args += list(rope_tables)
    if cached:
        in_specs += [pl.BlockSpec((None, tile, KV_WIDTH), lambda s, t: (s, 0, 0))] * 2
        args += list(cache)
    out_shape = [jax.ShapeDtypeStruct((n_seq, N_HEADS, seq_len, HEAD_DIM), BF16),
                 jax.ShapeDtypeStruct((n_seq, N_KV_HEADS, n_keys, HEAD_DIM), BF16),
                 jax.ShapeDtypeStruct((n_seq, N_KV_HEADS, HEAD_DIM, n_keys), BF16)]
    out_specs = [pl.BlockSpec((None, N_HEADS, tile, HEAD_DIM), lambda s, t: (s, 0, tok(t), 0)),
                 pl.BlockSpec((None, N_KV_HEADS, tile, HEAD_DIM), lambda s, t: (s, 0, t, 0)),
                 pl.BlockSpec((None, N_KV_HEADS, HEAD_DIM, tile), lambda s, t: (s, 0, 0, t))]
    if keep_f32:
        assert not cached
        out_shape += [jax.ShapeDtypeStruct((n_seq, seq_len, KV_WIDTH), F32)] * 2
        out_specs += [pl.BlockSpec((None, tile, KV_WIDTH), lambda s, t: (s, t, 0))] * 2
    return pl.pallas_call(
        functools.partial(_qkv_kernel, rope=rope, cached=cached, keep_f32=keep_f32),
        out_shape=tuple(out_shape),
        grid=(n_seq, seq_len // tile + lead),
        in_specs=in_specs,
        out_specs=tuple(out_specs),
        compiler_params=_params(("arbitrary", "arbitrary")),
        name="qkv_proj",
    )(*args)


def _attn_kernel(q_ref, kh_ref, vt_ref, x_ref, mod_ref, w_o_ref, o_ref, *, key_chunk):
    group = N_HEADS // N_KV_HEADS
    n_chunks = kh_ref.shape[1] // key_chunk
    items = [(hd, c) for hd in range(N_HEADS) for c in range(n_chunks)]
    look = n_chunks + ATTN_LOOKAHEAD
    scores, col_max = {}, {}

    def issue_scores(j):
        hd, c = items[j]
        keys = kh_ref[hd // group, c * key_chunk:(c + 1) * key_chunk, :]
        s = _dot_nt(keys, q_ref[hd])
        scores[j] = s
        cm = jnp.max(s, axis=0, keepdims=True)
        col_max[hd] = cm if c == 0 else jnp.maximum(col_max[hd], cm)

    for j in range(min(look, len(items))):
        issue_scores(j)
    outs = []
    for i, (hd, c) in enumerate(items):
        if i + look < len(items):
            issue_scores(i + look)
        p = jnp.exp2(scores.pop(i) - col_max[hd])
        ps = jnp.sum(p, axis=0, keepdims=True)
        pv = _dot(vt_ref[hd // group, :, c * key_chunk:(c + 1) * key_chunk], p.astype(BF16))
        denom, acc = (ps, pv) if c == 0 else (denom + ps, acc + pv)
        if c == n_chunks - 1:
            outs.append((acc * (1.0 / denom)).T.astype(BF16))
    y = _dot(jnp.concatenate(outs, axis=-1), w_o_ref[...])
    o_ref[...] = x_ref[...] + mod_ref[2:3] * y


def _attn_sublayer(q, kh, vt, x, mod, layer, row0, row_stride, w_o, tile):
    n_seq, seq_len, _ = x.shape
    n_keys = kh.shape[2]
    return pl.pallas_call(
        functools.partial(_attn_kernel, key_chunk=min(n_keys, ATTN_KEY_CHUNK)),
        out_shape=jax.ShapeDtypeStruct(x.shape, F32),
        grid=(n_seq, seq_len // tile),
        in_specs=[
            pl.BlockSpec((None, N_HEADS, tile, HEAD_DIM), lambda s, t: (s, 0, t, 0)),
            pl.BlockSpec((None, N_KV_HEADS, n_keys, HEAD_DIM), lambda s, t: (s, 0, 0, 0)),
            pl.BlockSpec((None, N_KV_HEADS, HEAD_DIM, n_keys), lambda s, t: (s, 0, 0, 0)),
            pl.BlockSpec((None, tile, D_MODEL), lambda s, t: (s, t, 0)),
            _mod_spec(layer, row0, row_stride),
            _const_spec((D_MODEL, D_MODEL)),
        ],
        out_specs=pl.BlockSpec((None, tile, D_MODEL), lambda s, t: (s, t, 0)),
        compiler_params=_params(("arbitrary", "arbitrary"), ATTN_FLAGS),
        name="attention",
    )(q, kh, vt, x, mod, w_o)


def _rope_tables(seq_len):
    rows = seq_len // GRID_W
    r = jnp.repeat(jnp.arange(rows), GRID_W).astype(F32)
    col = jnp.tile(jnp.arange(GRID_W), rows).astype(F32)
    inv = ROPE_THETA ** (-jnp.arange(0, AXIS_DIM, 2, dtype=F32) / AXIS_DIM)
    ar = r[:, None] * inv
    ac = col[:, None] * inv
    ang = jnp.concatenate([ar, ar, ac, ac], axis=-1)
    cos, sin = jnp.cos(ang), jnp.sin(ang)
    lo = (jnp.arange(HEAD_DIM) % AXIS_DIM) < AXIS_DIM // 2
    return cos, jnp.where(lo, -sin, 0.0), jnp.where(lo, 0.0, sin)


def _run_group(x, mod, row0, row_stride, wts, cache, rope_tables, tile, attn_tile):
    new_kv = []
    for l in range(DEPTH):
        i = l // 2
        g1 = wts["norm_g"][l, 0].reshape(1, D_MODEL)
        g2 = wts["norm_g"][l, 1].reshape(1, D_MODEL)
        if l % 2 == 0:
            x = _even_sublayer(x, mod, l, row0, row_stride, g1, wts["w_in_even"][i],
                               wts["pool_w"][i], wts["pool_scale"][i].reshape(1, POOL_WIDTH),
                               wts["conv_w"][i], wts["conv_b"][i].reshape(1, CONV_WIDTH),
                               wts["w_out_even"][i], tile)
        else:
            layer_cache = None if cache is None else (cache[0][:, i], cache[1][:, i])
            res = _qkv_proj(x, mod, l, row0, row_stride, g1, wts["w_qkv"][i],
                            wts["q_gain"][i].reshape(1, HEAD_DIM),
                            wts["k_gain"][i].reshape(1, HEAD_DIM), rope_tables, layer_cache,
                            cache is None, tile)
            q, kh, vt = res[:3]
            if cache is None:
                new_kv.append(res[3:])
            x = _attn_sublayer(q, kh, vt, x, mod, l, row0, row_stride, wts["w_o"][i], attn_tile)
        x = _ffn_sublayer(x, mod, l, row0, row_stride, g2, wts["w_ffn_in"][l], wts["w_ffn_out"][l],
                          wts["final_g"].reshape(1, D_MODEL), l == DEPTH - 1, tile)
    return x, new_kv


def kernel(x_prompt, x_sample, cache_k, cache_v, c, c_ctx, w_ada, b_ada, norm_g, w_in_even, pool_w,
           pool_scale, conv_w, conv_b, w_out_even, w_qkv, q_gain, k_gain, w_o, w_ffn_in, w_ffn_out,
           final_g):
    n_ctx, ctx_len, _ = x_prompt.shape
    n_lat, lat_len, _ = x_sample.shape
    assert 1 + n_lat <= COND_ROWS

    cond = jnp.concatenate(
        [c_ctx[None, :], c, jnp.zeros((COND_ROWS - 1 - n_lat, D_MODEL), F32)], axis=0)
    mod = _ada(cond, w_ada, b_ada).reshape(DEPTH, COND_ROWS, 6, D_MODEL)

    wts = dict(
        norm_g=norm_g, final_g=final_g, pool_scale=pool_scale, conv_w=conv_w, conv_b=conv_b,
        q_gain=q_gain, k_gain=k_gain,
        w_in_even=w_in_even.astype(BF16), pool_w=pool_w.astype(BF16),
        w_out_even=w_out_even.astype(BF16), w_qkv=w_qkv.astype(BF16), w_o=w_o.astype(BF16),
        w_ffn_in=w_ffn_in.astype(BF16), w_ffn_out=w_ffn_out.astype(BF16),
    )

    y_prompt, ctx_kv = _run_group(x_prompt, mod, 0, 0, wts, None, None,
                                  tile=ctx_len, attn_tile=ctx_len)
    new_cache_k = jnp.stack([k.reshape(n_ctx, ctx_len, N_KV_HEADS, HEAD_DIM) for k, _ in ctx_kv], 1)
    new_cache_v = jnp.stack([v.reshape(n_ctx, ctx_len, N_KV_HEADS, HEAD_DIM) for _, v in ctx_kv], 1)

    past = cache_k.shape[2]
    cache = (cache_k.reshape(n_lat, DEPTH // 2, past, KV_WIDTH),
             cache_v.reshape(n_lat, DEPTH // 2, past, KV_WIDTH))
    y_sample, _ = _run_group(x_sample, mod, 1, 1, wts, cache, _rope_tables(lat_len),
                             tile=512, attn_tile=512)
    return y_prompt, y_sample, new_cache_k, new_cache_v
```

```python
import functools
import math

import jax
import jax.numpy as jnp
from jax import lax
from jax.experimental import pallas as pl
from jax.experimental.pallas import tpu as pltpu

D_MODEL = 1024
DEPTH = 2
GRID_W = 64
EPS = 1e-6
POOL_WIDTH = D_MODEL // 2
POOL_GROUPS = 4
POOL_GROUP_DIM = POOL_WIDTH // POOL_GROUPS
POOL_WINDOWS = (2, 4, 8, 16)
CONV_WIDTH = D_MODEL // 2
EVEN_IN = POOL_WIDTH + 3 * CONV_WIDTH
HEAD_DIM = 128
N_HEADS = D_MODEL // HEAD_DIM
N_KV_HEADS = N_HEADS // 4
KV_WIDTH = N_KV_HEADS * HEAD_DIM
QKV_WIDTH = (N_HEADS + 2 * N_KV_HEADS) * HEAD_DIM
ROPE_THETA = 10000.0
AXIS_DIM = HEAD_DIM // 2
D_FF = ((8 * D_MODEL // 3 + 255) // 256) * 256
SM_SCALE = HEAD_DIM ** -0.5
LOG2E = math.log2(math.e)

COND_ROWS = 8
HALO = 16
ADA_BLOCK_N = 1536
FF_CHUNKS = ((0, 1024), (1024, 1024), (2048, 768))
ATTN_KEY_CHUNK = 512
ATTN_LOOKAHEAD = 4
ATTN_FLAGS = None
VMEM_LIMIT_BYTES = 56 * 1024 * 1024

BF16 = jnp.bfloat16
F32 = jnp.float32


def _dot(a, b):
    return jnp.dot(a, b, preferred_element_type=F32)


def _dot_nt(a, b):
    return lax.dot_general(a, b, (((1,), (1,)), ((), ())), preferred_element_type=F32)


def _silu(a):
    return a * (1.0 / (1.0 + jnp.exp(-a)))


def _rms(x):
    return x * lax.rsqrt(jnp.mean(x * x, axis=-1, keepdims=True) + EPS)


def _modulate(x, g, shift, scale):
    return (_rms(x) * g) * (1.0 + scale) + shift


def _params(sem, flags=None):
    return pltpu.CompilerParams(dimension_semantics=sem, vmem_limit_bytes=VMEM_LIMIT_BYTES,
                                flags=flags)


def _const_spec(shape):
    n = len(shape)
    return pl.BlockSpec(shape, lambda *_: (0,) * n, pipeline_mode=pl.Buffered(1))


def _ada_kernel(cond_ref, w_ref, b_ref, o_ref):
    s = _silu(cond_ref[...]).astype(BF16)
    o_ref[...] = _dot(s, w_ref[...].astype(BF16)) + b_ref[...]


def _ada(cond, w_ada, b_ada):
    n_out = w_ada.shape[-1]
    return pl.pallas_call(
        _ada_kernel,
        out_shape=jax.ShapeDtypeStruct((DEPTH, COND_ROWS, n_out), F32),
        grid=(DEPTH, n_out // ADA_BLOCK_N),
        in_specs=[
            pl.BlockSpec((COND_ROWS, D_MODEL), lambda l, j: (0, 0)),
            pl.BlockSpec((None, D_MODEL, ADA_BLOCK_N), lambda l, j: (l, 0, j)),
            pl.BlockSpec((None, 1, ADA_BLOCK_N), lambda l, j: (l, 0, j)),
        ],
        out_specs=pl.BlockSpec((None, COND_ROWS, ADA_BLOCK_N), lambda l, j: (l, 0, j)),
        compiler_params=_params(("arbitrary", "arbitrary")),
        name="ada",
    )(cond, w_ada, b_ada.reshape(DEPTH, 1, n_out))


def _mod_spec(layer, row0, row_stride):
    return pl.BlockSpec((None, None, 6, D_MODEL),
                        lambda s, t: (layer, row0 + row_stride * s, 0, 0))


def _even_kernel(x_ref, xp_ref, xn_ref, mod_ref, g_ref, w_in_ref, pool_w_ref, pool_scale_ref,
                 conv_w_ref, conv_b_ref, w_out_ref, o_ref, *, tile, seq_len):
    t = pl.program_id(1)
    n_t = pl.num_programs(1)
    ext = tile + 2 * HALO
    mod = mod_ref[...]
    shift, scale, gate = mod[0:1], mod[1:2], mod[2:3]
    g = g_ref[...]
    x = x_ref[...]
    h = jnp.concatenate(
        [_modulate(xp_ref[...], g, shift, scale).astype(BF16),
         _modulate(x, g, shift, scale).astype(BF16),
         _modulate(xn_ref[...], g, shift, scale).astype(BF16)], axis=0)
    p = _dot(h, w_in_ref[...])

    row = lax.broadcasted_iota(jnp.int32, (ext, 1), 0)
    first_valid = jnp.where(t > 0, 0, HALO)
    end_valid = jnp.where(t < n_t - 1, ext, HALO + tile)
    valid = jnp.logical_and(row >= first_valid, row < end_valid)
    pos = t * tile + row[HALO:HALO + tile] - HALO

    def shifted(a, s):
        return pltpu.roll(a, s % ext, 0)

    u = jnp.where(valid, p[:, :POOL_WIDTH], 0.0)
    ya = []
    for i, w in enumerate(POOL_WINDOWS):
        ug = u[:, i * POOL_GROUP_DIM:(i + 1) * POOL_GROUP_DIM]
        win = ug + shifted(ug, 1)
        half = 1
        while 2 * half < w:
            win = shifted(win, half) + shifted(win, -half)
            half *= 2
        cnt = (jnp.clip(pos + w // 2, 0, seq_len) - jnp.clip(pos - w // 2, 0, seq_len)).astype(F32)
        pooled = win[HALO:HALO + tile] / cnt - ug[HALO:HALO + tile]
        ya.append(_dot(pooled.astype(BF16), pool_w_ref[i]))
    ya = jnp.concatenate(ya, axis=-1) * pool_scale_ref[...]

    c0 = POOL_WIDTH
    bg = p[HALO:HALO + tile, c0:c0 + CONV_WIDTH]
    z = jnp.where(valid, p[:, c0 + CONV_WIDTH:c0 + 2 * CONV_WIDTH] * p[:, c0 + 2 * CONV_WIDTH:], 0.0)
    cw = conv_w_ref[...]
    conv = (shifted(z, 1) * cw[0:1] + z * cw[1:2] + shifted(z, -1) * cw[2:3])[HALO:HALO + tile]
    yb = bg * (conv + conv_b_ref[...])

    y = _dot(jnp.concatenate([ya, yb], axis=-1).astype(BF16), w_out_ref[...])
    o_ref[...] = x + gate * y


def _even_sublayer(x, mod, layer, row0, row_stride, g, w_in, pool_w, pool_scale, conv_w, conv_b,
                   w_out, tile):
    n_seq, seq_len, _ = x.shape
    n_t = seq_len // tile
    hb = tile // HALO
    n_hb = seq_len // HALO
    kern = functools.partial(_even_kernel, tile=tile, seq_len=seq_len)
    return pl.pallas_call(
        kern,
        out_shape=jax.ShapeDtypeStruct(x.shape, F32),
        grid=(n_seq, n_t),
        in_specs=[
            pl.BlockSpec((None, tile, D_MODEL), lambda s, t: (s, t, 0)),
            pl.BlockSpec((None, HALO, D_MODEL), lambda s, t: (s, jnp.maximum(t * hb - 1, 0), 0)),
            pl.BlockSpec((None, HALO, D_MODEL),
                         lambda s, t: (s, jnp.minimum((t + 1) * hb, n_hb - 1), 0)),
            _mod_spec(layer, row0, row_stride),
            _const_spec((1, D_MODEL)),
            _const_spec((D_MODEL, EVEN_IN)),
            _const_spec((POOL_GROUPS, POOL_GROUP_DIM, POOL_GROUP_DIM)),
            _const_spec((1, POOL_WIDTH)),
            _const_spec((3, CONV_WIDTH)),
            _const_spec((1, CONV_WIDTH)),
            _const_spec((POOL_WIDTH + CONV_WIDTH, D_MODEL)),
        ],
        out_specs=pl.BlockSpec((None, tile, D_MODEL), lambda s, t: (s, t, 0)),
        compiler_params=_params(("arbitrary", "arbitrary")),
        name="even_mixer",
    )(x, x, x, mod, g, w_in, pool_w, pool_scale, conv_w, conv_b, w_out)


def _ffn_kernel(x_ref, mod_ref, g_ref, w_in_ref, w_out_ref, fg_ref, o_ref, *, final_norm):
    mod = mod_ref[...]
    shift, scale, gate = mod[3:4], mod[4:5], mod[5:6]
    x = x_ref[...]
    h = _modulate(x, g_ref[...], shift, scale).astype(BF16)
    acc = None
    for c0, cw in FF_CHUNKS:
        a = _dot(h, w_in_ref[:, c0:c0 + cw])
        b = _dot(h, w_in_ref[:, D_FF + c0:D_FF + c0 + cw])
        part = _dot((_silu(a) * b).astype(BF16), w_out_ref[c0:c0 + cw, :])
        acc = part if acc is None else acc + part
    out = x + gate * acc
    if final_norm:
        out = _rms(out) * fg_ref[...]
    o_ref[...] = out


def _ffn_sublayer(x, mod, layer, row0, row_stride, g, w_in, w_out, final_g, final_norm, tile):
    n_seq, seq_len, _ = x.shape
    kern = functools.partial(_ffn_kernel, final_norm=final_norm)
    return pl.pallas_call(
        kern,
        out_shape=jax.ShapeDtypeStruct(x.shape, F32),
        grid=(n_seq, seq_len // tile),
        in_specs=[
            pl.BlockSpec((None, tile, D_MODEL), lambda s, t: (s, t, 0)),
            _mod_spec(layer, row0, row_stride),
            _const_spec((1, D_MODEL)),
            _const_spec((D_MODEL, 2 * D_FF)),
            _const_spec((D_FF, D_MODEL)),
            _const_spec((1, D_MODEL)),
        ],
        out_specs=pl.BlockSpec((None, tile, D_MODEL), lambda s, t: (s, t, 0)),
        compiler_params=_params(("arbitrary", "arbitrary")),
        name="ffn",
    )(x, mod, g, w_in, w_out, final_g)


def _rope_lane_order(a):
    lead = a.shape[:-1]
    return a.reshape(*lead, 2, 2, AXIS_DIM // 2).swapaxes(-3, -2).reshape(*lead, HEAD_DIM)


def _rope(a, cos, sin_signed):
    return a * cos + pltpu.roll(a, HEAD_DIM // 2, 1) * sin_signed


def _pair_mean_matrix():
    r = lax.broadcasted_iota(jnp.int32, (2 * HEAD_DIM, 2 * HEAD_DIM), 0) // HEAD_DIM
    c = lax.broadcasted_iota(jnp.int32, (2 * HEAD_DIM, 2 * HEAD_DIM), 1) // HEAD_DIM
    return jnp.where(r == c, 1.0 / HEAD_DIM, 0.0).astype(BF16)


def _rope_lane_matrix():
    src = lax.broadcasted_iota(jnp.int32, (HEAD_DIM, HEAD_DIM), 0)
    n = lax.broadcasted_iota(jnp.int32, (HEAD_DIM, HEAD_DIM), 1)
    quarter = AXIS_DIM // 2
    half, axis, c = n // AXIS_DIM, (n // quarter) % 2, n % quarter
    return jnp.where(src == axis * AXIS_DIM + half * quarter + c, 1.0, 0.0).astype(BF16)


def _qkv_kernel(*refs, rope, cached, keep_f32):
    it = iter(refs)
    x_ref, mod_ref, g_ref, w_ref, qg_ref, kg_ref = (next(it) for _ in range(6))
    rope_refs = [next(it) for _ in range(2)] if rope else None
    cache_refs = [next(it) for _ in range(2)] if cached else None
    q_ref, kh_ref, vt_ref = (next(it) for _ in range(3))
    f32_refs = [next(it) for _ in range(2)] if keep_f32 else None

    def project():
        mod = mod_ref[...]
        shift, scale = mod[0:1], mod[1:2]
        h = _modulate(x_ref[...], g_ref[...], shift, scale).astype(BF16)
        p = _dot(h, w_ref[...])
        if rope:
            tables = [r[...] for r in rope_refs]
        gains = [qg_ref[...] * (SM_SCALE * LOG2E)] * N_HEADS + [kg_ref[...]] * N_KV_HEADS
        pair_mean = _pair_mean_matrix()
        v0 = (N_HEADS + N_KV_HEADS) * HEAD_DIM
        for pair in range((N_HEADS + N_KV_HEADS) // 2):
            blk = p[:, pair * 2 * HEAD_DIM:(pair + 1) * 2 * HEAD_DIM]
            ms = _dot((blk * blk).astype(BF16), pair_mean)
            normed = blk * lax.rsqrt(ms + EPS)
            for sub in range(2):
                hd = 2 * pair + sub
                a = normed[:, sub * HEAD_DIM:(sub + 1) * HEAD_DIM] * gains[hd]
                if rope:
                    a = _rope(a, *tables)
                if hd < N_HEADS:
                    q_ref[hd] = a.astype(BF16)
                else:
                    kv = hd - N_HEADS
                    v = p[:, v0 + kv * HEAD_DIM:v0 + (kv + 1) * HEAD_DIM]
                    kh_ref[kv] = a.astype(BF16)
                    vt_ref[kv] = v.T.astype(BF16)
                    if keep_f32:
                        f32_refs[0][:, kv * HEAD_DIM:(kv + 1) * HEAD_DIM] = a
                        f32_refs[1][:, kv * HEAD_DIM:(kv + 1) * HEAD_DIM] = v

    if not cached:
        project()
        return

    t = pl.program_id(1)

    @pl.when(t == 0)
    def _():
        ck, cv = cache_refs[0][...], cache_refs[1][...]
        for kv in range(N_KV_HEADS):
            k = ck[:, kv * HEAD_DIM:(kv + 1) * HEAD_DIM].astype(BF16)
            if rope:
                k = _dot(k, _rope_lane_matrix()).astype(BF16)
            kh_ref[kv] = k
            vt_ref[kv] = cv[:, kv * HEAD_DIM:(kv + 1) * HEAD_DIM].T.astype(BF16)

    pl.when(t > 0)(project)


def _qkv_proj(x, mod, layer, row0, row_stride, g, w_qkv, q_gain, k_gain, rope_tables, cache,
              keep_f32, tile):
    n_seq, seq_len, _ = x.shape
    rope = rope_tables is not None
    cached = cache is not None
    lead = 0
    if cached:
        assert cache[0].shape[1] == tile
        lead = 1
    n_keys = seq_len + lead * tile
    tok = lambda t: jnp.maximum(t - lead, 0)
    in_specs = [
        pl.BlockSpec((None, tile, D_MODEL), lambda s, t: (s, tok(t), 0)),
        _mod_spec(layer, row0, row_stride),
        _const_spec((1, D_MODEL)),
        _const_spec((D_MODEL, QKV_WIDTH)),
        _const_spec((1, HEAD_DIM)),
        _const_spec((1, HEAD_DIM)),
    ]
    args = [x, mod, g, w_qkv, q_gain, k_gain]
    if rope:
        in_specs += [pl.BlockSpec((tile, HEAD_DIM), lambda s, t: (tok(t), 0))] * 2
        args += list(rope_tables)
    if cached:
        in_specs += [pl.BlockSpec((None, tile, KV_WIDTH), lambda s, t: (s, 0, 0))] * 2
        args += list(cache)
    out_shape = [jax.ShapeDtypeStruct((n_seq, N_HEADS, seq_len, HEAD_DIM), BF16),
                 jax.ShapeDtypeStruct((n_seq, N_KV_HEADS, n_keys, HEAD_DIM), BF16),
                 jax.ShapeDtypeStruct((n_seq, N_KV_HEADS, HEAD_DIM, n_keys), BF16)]
    out_specs = [pl.BlockSpec((None, N_HEADS, tile, HEAD_DIM), lambda s, t: (s, 0, tok(t), 0)),
                 pl.BlockSpec((None, N_KV_HEADS, tile, HEAD_DIM), lambda s, t: (s, 0, t, 0)),
                 pl.BlockSpec((None, N_KV_HEADS, HEAD_DIM, tile), lambda s, t: (s, 0, 0, t))]
    if keep_f32:
        assert not cached
        out_shape += [jax.ShapeDtypeStruct((n_seq, seq_len, KV_WIDTH), F32)] * 2
        out_specs += [pl.BlockSpec((None, tile, KV_WIDTH), lambda s, t: (s, t, 0))] * 2
    return pl.pallas_call(
        functools.partial(_qkv_kernel, rope=rope, cached=cached, keep_f32=keep_f32),
        out_shape=tuple(out_shape),
        grid=(n_seq, seq_len // tile + lead),
        in_specs=in_specs,
        out_specs=tuple(out_specs),
        compiler_params=_params(("arbitrary", "arbitrary")),
        name="qkv_proj",
    )(*args)


def _attn_kernel(q_ref, kh_ref, vt_ref, x_ref, mod_ref, w_o_ref, o_ref, *, key_chunk):
    group = N_HEADS // N_KV_HEADS
    n_chunks = kh_ref.shape[1] // key_chunk
    items = [(hd, c) for hd in range(N_HEADS) for c in range(n_chunks)]
    look = n_chunks + ATTN_LOOKAHEAD
    scores, col_max = {}, {}

    def issue_scores(j):
        hd, c = items[j]
        keys = kh_ref[hd // group, c * key_chunk:(c + 1) * key_chunk, :]
        s = _dot_nt(keys, q_ref[hd])
        scores[j] = s
        cm = jnp.max(s, axis=0, keepdims=True)
        col_max[hd] = cm if c == 0 else jnp.maximum(col_max[hd], cm)

    for j in range(min(look, len(items))):
        issue_scores(j)
    outs = []
    for i, (hd, c) in enumerate(items):
        if i + look < len(items):
            issue_scores(i + look)
        p = jnp.exp2(scores.pop(i) - col_max[hd])
        ps = jnp.sum(p, axis=0, keepdims=True)
        pv = _dot(vt_ref[hd // group, :, c * key_chunk:(c + 1) * key_chunk], p.astype(BF16))
        denom, acc = (ps, pv) if c == 0 else (denom + ps, acc + pv)
        if c == n_chunks - 1:
            outs.append((acc * (1.0 / denom)).T.astype(BF16))
    y = _dot(jnp.concatenate(outs, axis=-1), w_o_ref[...])
    o_ref[...] = x_ref[...] + mod_ref[2:3] * y


def _attn_sublayer(q, kh, vt, x, mod, layer, row0, row_stride, w_o, tile):
    n_seq, seq_len, _ = x.shape
    n_keys = kh.shape[2]
    return pl.pallas_call(
        functools.partial(_attn_kernel, key_chunk=min(n_keys, ATTN_KEY_CHUNK)),
        out_shape=jax.ShapeDtypeStruct(x.shape, F32),
        grid=(n_seq, seq_len // tile),
        in_specs=[
            pl.BlockSpec((None, N_HEADS, tile, HEAD_DIM), lambda s, t: (s, 0, t, 0)),
            pl.BlockSpec((None, N_KV_HEADS, n_keys, HEAD_DIM), lambda s, t: (s, 0, 0, 0)),
            pl.BlockSpec((None, N_KV_HEADS, HEAD_DIM, n_keys), lambda s, t: (s, 0, 0, 0)),
            pl.BlockSpec((None, tile, D_MODEL), lambda s, t: (s, t, 0)),
            _mod_spec(layer, row0, row_stride),
            _const_spec((D_MODEL, D_MODEL)),
        ],
        out_specs=pl.BlockSpec((None, tile, D_MODEL), lambda s, t: (s, t, 0)),
        compiler_params=_params(("arbitrary", "arbitrary"), ATTN_FLAGS),
        name="attention",
    )(q, kh, vt, x, mod, w_o)


def _rope_tables(seq_len):
    rows = seq_len // GRID_W
    r = jnp.repeat(jnp.arange(rows), GRID_W).astype(F32)
    col = jnp.tile(jnp.arange(GRID_W), rows).astype(F32)
    inv = ROPE_THETA ** (-jnp.arange(0, AXIS_DIM, 2, dtype=F32) / AXIS_DIM)
    ar = r[:, None] * inv
    ac = col[:, None] * inv
    ang = jnp.concatenate([ar, ar, ac, ac], axis=-1)
    cos, sin = jnp.cos(ang), jnp.sin(ang)
    first = (jnp.arange(HEAD_DIM) % AXIS_DIM) < AXIS_DIM // 2
    return _rope_lane_order(cos), _rope_lane_order(jnp.where(first, -sin, sin))


def _rope_weights(w_qkv, q_gain, k_gain):
    qk = (N_HEADS + N_KV_HEADS) * HEAD_DIM
    lead = w_qkv.shape[:-1]
    w_qk = _rope_lane_order(w_qkv[..., :qk].reshape(*lead, N_HEADS + N_KV_HEADS, HEAD_DIM))
    w = jnp.concatenate([w_qk.reshape(*lead, qk), w_qkv[..., qk:]], axis=-1)
    return w, _rope_lane_order(q_gain), _rope_lane_order(k_gain)


def _run_group(x, mod, row0, row_stride, wts, cache, rope_tables, tile, attn_tile):
    new_kv = []
    for l in range(DEPTH):
        i = l // 2
        g1 = wts["norm_g"][l, 0].reshape(1, D_MODEL)
        g2 = wts["norm_g"][l, 1].reshape(1, D_MODEL)
        if l % 2 == 0:
            x = _even_sublayer(x, mod, l, row0, row_stride, g1, wts["w_in_even"][i],
                               wts["pool_w"][i], wts["pool_scale"][i].reshape(1, POOL_WIDTH),
                               wts["conv_w"][i], wts["conv_b"][i].reshape(1, CONV_WIDTH),
                               wts["w_out_even"][i], tile)
        else:
            layer_cache = None if cache is None else (cache[0][:, i], cache[1][:, i])
            sfx = "" if rope_tables is None else "_rope"
            res = _qkv_proj(x, mod, l, row0, row_stride, g1, wts["w_qkv" + sfx][i],
                            wts["q_gain" + sfx][i].reshape(1, HEAD_DIM),
                            wts["k_gain" + sfx][i].reshape(1, HEAD_DIM), rope_tables, layer_cache,
                            cache is None, tile)
            q, kh, vt = res[:3]
            if cache is None:
                new_kv.append(res[3:])
            x = _attn_sublayer(q, kh, vt, x, mod, l, row0, row_stride, wts["w_o"][i], attn_tile)
        x = _ffn_sublayer(x, mod, l, row0, row_stride, g2, wts["w_ffn_in"][l], wts["w_ffn_out"][l],
                          wts["final_g"].reshape(1, D_MODEL), l == DEPTH - 1, tile)
    return x, new_kv


def kernel(x_prompt, x_sample, cache_k, cache_v, c, c_ctx, w_ada, b_ada, norm_g, w_in_even, pool_w,
           pool_scale, conv_w, conv_b, w_out_even, w_qkv, q_gain, k_gain, w_o, w_ffn_in, w_ffn_out,
           final_g):
    n_ctx, ctx_len, _ = x_prompt.shape
    n_lat, lat_len, _ = x_sample.shape
    assert 1 + n_lat <= COND_ROWS

    cond = jnp.concatenate(
        [c_ctx[None, :], c, jnp.zeros((COND_ROWS - 1 - n_lat, D_MODEL), F32)], axis=0)
    mod = _ada(cond, w_ada, b_ada).reshape(DEPTH, COND_ROWS, 6, D_MODEL)

    wts = dict(
        norm_g=norm_g, final_g=final_g, pool_scale=pool_scale, conv_w=conv_w, conv_b=conv_b,
        q_gain=q_gain, k_gain=k_gain,
        w_in_even=w_in_even.astype(BF16), pool_w=pool_w.astype(BF16),
        w_out_even=w_out_even.astype(BF16), w_qkv=w_qkv.astype(BF16), w_o=w_o.astype(BF16),
        w_ffn_in=w_ffn_in.astype(BF16), w_ffn_out=w_ffn_out.astype(BF16),
    )
    w_rope, qg_rope, kg_rope = _rope_weights(w_qkv, q_gain, k_gain)
    wts.update(w_qkv_rope=w_rope.astype(BF16), q_gain_rope=qg_rope, k_gain_rope=kg_rope)

    y_prompt, ctx_kv = _run_group(x_prompt, mod, 0, 0, wts, None, None,
                                  tile=ctx_len, attn_tile=ctx_len)
    new_cache_k = jnp.stack([k.reshape(n_ctx, ctx_len, N_KV_HEADS, HEAD_DIM) for k, _ in ctx_kv], 1)
    new_cache_v = jnp.stack([v.reshape(n_ctx, ctx_len, N_KV_HEADS, HEAD_DIM) for _, v in ctx_kv], 1)

    past = cache_k.shape[2]
    cache = (cache_k.reshape(n_lat, DEPTH // 2, past, KV_WIDTH),
             cache_v.reshape(n_lat, DEPTH // 2, past, KV_WIDTH))
    y_sample, _ = _run_group(x_sample, mod, 1, 1, wts, cache, _rope_tables(lat_len),
                             tile=512, attn_tile=512)
    return y_prompt, y_sample, new_cache_k, new_cache_v
```

```python
import functools
import math

import jax
import jax.numpy as jnp
import numpy as np
from jax import lax
from jax.experimental import pallas as pl
from jax.experimental.pallas import tpu as pltpu

D_MODEL = 1024
DEPTH = 2
GRID_W = 64
EPS = 1e-6
POOL_WIDTH = D_MODEL // 2
POOL_GROUPS = 4
POOL_GROUP_DIM = POOL_WIDTH // POOL_GROUPS
POOL_WINDOWS = (2, 4, 8, 16)
CONV_WIDTH = D_MODEL // 2
EVEN_IN = POOL_WIDTH + 3 * CONV_WIDTH
HEAD_DIM = 128
N_HEADS = D_MODEL // HEAD_DIM
N_KV_HEADS = N_HEADS // 4
KV_WIDTH = N_KV_HEADS * HEAD_DIM
QKV_WIDTH = (N_HEADS + 2 * N_KV_HEADS) * HEAD_DIM
ROPE_THETA = 10000.0
AXIS_DIM = HEAD_DIM // 2
D_FF = ((8 * D_MODEL // 3 + 255) // 256) * 256
SM_SCALE = HEAD_DIM ** -0.5
LOG2E = math.log2(math.e)

COND_ROWS = 8
HALO = 16
ADA_BLOCK_N = 1536
FF_CHUNKS = ((0, 1024), (1024, 1024), (2048, 768))
ATTN_KEY_CHUNK = 512
ATTN_LOOKAHEAD = 4
ATTN_FLAGS = None
VMEM_LIMIT_BYTES = 56 * 1024 * 1024

BF16 = jnp.bfloat16
F32 = jnp.float32


def _dot(a, b):
    return jnp.dot(a, b, preferred_element_type=F32)


def _dot_nt(a, b):
    return lax.dot_general(a, b, (((1,), (1,)), ((), ())), preferred_element_type=F32)


def _silu(a):
    return a * (1.0 / (1.0 + jnp.exp(-a)))


def _rms(x):
    return x * lax.rsqrt(jnp.mean(x * x, axis=-1, keepdims=True) + EPS)


def _modulate(x, g, shift, scale):
    return (_rms(x) * g) * (1.0 + scale) + shift


def _params(sem, flags=None):
    return pltpu.CompilerParams(dimension_semantics=sem, vmem_limit_bytes=VMEM_LIMIT_BYTES,
                                flags=flags)


def _const_spec(shape):
    n = len(shape)
    return pl.BlockSpec(shape, lambda *_: (0,) * n, pipeline_mode=pl.Buffered(1))


def _layer_spec(shape, layer):
    n = len(shape)
    return pl.BlockSpec((None, *shape), lambda *_: (layer,) + (0,) * n,
                        pipeline_mode=pl.Buffered(1))


def _ada_kernel(cond_ref, w_ref, b_ref, o_ref):
    s = _silu(cond_ref[...]).astype(BF16)
    o_ref[...] = _dot(s, w_ref[...].astype(BF16)) + b_ref[...]


def _ada(cond, w_ada, b_ada):
    n_out = w_ada.shape[-1]
    return pl.pallas_call(
        _ada_kernel,
        out_shape=jax.ShapeDtypeStruct((DEPTH, COND_ROWS, n_out), F32),
        grid=(DEPTH, n_out // ADA_BLOCK_N),
        in_specs=[
            pl.BlockSpec((COND_ROWS, D_MODEL), lambda l, j: (0, 0)),
            pl.BlockSpec((None, D_MODEL, ADA_BLOCK_N), lambda l, j: (l, 0, j)),
            pl.BlockSpec((None, 1, ADA_BLOCK_N), lambda l, j: (l, 0, j)),
        ],
        out_specs=pl.BlockSpec((None, COND_ROWS, ADA_BLOCK_N), lambda l, j: (l, 0, j)),
        compiler_params=_params(("arbitrary", "arbitrary")),
        name="ada",
    )(cond, w_ada, b_ada.reshape(DEPTH, 1, n_out))


def _mod_spec(layer, row0, row_stride):
    return pl.BlockSpec((None, None, 6, D_MODEL),
                        lambda s, t: (layer, row0 + row_stride * s, 0, 0))


def _even_kernel(x_ref, xp_ref, xn_ref, mod_ref, g_ref, w_in_ref, pool_w_ref, pool_scale_ref,
                 conv_w_ref, conv_b_ref, w_out_ref, o_ref, *, tile, seq_len):
    t = pl.program_id(1)
    n_t = pl.num_programs(1)
    ext = tile + 2 * HALO
    mod = mod_ref[...]
    shift, scale, gate = mod[0:1], mod[1:2], mod[2:3]
    g = g_ref[...]
    x = x_ref[...]
    h = jnp.concatenate(
        [_modulate(xp_ref[...], g, shift, scale).astype(BF16),
         _modulate(x, g, shift, scale).astype(BF16),
         _modulate(xn_ref[...], g, shift, scale).astype(BF16)], axis=0)
    p = _dot(h, w_in_ref[...])

    row = lax.broadcasted_iota(jnp.int32, (ext, 1), 0)
    first_valid = jnp.where(t > 0, 0, HALO)
    end_valid = jnp.where(t < n_t - 1, ext, HALO + tile)
    valid = jnp.logical_and(row >= first_valid, row < end_valid)
    pos = t * tile + row[HALO:HALO + tile] - HALO

    def shifted(a, s):
        return pltpu.roll(a, s % ext, 0)

    u = jnp.where(valid, p[:, :POOL_WIDTH], 0.0)
    ya = []
    for i, w in enumerate(POOL_WINDOWS):
        ug = u[:, i * POOL_GROUP_DIM:(i + 1) * POOL_GROUP_DIM]
        win = ug + shifted(ug, 1)
        half = 1
        while 2 * half < w:
            win = shifted(win, half) + shifted(win, -half)
            half *= 2
        cnt = (jnp.clip(pos + w // 2, 0, seq_len) - jnp.clip(pos - w // 2, 0, seq_len)).astype(F32)
        pooled = win[HALO:HALO + tile] / cnt - ug[HALO:HALO + tile]
        ya.append(_dot(pooled.astype(BF16), pool_w_ref[i]))
    ya = jnp.concatenate(ya, axis=-1) * pool_scale_ref[...]

    c0 = POOL_WIDTH
    bg = p[HALO:HALO + tile, c0:c0 + CONV_WIDTH]
    z = jnp.where(valid, p[:, c0 + CONV_WIDTH:c0 + 2 * CONV_WIDTH] * p[:, c0 + 2 * CONV_WIDTH:], 0.0)
    cw = conv_w_ref[...]
    conv = (shifted(z, 1) * cw[0:1] + z * cw[1:2] + shifted(z, -1) * cw[2:3])[HALO:HALO + tile]
    yb = bg * (conv + conv_b_ref[...])

    y = _dot(jnp.concatenate([ya, yb], axis=-1).astype(BF16), w_out_ref[...])
    o_ref[...] = x + gate * y


def _even_sublayer(x, mod, layer, row0, row_stride, g, w_in, pool_w, pool_scale, conv_w, conv_b,
                   w_out, tile):
    n_seq, seq_len, _ = x.shape
    n_t = seq_len // tile
    hb = tile // HALO
    n_hb = seq_len // HALO
    kern = functools.partial(_even_kernel, tile=tile, seq_len=seq_len)
    return pl.pallas_call(
        kern,
        out_shape=jax.ShapeDtypeStruct(x.shape, F32),
        grid=(n_seq, n_t),
        in_specs=[
            pl.BlockSpec((None, tile, D_MODEL), lambda s, t: (s, t, 0)),
            pl.BlockSpec((None, HALO, D_MODEL), lambda s, t: (s, jnp.maximum(t * hb - 1, 0), 0)),
            pl.BlockSpec((None, HALO, D_MODEL),
                         lambda s, t: (s, jnp.minimum((t + 1) * hb, n_hb - 1), 0)),
            _mod_spec(layer, row0, row_stride),
            _const_spec((1, D_MODEL)),
            _const_spec((D_MODEL, EVEN_IN)),
            _const_spec((POOL_GROUPS, POOL_GROUP_DIM, POOL_GROUP_DIM)),
            _const_spec((1, POOL_WIDTH)),
            _const_spec((3, CONV_WIDTH)),
            _const_spec((1, CONV_WIDTH)),
            _const_spec((POOL_WIDTH + CONV_WIDTH, D_MODEL)),
        ],
        out_specs=pl.BlockSpec((None, tile, D_MODEL), lambda s, t: (s, t, 0)),
        compiler_params=_params(("arbitrary", "arbitrary")),
        name="even_mixer",
    )(x, x, x, mod, g, w_in, pool_w, pool_scale, conv_w, conv_b, w_out)


def _ffn_kernel(x_ref, mod_ref, g_ref, w_in_ref, w_out_ref, fg_ref, o_ref, *, final_norm):
    mod = mod_ref[...]
    shift, scale, gate = mod[3:4], mod[4:5], mod[5:6]
    x = x_ref[...]
    h = _modulate(x, g_ref[...], shift, scale).astype(BF16)
    acc = None
    for c0, cw in FF_CHUNKS:
        a = _dot(h, w_in_ref[:, c0:c0 + cw])
        b = _dot(h, w_in_ref[:, D_FF + c0:D_FF + c0 + cw])
        part = _dot((_silu(a) * b).astype(BF16), w_out_ref[c0:c0 + cw, :])
        acc = part if acc is None else acc + part
    out = x + gate * acc
    if final_norm:
        out = _rms(out) * fg_ref[...]
    o_ref[...] = out


def _ffn_sublayer(x, mod, layer, row0, row_stride, g, w_in, w_out, final_g, final_norm, tile):
    n_seq, seq_len, _ = x.shape
    kern = functools.partial(_ffn_kernel, final_norm=final_norm)
    return pl.pallas_call(
        kern,
        out_shape=jax.ShapeDtypeStruct(x.shape, F32),
        grid=(n_seq, seq_len // tile),
        in_specs=[
            pl.BlockSpec((None, tile, D_MODEL), lambda s, t: (s, t, 0)),
            _mod_spec(layer, row0, row_stride),
            _const_spec((1, D_MODEL)),
            _layer_spec((D_MODEL, 2 * D_FF), layer),
            _layer_spec((D_FF, D_MODEL), layer),
            _const_spec((1, D_MODEL)),
        ],
        out_specs=pl.BlockSpec((None, tile, D_MODEL), lambda s, t: (s, t, 0)),
        compiler_params=_params(("arbitrary", "arbitrary")),
        name="ffn",
    )(x, mod, g, w_in, w_out, final_g)


def _rope_lane_order(a):
    lead = a.shape[:-1]
    return a.reshape(*lead, 2, 2, AXIS_DIM // 2).swapaxes(-3, -2).reshape(*lead, HEAD_DIM)


def _rope(a, cos, sin_signed):
    return a * cos + pltpu.roll(a, HEAD_DIM // 2, 1) * sin_signed


def _pair_mean_matrix():
    r = lax.broadcasted_iota(jnp.int32, (2 * HEAD_DIM, 2 * HEAD_DIM), 0) // HEAD_DIM
    c = lax.broadcasted_iota(jnp.int32, (2 * HEAD_DIM, 2 * HEAD_DIM), 1) // HEAD_DIM
    return jnp.where(r == c, 1.0 / HEAD_DIM, 0.0).astype(BF16)


def _rope_lane_matrix():
    src = lax.broadcasted_iota(jnp.int32, (HEAD_DIM, HEAD_DIM), 0)
    n = lax.broadcasted_iota(jnp.int32, (HEAD_DIM, HEAD_DIM), 1)
    quarter = AXIS_DIM // 2
    half, axis, c = n // AXIS_DIM, (n // quarter) % 2, n % quarter
    return jnp.where(src == axis * AXIS_DIM + half * quarter + c, 1.0, 0.0).astype(BF16)


def _qkv_kernel(*refs, rope, cached, keep_f32):
    it = iter(refs)
    x_ref, mod_ref, g_ref, w_ref, qg_ref, kg_ref = (next(it) for _ in range(6))
    rope_refs = [next(it) for _ in range(2)] if rope else None
    cache_refs = [next(it) for _ in range(2)] if cached else None
    q_ref, kh_ref, vt_ref = (next(it) for _ in range(3))
    f32_refs = [next(it) for _ in range(2)] if keep_f32 else None

    def project():
        mod = mod_ref[...]
        shift, scale = mod[0:1], mod[1:2]
        h = _modulate(x_ref[...], g_ref[...], shift, scale).astype(BF16)
        p = _dot(h, w_ref[...])
        if rope:
            tables = [r[...] for r in rope_refs]
        gains = [qg_ref[...] * (SM_SCALE * LOG2E)] * N_HEADS + [kg_ref[...]] * N_KV_HEADS
        pair_mean = _pair_mean_matrix()
        v0 = (N_HEADS + N_KV_HEADS) * HEAD_DIM
        for pair in range((N_HEADS + N_KV_HEADS) // 2):
            blk = p[:, pair * 2 * HEAD_DIM:(pair + 1) * 2 * HEAD_DIM]
            ms = _dot((blk * blk).astype(BF16), pair_mean)
            normed = blk * lax.rsqrt(ms + EPS)
            for sub in range(2):
                hd = 2 * pair + sub
                a = normed[:, sub * HEAD_DIM:(sub + 1) * HEAD_DIM] * gains[hd]
                if rope:
                    a = _rope(a, *tables)
                if hd < N_HEADS:
                    q_ref[hd] = a.astype(BF16)
                else:
                    kv = hd - N_HEADS
                    v = p[:, v0 + kv * HEAD_DIM:v0 + (kv + 1) * HEAD_DIM]
                    kh_ref[kv] = a.astype(BF16)
                    vt_ref[kv] = v.T.astype(BF16)
                    if keep_f32:
                        f32_refs[0][:, kv, :] = a
                        f32_refs[1][:, kv, :] = v

    if not cached:
        project()
        return

    t = pl.program_id(1)

    @pl.when(t == 0)
    def _():
        for kv in range(N_KV_HEADS):
            k = cache_refs[0][:, kv, :].astype(BF16)
            if rope:
                k = _dot(k, _rope_lane_matrix()).astype(BF16)
            kh_ref[kv] = k
            vt_ref[kv] = cache_refs[1][:, kv, :].T.astype(BF16)

    pl.when(t > 0)(project)


def _qkv_proj(x, mod, layer, row0, row_stride, g, w_qkv, q_gain, k_gain, rope_tables, cache,
              cache_layer, keep_f32, tile):
    n_seq, seq_len, _ = x.shape
    rope = rope_tables is not None
    cached = cache is not None
    lead = 0
    if cached:
        assert cache[0].shape[2] == tile
        lead = 1
    n_keys = seq_len + lead * tile
    tok = lambda t: jnp.maximum(t - lead, 0)
    in_specs = [
        pl.BlockSpec((None, tile, D_MODEL), lambda s, t: (s, tok(t), 0)),
        _mod_spec(layer, row0, row_stride),
        _const_spec((1, D_MODEL)),
        _const_spec((D_MODEL, QKV_WIDTH)),
        _const_spec((1, HEAD_DIM)),
        _const_spec((1, HEAD_DIM)),
    ]
    args = [x, mod, g, w_qkv, q_gain, k_gain]
    if rope:
        in_specs += [pl.BlockSpec((tile, HEAD_DIM), lambda s, t: (tok(t), 0))] * 2
        args += list(rope_tables)
    if cached:
        in_specs += [pl.BlockSpec((None, None, tile, N_KV_HEADS, HEAD_DIM),
                                  lambda s, t: (s, cache_layer, 0, 0, 0))] * 2
        args += list(cache)
    out_shape = [jax.ShapeDtypeStruct((n_seq, N_HEADS, seq_len, HEAD_DIM), BF16),
                 jax.ShapeDtypeStruct((n_seq, N_KV_HEADS, n_keys, HEAD_DIM), BF16),
                 jax.ShapeDtypeStruct((n_seq, N_KV_HEADS, HEAD_DIM, n_keys), BF16)]
    out_specs = [pl.BlockSpec((None, N_HEADS, tile, HEAD_DIM), lambda s, t: (s, 0, tok(t), 0)),
                 pl.BlockSpec((None, N_KV_HEADS, tile, HEAD_DIM), lambda s, t: (s, 0, t, 0)),
                 pl.BlockSpec((None, N_KV_HEADS, HEAD_DIM, tile), lambda s, t: (s, 0, 0, t))]
    if keep_f32:
        assert not cached
        out_shape += [jax.ShapeDtypeStruct((n_seq, 1, seq_len, N_KV_HEADS, HEAD_DIM), F32)] * 2
        out_specs += [pl.BlockSpec((None, None, tile, N_KV_HEADS, HEAD_DIM),
                                   lambda s, t: (s, 0, t, 0, 0))] * 2
    return pl.pallas_call(
        functools.partial(_qkv_kernel, rope=rope, cached=cached, keep_f32=keep_f32),
        out_shape=tuple(out_shape),
        grid=(n_seq, seq_len // tile + lead),
        in_specs=in_specs,
        out_specs=tuple(out_specs),
        compiler_params=_params(("arbitrary", "arbitrary")),
        name="qkv_proj",
    )(*args)


def _attn_kernel(q_ref, kh_ref, vt_ref, x_ref, mod_ref, w_o_ref, o_ref, *, key_chunk):
    group = N_HEADS // N_KV_HEADS
    n_chunks = kh_ref.shape[1] // key_chunk
    items = [(hd, c) for hd in range(N_HEADS) for c in range(n_chunks)]
    look = n_chunks + ATTN_LOOKAHEAD
    scores, col_max = {}, {}

    def issue_scores(j):
        hd, c = items[j]
        keys = kh_ref[hd // group, c * key_chunk:(c + 1) * key_chunk, :]
        s = _dot_nt(keys, q_ref[hd])
        scores[j] = s
        cm = jnp.max(s, axis=0, keepdims=True)
        col_max[hd] = cm if c == 0 else jnp.maximum(col_max[hd], cm)

    for j in range(min(look, len(items))):
        issue_scores(j)
    outs = []
    for i, (hd, c) in enumerate(items):
        if i + look < len(items):
            issue_scores(i + look)
        p = jnp.exp2(scores.pop(i) - col_max[hd])
        ps = jnp.sum(p, axis=0, keepdims=True)
        pv = _dot(vt_ref[hd // group, :, c * key_chunk:(c + 1) * key_chunk], p.astype(BF16))
        denom, acc = (ps, pv) if c == 0 else (denom + ps, acc + pv)
        if c == n_chunks - 1:
            outs.append((acc * (1.0 / denom)).T.astype(BF16))
    y = _dot(jnp.concatenate(outs, axis=-1), w_o_ref[...])
    o_ref[...] = x_ref[...] + mod_ref[2:3] * y


def _attn_sublayer(q, kh, vt, x, mod, layer, row0, row_stride, w_o, tile):
    n_seq, seq_len, _ = x.shape
    n_keys = kh.shape[2]
    return pl.pallas_call(
        functools.partial(_attn_kernel, key_chunk=min(n_keys, ATTN_KEY_CHUNK)),
        out_shape=jax.ShapeDtypeStruct(x.shape, F32),
        grid=(n_seq, seq_len // tile),
        in_specs=[
            pl.BlockSpec((None, N_HEADS, tile, HEAD_DIM), lambda s, t: (s, 0, t, 0)),
            pl.BlockSpec((None, N_KV_HEADS, n_keys, HEAD_DIM), lambda s, t: (s, 0, 0, 0)),
            pl.BlockSpec((None, N_KV_HEADS, HEAD_DIM, n_keys), lambda s, t: (s, 0, 0, 0)),
            pl.BlockSpec((None, tile, D_MODEL), lambda s, t: (s, t, 0)),
            _mod_spec(layer, row0, row_stride),
            _const_spec((D_MODEL, D_MODEL)),
        ],
        out_specs=pl.BlockSpec((None, tile, D_MODEL), lambda s, t: (s, t, 0)),
        compiler_params=_params(("arbitrary", "arbitrary"), ATTN_FLAGS),
        name="attention",
    )(q, kh, vt, x, mod, w_o)


def _rope_tables(seq_len):
    rows = seq_len // GRID_W
    r = np.repeat(np.arange(rows), GRID_W).astype(np.float32)
    col = np.tile(np.arange(GRID_W), rows).astype(np.float32)
    expo = -np.arange(0, AXIS_DIM, 2, dtype=np.float32) / np.float32(AXIS_DIM)
    inv = np.power(np.float32(ROPE_THETA), expo).astype(np.float32)
    ar = r[:, None] * inv
    ac = col[:, None] * inv
    ang = np.concatenate([ar, ar, ac, ac], axis=-1)
    cos, sin = np.cos(ang), np.sin(ang)
    first = (np.arange(HEAD_DIM) % AXIS_DIM) < AXIS_DIM // 2
    return (jnp.asarray(_rope_lane_order(cos), F32),
            jnp.asarray(_rope_lane_order(np.where(first, -sin, sin)), F32))


def _rope_weights(w_qkv, q_gain, k_gain):
    qk = (N_HEADS + N_KV_HEADS) * HEAD_DIM
    lead = w_qkv.shape[:-1]
    w_qk = _rope_lane_order(w_qkv[..., :qk].reshape(*lead, N_HEADS + N_KV_HEADS, HEAD_DIM))
    w = jnp.concatenate([w_qk.reshape(*lead, qk), w_qkv[..., qk:]], axis=-1)
    return w, _rope_lane_order(q_gain), _rope_lane_order(k_gain)


def _run_group(x, mod, row0, row_stride, wts, cache, rope_tables, tile, attn_tile):
    new_kv = []
    for l in range(DEPTH):
        i = l // 2
        g1 = wts["norm_g"][l, 0].reshape(1, D_MODEL)
        g2 = wts["norm_g"][l, 1].reshape(1, D_MODEL)
        if l % 2 == 0:
            x = _even_sublayer(x, mod, l, row0, row_stride, g1, wts["w_in_even"][i],
                               wts["pool_w"][i], wts["pool_scale"][i].reshape(1, POOL_WIDTH),
                               wts["conv_w"][i], wts["conv_b"][i].reshape(1, CONV_WIDTH),
                               wts["w_out_even"][i], tile)
        else:
            sfx = "" if rope_tables is None else "_rope"
            res = _qkv_proj(x, mod, l, row0, row_stride, g1, wts["w_qkv" + sfx][i],
                            wts["q_gain" + sfx][i].reshape(1, HEAD_DIM),
                            wts["k_gain" + sfx][i].reshape(1, HEAD_DIM), rope_tables, cache, i,
                            cache is None, tile)
            q, kh, vt = res[:3]
            if cache is None:
                new_kv.append(res[3:])
            x = _attn_sublayer(q, kh, vt, x, mod, l, row0, row_stride, wts["w_o"][i], attn_tile)
        x = _ffn_sublayer(x, mod, l, row0, row_stride, g2, wts["w_ffn_in"], wts["w_ffn_out"],
                          wts["final_g"].reshape(1, D_MODEL), l == DEPTH - 1, tile)
    return x, new_kv


def kernel(x_prompt, x_sample, cache_k, cache_v, c, c_ctx, w_ada, b_ada, norm_g, w_in_even, pool_w,
           pool_scale, conv_w, conv_b, w_out_even, w_qkv, q_gain, k_gain, w_o, w_ffn_in, w_ffn_out,
           final_g):
    n_ctx, ctx_len, _ = x_prompt.shape
    n_lat, lat_len, _ = x_sample.shape
    assert 1 + n_lat <= COND_ROWS

    cond = jnp.concatenate(
        [c_ctx[None, :], c, jnp.zeros((COND_ROWS - 1 - n_lat, D_MODEL), F32)], axis=0)
    mod = _ada(cond, w_ada, b_ada).reshape(DEPTH, COND_ROWS, 6, D_MODEL)

    wts = dict(
        norm_g=norm_g, final_g=final_g, pool_scale=pool_scale, conv_w=conv_w, conv_b=conv_b,
        q_gain=q_gain, k_gain=k_gain,
        w_in_even=w_in_even.astype(BF16), pool_w=pool_w.astype(BF16),
        w_out_even=w_out_even.astype(BF16), w_qkv=w_qkv.astype(BF16), w_o=w_o.astype(BF16),
        w_ffn_in=w_ffn_in.astype(BF16), w_ffn_out=w_ffn_out.astype(BF16),
    )
    w_rope, qg_rope, kg_rope = _rope_weights(wts["w_qkv"], q_gain, k_gain)
    wts.update(w_qkv_rope=w_rope, q_gain_rope=qg_rope, k_gain_rope=kg_rope)

    y_prompt, ctx_kv = _run_group(x_prompt, mod, 0, 0, wts, None, None,
                                  tile=ctx_len, attn_tile=ctx_len)
    new_cache_k = jnp.concatenate([k for k, _ in ctx_kv], axis=1)
    new_cache_v = jnp.concatenate([v for _, v in ctx_kv], axis=1)

    y_sample, _ = _run_group(x_sample, mod, 1, 1, wts, (cache_k, cache_v), _rope_tables(lat_len),
                             tile=512, attn_tile=512)
    return y_prompt, y_sample, new_cache_k, new_cache_v
```

```python
import functools
import math

import jax
import jax.numpy as jnp
import numpy as np
from jax import lax
from jax.experimental import pallas as pl
from jax.experimental.pallas import tpu as pltpu

D_MODEL = 1024
DEPTH = 2
GRID_W = 64
EPS = 1e-6
POOL_WIDTH = D_MODEL // 2
POOL_GROUPS = 4
POOL_GROUP_DIM = POOL_WIDTH // POOL_GROUPS
POOL_WINDOWS = (2, 4, 8, 16)
CONV_WIDTH = D_MODEL // 2
EVEN_IN = POOL_WIDTH + 3 * CONV_WIDTH
HEAD_DIM = 128
N_HEADS = D_MODEL // HEAD_DIM
N_KV_HEADS = N_HEADS // 4
KV_WIDTH = N_KV_HEADS * HEAD_DIM
QKV_WIDTH = (N_HEADS + 2 * N_KV_HEADS) * HEAD_DIM
ROPE_THETA = 10000.0
AXIS_DIM = HEAD_DIM // 2
D_FF = ((8 * D_MODEL // 3 + 255) // 256) * 256
SM_SCALE = HEAD_DIM ** -0.5
LOG2E = math.log2(math.e)

COND_ROWS = 8
HALO = 16
ADA_BLOCK_N = 1536
FF_CHUNKS = ((0, 1024), (1024, 1024), (2048, 768))
ATTN_KEY_CHUNK = 512
ATTN_LOOKAHEAD = 2
BOUND_SLACK = 1.1
MAX_SAFE_SHIFT = 60.0
VMEM_LIMIT_BYTES = 56 * 1024 * 1024

BF16 = jnp.bfloat16
F32 = jnp.float32


def _dot(a, b):
    return jnp.dot(a, b, preferred_element_type=F32)


def _dot_nt(a, b):
    return lax.dot_general(a, b, (((1,), (1,)), ((), ())), preferred_element_type=F32)


def _silu(a):
    return a * (1.0 / (1.0 + jnp.exp(-a)))


def _rms(x):
    return x * lax.rsqrt(jnp.mean(x * x, axis=-1, keepdims=True) + EPS)


def _modulate(x, g, shift, scale):
    return (_rms(x) * g) * (1.0 + scale) + shift


def _params(sem):
    return pltpu.CompilerParams(dimension_semantics=sem, vmem_limit_bytes=VMEM_LIMIT_BYTES)


def _const_spec(shape):
    n = len(shape)
    return pl.BlockSpec(shape, lambda *_: (0,) * n, pipeline_mode=pl.Buffered(1))


def _layer_spec(shape, layer):
    n = len(shape)
    return pl.BlockSpec((None, *shape), lambda *_: (layer,) + (0,) * n,
                        pipeline_mode=pl.Buffered(1))


def _ada_kernel(cond_ref, w_ref, b_ref, o_ref):
    s = _silu(cond_ref[...]).astype(BF16)
    o_ref[...] = _dot(s, w_ref[...].astype(BF16)) + b_ref[...]


def _ada(cond, w_ada, b_ada):
    n_out = w_ada.shape[-1]
    return pl.pallas_call(
        _ada_kernel,
        out_shape=jax.ShapeDtypeStruct((DEPTH, COND_ROWS, n_out), F32),
        grid=(DEPTH, n_out // ADA_BLOCK_N),
        in_specs=[
            pl.BlockSpec((COND_ROWS, D_MODEL), lambda l, j: (0, 0)),
            pl.BlockSpec((None, D_MODEL, ADA_BLOCK_N), lambda l, j: (l, 0, j)),
            pl.BlockSpec((None, 1, ADA_BLOCK_N), lambda l, j: (l, 0, j)),
        ],
        out_specs=pl.BlockSpec((None, COND_ROWS, ADA_BLOCK_N), lambda l, j: (l, 0, j)),
        compiler_params=_params(("arbitrary", "arbitrary")),
        name="ada",
    )(cond, w_ada, b_ada.reshape(DEPTH, 1, n_out))


def _mod_spec(layer, row0, row_stride):
    return pl.BlockSpec((None, None, 6, D_MODEL),
                        lambda s, t: (layer, row0 + row_stride * s, 0, 0))


def _even_kernel(x_ref, xp_ref, xn_ref, mod_ref, g_ref, w_in_ref, pool_w_ref, pool_scale_ref,
                 conv_w_ref, conv_b_ref, w_out_ref, o_ref, *, tile, seq_len):
    t = pl.program_id(1)
    n_t = pl.num_programs(1)
    ext = tile + 2 * HALO
    mod = mod_ref[...]
    shift, scale, gate = mod[0:1], mod[1:2], mod[2:3]
    g = g_ref[...]
    x = x_ref[...]
    h = jnp.concatenate(
        [_modulate(xp_ref[...], g, shift, scale).astype(BF16),
         _modulate(x, g, shift, scale).astype(BF16),
         _modulate(xn_ref[...], g, shift, scale).astype(BF16)], axis=0)
    p = _dot(h, w_in_ref[...])

    row = lax.broadcasted_iota(jnp.int32, (ext, 1), 0)
    first_valid = jnp.where(t > 0, 0, HALO)
    end_valid = jnp.where(t < n_t - 1, ext, HALO + tile)
    valid = jnp.logical_and(row >= first_valid, row < end_valid)
    pos = t * tile + row[HALO:HALO + tile] - HALO

    def shifted(a, s):
        return pltpu.roll(a, s % ext, 0)

    u = jnp.where(valid, p[:, :POOL_WIDTH], 0.0)
    ya = []
    for i, w in enumerate(POOL_WINDOWS):
        ug = u[:, i * POOL_GROUP_DIM:(i + 1) * POOL_GROUP_DIM]
        win = ug + shifted(ug, 1)
        half = 1
        while 2 * half < w:
            win = shifted(win, half) + shifted(win, -half)
            half *= 2
        cnt = (jnp.clip(pos + w // 2, 0, seq_len) - jnp.clip(pos - w // 2, 0, seq_len)).astype(F32)
        pooled = win[HALO:HALO + tile] / cnt - ug[HALO:HALO + tile]
        ya.append(_dot(pooled.astype(BF16), pool_w_ref[i]))
    ya = jnp.concatenate(ya, axis=-1) * pool_scale_ref[...]

    c0 = POOL_WIDTH
    bg = p[HALO:HALO + tile, c0:c0 + CONV_WIDTH]
    z = jnp.where(valid, p[:, c0 + CONV_WIDTH:c0 + 2 * CONV_WIDTH] * p[:, c0 + 2 * CONV_WIDTH:], 0.0)
    cw = conv_w_ref[...]
    conv = (shifted(z, 1) * cw[0:1] + z * cw[1:2] + shifted(z, -1) * cw[2:3])[HALO:HALO + tile]
    yb = bg * (conv + conv_b_ref[...])

    y = _dot(jnp.concatenate([ya, yb], axis=-1).astype(BF16), w_out_ref[...])
    o_ref[...] = x + gate * y


def _even_sublayer(x, mod, layer, row0, row_stride, g, w_in, pool_w, pool_scale, conv_w, conv_b,
                   w_out, tile):
    n_seq, seq_len, _ = x.shape
    n_t = seq_len // tile
    hb = tile // HALO
    n_hb = seq_len // HALO
    kern = functools.partial(_even_kernel, tile=tile, seq_len=seq_len)
    return pl.pallas_call(
        kern,
        out_shape=jax.ShapeDtypeStruct(x.shape, F32),
        grid=(n_seq, n_t),
        in_specs=[
            pl.BlockSpec((None, tile, D_MODEL), lambda s, t: (s, t, 0)),
            pl.BlockSpec((None, HALO, D_MODEL), lambda s, t: (s, jnp.maximum(t * hb - 1, 0), 0)),
            pl.BlockSpec((None, HALO, D_MODEL),
                         lambda s, t: (s, jnp.minimum((t + 1) * hb, n_hb - 1), 0)),
            _mod_spec(layer, row0, row_stride),
            _const_spec((1, D_MODEL)),
            _const_spec((D_MODEL, EVEN_IN)),
            _const_spec((POOL_GROUPS, POOL_GROUP_DIM, POOL_GROUP_DIM)),
            _const_spec((1, POOL_WIDTH)),
            _const_spec((3, CONV_WIDTH)),
            _const_spec((1, CONV_WIDTH)),
            _const_spec((POOL_WIDTH + CONV_WIDTH, D_MODEL)),
        ],
        out_specs=pl.BlockSpec((None, tile, D_MODEL), lambda s, t: (s, t, 0)),
        compiler_params=_params(("arbitrary", "arbitrary")),
        name="even_mixer",
    )(x, x, x, mod, g, w_in, pool_w, pool_scale, conv_w, conv_b, w_out)


def _ffn_kernel(x_ref, mod_ref, g_ref, w_in_ref, w_out_ref, fg_ref, o_ref, *, final_norm):
    mod = mod_ref[...]
    shift, scale, gate = mod[3:4], mod[4:5], mod[5:6]
    x = x_ref[...]
    h = _modulate(x, g_ref[...], shift, scale).astype(BF16)
    acc = None
    for c0, cw in FF_CHUNKS:
        a = _dot(h, w_in_ref[:, c0:c0 + cw])
        b = _dot(h, w_in_ref[:, D_FF + c0:D_FF + c0 + cw])
        part = _dot((_silu(a) * b).astype(BF16), w_out_ref[c0:c0 + cw, :])
        acc = part if acc is None else acc + part
    out = x + gate * acc
    if final_norm:
        out = _rms(out) * fg_ref[...]
    o_ref[...] = out


def _ffn_sublayer(x, mod, layer, row0, row_stride, g, w_in, w_out, final_g, final_norm, tile):
    n_seq, seq_len, _ = x.shape
    kern = functools.partial(_ffn_kernel, final_norm=final_norm)
    return pl.pallas_call(
        kern,
        out_shape=jax.ShapeDtypeStruct(x.shape, F32),
        grid=(n_seq, seq_len // tile),
        in_specs=[
            pl.BlockSpec((None, tile, D_MODEL), lambda s, t: (s, t, 0)),
            _mod_spec(layer, row0, row_stride),
            _const_spec((1, D_MODEL)),
            _layer_spec((D_MODEL, 2 * D_FF), layer),
            _layer_spec((D_FF, D_MODEL), layer),
            _const_spec((1, D_MODEL)),
        ],
        out_specs=pl.BlockSpec((None, tile, D_MODEL), lambda s, t: (s, t, 0)),
        compiler_params=_params(("arbitrary", "arbitrary")),
        name="ffn",
    )(x, mod, g, w_in, w_out, final_g)


def _rope_lane_order(a):
    lead = a.shape[:-1]
    return a.reshape(*lead, 2, 2, AXIS_DIM // 2).swapaxes(-3, -2).reshape(*lead, HEAD_DIM)


def _rope(a, cos, sin_signed):
    return a * cos + pltpu.roll(a, HEAD_DIM // 2, 1) * sin_signed


def _pair_mean_matrix():
    r = lax.broadcasted_iota(jnp.int32, (2 * HEAD_DIM, 2 * HEAD_DIM), 0) // HEAD_DIM
    c = lax.broadcasted_iota(jnp.int32, (2 * HEAD_DIM, 2 * HEAD_DIM), 1) // HEAD_DIM
    return jnp.where(r == c, 1.0 / HEAD_DIM, 0.0).astype(BF16)


def _rope_lane_matrix():
    src = lax.broadcasted_iota(jnp.int32, (HEAD_DIM, HEAD_DIM), 0)
    n = lax.broadcasted_iota(jnp.int32, (HEAD_DIM, HEAD_DIM), 1)
    quarter = AXIS_DIM // 2
    half, axis, c = n // AXIS_DIM, (n // quarter) % 2, n % quarter
    return jnp.where(src == axis * AXIS_DIM + half * quarter + c, 1.0, 0.0).astype(BF16)


def _qkv_kernel(*refs, rope, cached, keep_f32):
    it = iter(refs)
    x_ref, mod_ref, g_ref, w_ref, qg_ref, kg_ref = (next(it) for _ in range(6))
    rope_refs = [next(it) for _ in range(2)] if rope else None
    cache_refs = [next(it) for _ in range(2)] if cached else None
    q_ref, kh_ref, vt_ref = (next(it) for _ in range(3))
    f32_refs = [next(it) for _ in range(2)] if keep_f32 else None

    def project():
        mod = mod_ref[...]
        shift, scale = mod[0:1], mod[1:2]
        h = _modulate(x_ref[...], g_ref[...], shift, scale).astype(BF16)
        p = _dot(h, w_ref[...])
        if rope:
            tables = [r[...] for r in rope_refs]
        gains = [qg_ref[...] * (SM_SCALE * LOG2E)] * N_HEADS + [kg_ref[...]] * N_KV_HEADS
        pair_mean = _pair_mean_matrix()
        v0 = (N_HEADS + N_KV_HEADS) * HEAD_DIM
        for pair in range((N_HEADS + N_KV_HEADS) // 2):
            blk = p[:, pair * 2 * HEAD_DIM:(pair + 1) * 2 * HEAD_DIM]
            ms = _dot((blk * blk).astype(BF16), pair_mean)
            normed = blk * lax.rsqrt(ms + EPS)
            for sub in range(2):
                hd = 2 * pair + sub
                a = normed[:, sub * HEAD_DIM:(sub + 1) * HEAD_DIM] * gains[hd]
                if rope:
                    a = _rope(a, *tables)
                if hd < N_HEADS:
                    q_ref[hd] = a.astype(BF16)
                else:
                    kv = hd - N_HEADS
                    v = p[:, v0 + kv * HEAD_DIM:v0 + (kv + 1) * HEAD_DIM]
                    kh_ref[kv] = a.astype(BF16)
                    vt_ref[kv] = v.T.astype(BF16)
                    if keep_f32:
                        f32_refs[0][:, kv, :] = a
                        f32_refs[1][:, kv, :] = v

    if not cached:
        project()
        return

    t = pl.program_id(1)

    @pl.when(t == 0)
    def _():
        for kv in range(N_KV_HEADS):
            k = cache_refs[0][:, kv, :].astype(BF16)
            if rope:
                k = _dot(k, _rope_lane_matrix()).astype(BF16)
            kh_ref[kv] = k
            vt_ref[kv] = cache_refs[1][:, kv, :].T.astype(BF16)

    pl.when(t > 0)(project)


def _qkv_proj(x, mod, layer, row0, row_stride, g, w_qkv, q_gain, k_gain, rope_tables, cache,
              cache_layer, keep_f32, tile):
    n_seq, seq_len, _ = x.shape
    rope = rope_tables is not None
    cached = cache is not None
    lead = 0
    if cached:
        assert cache[0].shape[2] == tile
        lead = 1
    n_keys = seq_len + lead * tile
    tok = lambda t: jnp.maximum(t - lead, 0)
    in_specs = [
        pl.BlockSpec((None, tile, D_MODEL), lambda s, t: (s, tok(t), 0)),
        _mod_spec(layer, row0, row_stride),
        _const_spec((1, D_MODEL)),
        _const_spec((D_MODEL, QKV_WIDTH)),
        _const_spec((1, HEAD_DIM)),
        _const_spec((1, HEAD_DIM)),
    ]
    args = [x, mod, g, w_qkv, q_gain, k_gain]
    if rope:
        in_specs += [pl.BlockSpec((tile, HEAD_DIM), lambda s, t: (tok(t), 0))] * 2
        args += list(rope_tables)
    if cached:
        in_specs += [pl.BlockSpec((None, None, tile, N_KV_HEADS, HEAD_DIM),
                                  lambda s, t: (s, cache_layer, 0, 0, 0))] * 2
        args += list(cache)
    out_shape = [jax.ShapeDtypeStruct((n_seq, N_HEADS, seq_len, HEAD_DIM), BF16),
                 jax.ShapeDtypeStruct((n_seq, N_KV_HEADS, n_keys, HEAD_DIM), BF16),
                 jax.ShapeDtypeStruct((n_seq, N_KV_HEADS, HEAD_DIM, n_keys), BF16)]
    out_specs = [pl.BlockSpec((None, N_HEADS, tile, HEAD_DIM), lambda s, t: (s, 0, tok(t), 0)),
                 pl.BlockSpec((None, N_KV_HEADS, tile, HEAD_DIM), lambda s, t: (s, 0, t, 0)),
                 pl.BlockSpec((None, N_KV_HEADS, HEAD_DIM, tile), lambda s, t: (s, 0, 0, t))]
    if keep_f32:
        assert not cached
        out_shape += [jax.ShapeDtypeStruct((n_seq, 1, seq_len, N_KV_HEADS, HEAD_DIM), F32)] * 2
        out_specs += [pl.BlockSpec((None, None, tile, N_KV_HEADS, HEAD_DIM),
                                   lambda s, t: (s, 0, t, 0, 0))] * 2
    return pl.pallas_call(
        functools.partial(_qkv_kernel, rope=rope, cached=cached, keep_f32=keep_f32),
        out_shape=tuple(out_shape),
        grid=(n_seq, seq_len // tile + lead),
        in_specs=in_specs,
        out_specs=tuple(out_specs),
        compiler_params=_params(("arbitrary", "arbitrary")),
        name="qkv_proj",
    )(*args)


def _attn_kernel(q_ref, kh_ref, vt_ref, x_ref, mod_ref, w_o_ref, qg_ref, o_ref, heads_scr, shift_scr,
                 safe_scr, *, key_chunk):
    group = N_HEADS // N_KV_HEADS
    n_chunks = kh_ref.shape[1] // key_chunk

    @pl.when(pl.program_id(1) == 0)
    def _():
        qg = qg_ref[...]
        q_sq = HEAD_DIM * (SM_SCALE * LOG2E) ** 2 * jnp.max(qg * qg)
        for kv in range(N_KV_HEADS):
            k = kh_ref[kv].astype(F32)
            k_sq = jnp.max(jnp.sum(k * k, axis=1, keepdims=True))
            bound_sq = q_sq * k_sq * BOUND_SLACK
            shift_scr[kv] = jnp.full(shift_scr.shape[1:], jnp.sqrt(bound_sq), F32)
            safe_scr[kv] = (bound_sq <= MAX_SAFE_SHIFT ** 2).astype(jnp.int32)

    def bounded():
        items = [(hd, c) for hd in range(N_HEADS) for c in range(n_chunks)]
        scores = {}

        def issue_scores(j):
            hd, c = items[j]
            keys = kh_ref[hd // group, c * key_chunk:(c + 1) * key_chunk, :]
            scores[j] = _dot_nt(keys, q_ref[hd])

        for j in range(min(ATTN_LOOKAHEAD, len(items))):
            issue_scores(j)
        for i, (hd, c) in enumerate(items):
            if i + ATTN_LOOKAHEAD < len(items):
                issue_scores(i + ATTN_LOOKAHEAD)
            p = jnp.exp2(scores.pop(i) - shift_scr[hd // group][0:1, 0:1])
            ps = jnp.sum(p, axis=0, keepdims=True)
            pv = _dot(vt_ref[hd // group, :, c * key_chunk:(c + 1) * key_chunk], p.astype(BF16))
            denom, acc = (ps, pv) if c == 0 else (denom + ps, acc + pv)
            if c == n_chunks - 1:
                heads_scr[hd] = (acc * (1.0 / denom)).T.astype(BF16)

    def exact():
        def one_head(hd, carry):
            kv = hd // group
            s = _dot_nt(kh_ref[kv], q_ref[hd])
            p = jnp.exp2(s - jnp.max(s, axis=0, keepdims=True))
            denom = jnp.sum(p, axis=0, keepdims=True)
            o_t = _dot(vt_ref[kv], p.astype(BF16)) * (1.0 / denom)
            heads_scr[hd] = o_t.T.astype(BF16)
            return carry

        lax.fori_loop(0, N_HEADS, one_head, 0)

    safe = safe_scr[0]
    for kv in range(1, N_KV_HEADS):
        safe = jnp.minimum(safe, safe_scr[kv])
    pl.when(safe == 1)(bounded)
    pl.when(safe != 1)(exact)

    attn = jnp.concatenate([heads_scr[hd] for hd in range(N_HEADS)], axis=-1)
    o_ref[...] = x_ref[...] + mod_ref[2:3] * _dot(attn, w_o_ref[...])


def _attn_sublayer(q, kh, vt, x, mod, layer, row0, row_stride, w_o, q_gain, tile):
    n_seq, seq_len, _ = x.shape
    n_keys = kh.shape[2]
    return pl.pallas_call(
        functools.partial(_attn_kernel, key_chunk=min(n_keys, ATTN_KEY_CHUNK)),
        out_shape=jax.ShapeDtypeStruct(x.shape, F32),
        grid=(n_seq, seq_len // tile),
        in_specs=[
            pl.BlockSpec((None, N_HEADS, tile, HEAD_DIM), lambda s, t: (s, 0, t, 0)),
            pl.BlockSpec((None, N_KV_HEADS, n_keys, HEAD_DIM), lambda s, t: (s, 0, 0, 0)),
            pl.BlockSpec((None, N_KV_HEADS, HEAD_DIM, n_keys), lambda s, t: (s, 0, 0, 0)),
            pl.BlockSpec((None, tile, D_MODEL), lambda s, t: (s, t, 0)),
            _mod_spec(layer, row0, row_stride),
            _const_spec((D_MODEL, D_MODEL)),
            _const_spec((1, HEAD_DIM)),
        ],
        out_specs=pl.BlockSpec((None, tile, D_MODEL), lambda s, t: (s, t, 0)),
        scratch_shapes=[pltpu.VMEM((N_HEADS, tile, HEAD_DIM), BF16),
                        pltpu.VMEM((N_KV_HEADS, 8, HEAD_DIM), F32),
                        pltpu.SMEM((N_KV_HEADS,), jnp.int32)],
        compiler_params=_params(("arbitrary", "arbitrary")),
        name="attention",
    )(q, kh, vt, x, mod, w_o, q_gain)


def _rope_tables(seq_len):
    rows = seq_len // GRID_W
    r = np.repeat(np.arange(rows), GRID_W).astype(np.float32)
    col = np.tile(np.arange(GRID_W), rows).astype(np.float32)
    expo = -np.arange(0, AXIS_DIM, 2, dtype=np.float32) / np.float32(AXIS_DIM)
    inv = np.power(np.float32(ROPE_THETA), expo).astype(np.float32)
    ar = r[:, None] * inv
    ac = col[:, None] * inv
    ang = np.concatenate([ar, ar, ac, ac], axis=-1)
    cos, sin = np.cos(ang), np.sin(ang)
    first = (np.arange(HEAD_DIM) % AXIS_DIM) < AXIS_DIM // 2
    return (jnp.asarray(_rope_lane_order(cos), F32),
            jnp.asarray(_rope_lane_order(np.where(first, -sin, sin)), F32))


def _rope_weights(w_qkv, q_gain, k_gain):
    qk = (N_HEADS + N_KV_HEADS) * HEAD_DIM
    lead = w_qkv.shape[:-1]
    w_qk = _rope_lane_order(w_qkv[..., :qk].reshape(*lead, N_HEADS + N_KV_HEADS, HEAD_DIM))
    w = jnp.concatenate([w_qk.reshape(*lead, qk), w_qkv[..., qk:]], axis=-1)
    return w, _rope_lane_order(q_gain), _rope_lane_order(k_gain)


def _run_group(x, mod, row0, row_stride, wts, cache, rope_tables, tile, attn_tile):
    new_kv = []
    for l in range(DEPTH):
        i = l // 2
        g1 = wts["norm_g"][l, 0].reshape(1, D_MODEL)
        g2 = wts["norm_g"][l, 1].reshape(1, D_MODEL)
        if l % 2 == 0:
            x = _even_sublayer(x, mod, l, row0, row_stride, g1, wts["w_in_even"][i],
                               wts["pool_w"][i], wts["pool_scale"][i].reshape(1, POOL_WIDTH),
                               wts["conv_w"][i], wts["conv_b"][i].reshape(1, CONV_WIDTH),
                               wts["w_out_even"][i], tile)
        else:
            sfx = "" if rope_tables is None else "_rope"
            res = _qkv_proj(x, mod, l, row0, row_stride, g1, wts["w_qkv" + sfx][i],
                            wts["q_gain" + sfx][i].reshape(1, HEAD_DIM),
                            wts["k_gain" + sfx][i].reshape(1, HEAD_DIM), rope_tables, cache, i,
                            cache is None, tile)
            q, kh, vt = res[:3]
            if cache is None:
                new_kv.append(res[3:])
            x = _attn_sublayer(q, kh, vt, x, mod, l, row0, row_stride, wts["w_o"][i],
                               wts["q_gain"][i].reshape(1, HEAD_DIM), attn_tile)
        x = _ffn_sublayer(x, mod, l, row0, row_stride, g2, wts["w_ffn_in"], wts["w_ffn_out"],
                          wts["final_g"].reshape(1, D_MODEL), l == DEPTH - 1, tile)
    return x, new_kv


def kernel(x_prompt, x_sample, cache_k, cache_v, c, c_ctx, w_ada, b_ada, norm_g, w_in_even, pool_w,
           pool_scale, conv_w, conv_b, w_out_even, w_qkv, q_gain, k_gain, w_o, w_ffn_in, w_ffn_out,
           final_g):
    n_ctx, ctx_len, _ = x_prompt.shape
    n_lat, lat_len, _ = x_sample.shape
    assert 1 + n_lat <= COND_ROWS

    cond = jnp.concatenate(
        [c_ctx[None, :], c, jnp.zeros((COND_ROWS - 1 - n_lat, D_MODEL), F32)], axis=0)
    mod = _ada(cond, w_ada, b_ada).reshape(DEPTH, COND_ROWS, 6, D_MODEL)

    wts = dict(
        norm_g=norm_g, final_g=final_g, pool_scale=pool_scale, conv_w=conv_w, conv_b=conv_b,
        q_gain=q_gain, k_gain=k_gain,
        w_in_even=w_in_even.astype(BF16), pool_w=pool_w.astype(BF16),
        w_out_even=w_out_even.astype(BF16), w_qkv=w_qkv.astype(BF16), w_o=w_o.astype(BF16),
        w_ffn_in=w_ffn_in.astype(BF16), w_ffn_out=w_ffn_out.astype(BF16),
    )
    w_rope, qg_rope, kg_rope = _rope_weights(wts["w_qkv"], q_gain, k_gain)
    wts.update(w_qkv_rope=w_rope, q_gain_rope=qg_rope, k_gain_rope=kg_rope)

    y_prompt, ctx_kv = _run_group(x_prompt, mod, 0, 0, wts, None, None,
                                  tile=ctx_len, attn_tile=ctx_len)
    new_cache_k = jnp.concatenate([k for k, _ in ctx_kv], axis=1)
    new_cache_v = jnp.concatenate([v for _, v in ctx_kv], axis=1)

    y_sample, _ = _run_group(x_sample, mod, 1, 1, wts, (cache_k, cache_v), _rope_tables(lat_len),
                             tile=512, attn_tile=256)
    return y_prompt, y_sample, new_cache_k, new_cache_v
```

```python
import functools
import math

import jax
import jax.numpy as jnp
import numpy as np
from jax import lax
from jax.experimental import pallas as pl
from jax.experimental.pallas import tpu as pltpu

D_MODEL = 1024
DEPTH = 2
GRID_W = 64
EPS = 1e-6
POOL_WIDTH = D_MODEL // 2
POOL_GROUPS = 4
POOL_GROUP_DIM = POOL_WIDTH // POOL_GROUPS
POOL_WINDOWS = (2, 4, 8, 16)
CONV_WIDTH = D_MODEL // 2
EVEN_IN = POOL_WIDTH + 3 * CONV_WIDTH
HEAD_DIM = 128
N_HEADS = D_MODEL // HEAD_DIM
N_KV_HEADS = N_HEADS // 4
KV_WIDTH = N_KV_HEADS * HEAD_DIM
QKV_WIDTH = (N_HEADS + 2 * N_KV_HEADS) * HEAD_DIM
ROPE_THETA = 10000.0
AXIS_DIM = HEAD_DIM // 2
D_FF = ((8 * D_MODEL // 3 + 255) // 256) * 256
SM_SCALE = HEAD_DIM ** -0.5
LOG2E = math.log2(math.e)

COND_ROWS = 8
HALO = 16
ADA_BLOCK_N = 1536
FF_CHUNKS = ((0, 1024), (1024, 1024), (2048, 768))
FFN_TILE = 1024
FFN_SUB_ROWS = 256
FFN_LOOKAHEAD = 1
ATTN_KEY_CHUNK = 512
ATTN_LOOKAHEAD = 2
BOUND_SLACK = 1.1
MAX_SAFE_SHIFT = 60.0
VMEM_LIMIT_BYTES = 56 * 1024 * 1024

BF16 = jnp.bfloat16
F32 = jnp.float32


def _dot(a, b):
    return jnp.dot(a, b, preferred_element_type=F32)


def _dot_nt(a, b):
    return lax.dot_general(a, b, (((1,), (1,)), ((), ())), preferred_element_type=F32)


def _silu(a):
    return a * (1.0 / (1.0 + jnp.exp(-a)))


def _rms(x):
    return x * lax.rsqrt(jnp.mean(x * x, axis=-1, keepdims=True) + EPS)


def _modulate(x, g, shift, scale):
    return (_rms(x) * g) * (1.0 + scale) + shift


def _params(sem):
    return pltpu.CompilerParams(dimension_semantics=sem, vmem_limit_bytes=VMEM_LIMIT_BYTES)


def _const_spec(shape):
    n = len(shape)
    return pl.BlockSpec(shape, lambda *_: (0,) * n, pipeline_mode=pl.Buffered(1))


def _layer_spec(shape, layer):
    n = len(shape)
    return pl.BlockSpec((None, *shape), lambda *_: (layer,) + (0,) * n,
                        pipeline_mode=pl.Buffered(1))


def _ada_kernel(cond_ref, w_ref, b_ref, o_ref):
    s = _silu(cond_ref[...]).astype(BF16)
    o_ref[...] = _dot(s, w_ref[...].astype(BF16)) + b_ref[...]


def _ada(cond, w_ada, b_ada):
    n_out = w_ada.shape[-1]
    return pl.pallas_call(
        _ada_kernel,
        out_shape=jax.ShapeDtypeStruct((DEPTH, COND_ROWS, n_out), F32),
        grid=(DEPTH, n_out // ADA_BLOCK_N),
        in_specs=[
            pl.BlockSpec((COND_ROWS, D_MODEL), lambda l, j: (0, 0)),
            pl.BlockSpec((None, D_MODEL, ADA_BLOCK_N), lambda l, j: (l, 0, j)),
            pl.BlockSpec((None, 1, ADA_BLOCK_N), lambda l, j: (l, 0, j)),
        ],
        out_specs=pl.BlockSpec((None, COND_ROWS, ADA_BLOCK_N), lambda l, j: (l, 0, j)),
        compiler_params=_params(("arbitrary", "arbitrary")),
        name="ada",
    )(cond, w_ada, b_ada.reshape(DEPTH, 1, n_out))


def _mod_spec(layer, row0, row_stride):
    return pl.BlockSpec((None, None, 6, D_MODEL),
                        lambda s, t: (layer, row0 + row_stride * s, 0, 0))


def _even_kernel(x_ref, xp_ref, xn_ref, mod_ref, g_ref, w_in_ref, pool_w_ref, pool_scale_ref,
                 conv_w_ref, conv_b_ref, w_out_ref, o_ref, *, tile, seq_len):
    t = pl.program_id(1)
    n_t = pl.num_programs(1)
    ext = tile + 2 * HALO
    mod = mod_ref[...]
    shift, scale, gate = mod[0:1], mod[1:2], mod[2:3]
    g = g_ref[...]
    x = x_ref[...]
    h = jnp.concatenate(
        [_modulate(xp_ref[...], g, shift, scale).astype(BF16),
         _modulate(x, g, shift, scale).astype(BF16),
         _modulate(xn_ref[...], g, shift, scale).astype(BF16)], axis=0)
    p = _dot(h, w_in_ref[...])

    row = lax.broadcasted_iota(jnp.int32, (ext, 1), 0)
    first_valid = jnp.where(t > 0, 0, HALO)
    end_valid = jnp.where(t < n_t - 1, ext, HALO + tile)
    valid = jnp.logical_and(row >= first_valid, row < end_valid)
    pos = t * tile + row[HALO:HALO + tile] - HALO

    def shifted(a, s):
        return pltpu.roll(a, s % ext, 0)

    u = jnp.where(valid, p[:, :POOL_WIDTH], 0.0)
    ya = []
    for i, w in enumerate(POOL_WINDOWS):
        ug = u[:, i * POOL_GROUP_DIM:(i + 1) * POOL_GROUP_DIM]
        win = ug + shifted(ug, 1)
        half = 1
        while 2 * half < w:
            win = shifted(win, half) + shifted(win, -half)
            half *= 2
        cnt = (jnp.clip(pos + w // 2, 0, seq_len) - jnp.clip(pos - w // 2, 0, seq_len)).astype(F32)
        pooled = win[HALO:HALO + tile] / cnt - ug[HALO:HALO + tile]
        ya.append(_dot(pooled.astype(BF16), pool_w_ref[i]))
    ya = jnp.concatenate(ya, axis=-1) * pool_scale_ref[...]

    c0 = POOL_WIDTH
    bg = p[HALO:HALO + tile, c0:c0 + CONV_WIDTH]
    z = jnp.where(valid, p[:, c0 + CONV_WIDTH:c0 + 2 * CONV_WIDTH] * p[:, c0 + 2 * CONV_WIDTH:], 0.0)
    cw = conv_w_ref[...]
    conv = (shifted(z, 1) * cw[0:1] + z * cw[1:2] + shifted(z, -1) * cw[2:3])[HALO:HALO + tile]
    yb = bg * (conv + conv_b_ref[...])

    y = _dot(jnp.concatenate([ya, yb], axis=-1).astype(BF16), w_out_ref[...])
    o_ref[...] = x + gate * y


def _even_sublayer(x, mod, layer, row0, row_stride, g, w_in, pool_w, pool_scale, conv_w, conv_b,
                   w_out, tile):
    n_seq, seq_len, _ = x.shape
    n_t = seq_len // tile
    hb = tile // HALO
    n_hb = seq_len // HALO
    kern = functools.partial(_even_kernel, tile=tile, seq_len=seq_len)
    return pl.pallas_call(
        kern,
        out_shape=jax.ShapeDtypeStruct(x.shape, F32),
        grid=(n_seq, n_t),
        in_specs=[
            pl.BlockSpec((None, tile, D_MODEL), lambda s, t: (s, t, 0)),
            pl.BlockSpec((None, HALO, D_MODEL), lambda s, t: (s, jnp.maximum(t * hb - 1, 0), 0)),
            pl.BlockSpec((None, HALO, D_MODEL),
                         lambda s, t: (s, jnp.minimum((t + 1) * hb, n_hb - 1), 0)),
            _mod_spec(layer, row0, row_stride),
            _const_spec((1, D_MODEL)),
            _const_spec((D_MODEL, EVEN_IN)),
            _const_spec((POOL_GROUPS, POOL_GROUP_DIM, POOL_GROUP_DIM)),
            _const_spec((1, POOL_WIDTH)),
            _const_spec((3, CONV_WIDTH)),
            _const_spec((1, CONV_WIDTH)),
            _const_spec((POOL_WIDTH + CONV_WIDTH, D_MODEL)),
        ],
        out_specs=pl.BlockSpec((None, tile, D_MODEL), lambda s, t: (s, t, 0)),
        compiler_params=_params(("arbitrary", "arbitrary")),
        name="even_mixer",
    )(x, x, x, mod, g, w_in, pool_w, pool_scale, conv_w, conv_b, w_out)


def _ffn_kernel(x_ref, mod_ref, g_ref, w_in_ref, w_out_ref, fg_ref, o_ref, *, final_norm):
    mod = mod_ref[...]
    shift, scale, gate = mod[3:4], mod[4:5], mod[5:6]
    g = g_ref[...]
    n_sub = x_ref.shape[0] // FFN_SUB_ROWS
    items = [(sub, c) for sub in range(n_sub) for c in range(len(FF_CHUNKS))]
    hs, ups = {}, {}

    def issue_up(j):
        sub, c = items[j]
        if c == 0:
            rows = x_ref[sub * FFN_SUB_ROWS:(sub + 1) * FFN_SUB_ROWS, :]
            hs[sub] = _modulate(rows, g, shift, scale).astype(BF16)
        c0, cw = FF_CHUNKS[c]
        ups[j] = (_dot(hs[sub], w_in_ref[:, c0:c0 + cw]),
                  _dot(hs[sub], w_in_ref[:, D_FF + c0:D_FF + c0 + cw]))

    for j in range(min(FFN_LOOKAHEAD, len(items))):
        issue_up(j)
    for i, (sub, c) in enumerate(items):
        if i + FFN_LOOKAHEAD < len(items):
            issue_up(i + FFN_LOOKAHEAD)
        a, b = ups.pop(i)
        c0, cw = FF_CHUNKS[c]
        part = _dot((_silu(a) * b).astype(BF16), w_out_ref[c0:c0 + cw, :])
        acc = part if c == 0 else acc + part
        if c == len(FF_CHUNKS) - 1:
            rows = slice(sub * FFN_SUB_ROWS, (sub + 1) * FFN_SUB_ROWS)
            out = x_ref[rows, :] + gate * acc
            if final_norm:
                out = _rms(out) * fg_ref[...]
            o_ref[rows, :] = out


def _ffn_sublayer(x, mod, layer, row0, row_stride, g, w_in, w_out, final_g, final_norm, tile):
    n_seq, seq_len, _ = x.shape
    kern = functools.partial(_ffn_kernel, final_norm=final_norm)
    return pl.pallas_call(
        kern,
        out_shape=jax.ShapeDtypeStruct(x.shape, F32),
        grid=(n_seq, seq_len // tile),
        in_specs=[
            pl.BlockSpec((None, tile, D_MODEL), lambda s, t: (s, t, 0)),
            _mod_spec(layer, row0, row_stride),
            _const_spec((1, D_MODEL)),
            _layer_spec((D_MODEL, 2 * D_FF), layer),
            _layer_spec((D_FF, D_MODEL), layer),
            _const_spec((1, D_MODEL)),
        ],
        out_specs=pl.BlockSpec((None, tile, D_MODEL), lambda s, t: (s, t, 0)),
        compiler_params=_params(("arbitrary", "arbitrary")),
        name="ffn",
    )(x, mod, g, w_in, w_out, final_g)


def _rope_lane_order(a):
    lead = a.shape[:-1]
    return a.reshape(*lead, 2, 2, AXIS_DIM // 2).swapaxes(-3, -2).reshape(*lead, HEAD_DIM)


def _rope(a, cos, sin_signed):
    return a * cos + pltpu.roll(a, HEAD_DIM // 2, 1) * sin_signed


def _pair_mean_matrix():
    r = lax.broadcasted_iota(jnp.int32, (2 * HEAD_DIM, 2 * HEAD_DIM), 0) // HEAD_DIM
    c = lax.broadcasted_iota(jnp.int32, (2 * HEAD_DIM, 2 * HEAD_DIM), 1) // HEAD_DIM
    return jnp.where(r == c, 1.0 / HEAD_DIM, 0.0).astype(BF16)


def _rope_lane_matrix():
    src = lax.broadcasted_iota(jnp.int32, (HEAD_DIM, HEAD_DIM), 0)
    n = lax.broadcasted_iota(jnp.int32, (HEAD_DIM, HEAD_DIM), 1)
    quarter = AXIS_DIM // 2
    half, axis, c = n // AXIS_DIM, (n // quarter) % 2, n % quarter
    return jnp.where(src == axis * AXIS_DIM + half * quarter + c, 1.0, 0.0).astype(BF16)


def _qkv_kernel(*refs, rope, cached, keep_f32):
    it = iter(refs)
    x_ref, mod_ref, g_ref, w_ref, qg_ref, kg_ref = (next(it) for _ in range(6))
    rope_refs = [next(it) for _ in range(2)] if rope else None
    cache_refs = [next(it) for _ in range(2)] if cached else None
    q_ref, kh_ref, vt_ref = (next(it) for _ in range(3))
    f32_refs = [next(it) for _ in range(2)] if keep_f32 else None

    def project():
        mod = mod_ref[...]
        shift, scale = mod[0:1], mod[1:2]
        h = _modulate(x_ref[...], g_ref[...], shift, scale).astype(BF16)
        p = _dot(h, w_ref[...])
        if rope:
            tables = [r[...] for r in rope_refs]
        gains = [qg_ref[...] * (SM_SCALE * LOG2E)] * N_HEADS + [kg_ref[...]] * N_KV_HEADS
        pair_mean = _pair_mean_matrix()
        v0 = (N_HEADS + N_KV_HEADS) * HEAD_DIM
        for pair in range((N_HEADS + N_KV_HEADS) // 2):
            blk = p[:, pair * 2 * HEAD_DIM:(pair + 1) * 2 * HEAD_DIM]
            ms = _dot((blk * blk).astype(BF16), pair_mean)
            normed = blk * lax.rsqrt(ms + EPS)
            for sub in range(2):
                hd = 2 * pair + sub
                a = normed[:, sub * HEAD_DIM:(sub + 1) * HEAD_DIM] * gains[hd]
                if rope:
                    a = _rope(a, *tables)
                if hd < N_HEADS:
                    q_ref[hd] = a.astype(BF16)
                else:
                    kv = hd - N_HEADS
                    v = p[:, v0 + kv * HEAD_DIM:v0 + (kv + 1) * HEAD_DIM]
                    kh_ref[kv] = a.astype(BF16)
                    vt_ref[kv] = v.T.astype(BF16)
                    if keep_f32:
                        f32_refs[0][:, kv, :] = a
                        f32_refs[1][:, kv, :] = v

    if not cached:
        project()
        return

    t = pl.program_id(1)

    @pl.when(t == 0)
    def _():
        for kv in range(N_KV_HEADS):
            k = cache_refs[0][:, kv, :].astype(BF16)
            if rope:
                k = _dot(k, _rope_lane_matrix()).astype(BF16)
            kh_ref[kv] = k
            vt_ref[kv] = cache_refs[1][:, kv, :].T.astype(BF16)

    pl.when(t > 0)(project)


def _qkv_proj(x, mod, layer, row0, row_stride, g, w_qkv, q_gain, k_gain, rope_tables, cache,
              cache_layer, keep_f32, tile):
    n_seq, seq_len, _ = x.shape
    rope = rope_tables is not None
    cached = cache is not None
    lead = 0
    if cached:
        assert cache[0].shape[2] == tile
        lead = 1
    n_keys = seq_len + lead * tile
    tok = lambda t: jnp.maximum(t - lead, 0)
    in_specs = [
        pl.BlockSpec((None, tile, D_MODEL), lambda s, t: (s, tok(t), 0)),
        _mod_spec(layer, row0, row_stride),
        _const_spec((1, D_MODEL)),
        _const_spec((D_MODEL, QKV_WIDTH)),
        _const_spec((1, HEAD_DIM)),
        _const_spec((1, HEAD_DIM)),
    ]
    args = [x, mod, g, w_qkv, q_gain, k_gain]
    if rope:
        in_specs += [pl.BlockSpec((tile, HEAD_DIM), lambda s, t: (tok(t), 0))] * 2
        args += list(rope_tables)
    if cached:
        in_specs += [pl.BlockSpec((None, None, tile, N_KV_HEADS, HEAD_DIM),
                                  lambda s, t: (s, cache_layer, 0, 0, 0))] * 2
        args += list(cache)
    out_shape = [jax.ShapeDtypeStruct((n_seq, N_HEADS, seq_len, HEAD_DIM), BF16),
                 jax.ShapeDtypeStruct((n_seq, N_KV_HEADS, n_keys, HEAD_DIM), BF16),
                 jax.ShapeDtypeStruct((n_seq, N_KV_HEADS, HEAD_DIM, n_keys), BF16)]
    out_specs = [pl.BlockSpec((None, N_HEADS, tile, HEAD_DIM), lambda s, t: (s, 0, tok(t), 0)),
                 pl.BlockSpec((None, N_KV_HEADS, tile, HEAD_DIM), lambda s, t: (s, 0, t, 0)),
                 pl.BlockSpec((None, N_KV_HEADS, HEAD_DIM, tile), lambda s, t: (s, 0, 0, t))]
    if keep_f32:
        assert not cached
        out_shape += [jax.ShapeDtypeStruct((n_seq, 1, seq_len, N_KV_HEADS, HEAD_DIM), F32)] * 2
        out_specs += [pl.BlockSpec((None, None, tile, N_KV_HEADS, HEAD_DIM),
                                   lambda s, t: (s, 0, t, 0, 0))] * 2
    return pl.pallas_call(
        functools.partial(_qkv_kernel, rope=rope, cached=cached, keep_f32=keep_f32),
        out_shape=tuple(out_shape),
        grid=(n_seq, seq_len // tile + lead),
        in_specs=in_specs,
        out_specs=tuple(out_specs),
        compiler_params=_params(("arbitrary", "arbitrary")),
        name="qkv_proj",
    )(*args)


def _attn_kernel(q_ref, kh_ref, vt_ref, x_ref, mod_ref, w_o_ref, qg_ref, o_ref, heads_scr, shift_scr,
                 safe_scr, *, key_chunk):
    group = N_HEADS // N_KV_HEADS
    n_chunks = kh_ref.shape[1] // key_chunk

    @pl.when(pl.program_id(1) == 0)
    def _():
        qg = qg_ref[...]
        q_sq = HEAD_DIM * (SM_SCALE * LOG2E) ** 2 * jnp.max(qg * qg)
        for kv in range(N_KV_HEADS):
            k = kh_ref[kv].astype(F32)
            k_sq = jnp.max(jnp.sum(k * k, axis=1, keepdims=True))
            bound_sq = q_sq * k_sq * BOUND_SLACK
            shift_scr[kv] = jnp.full(shift_scr.shape[1:], jnp.sqrt(bound_sq), F32)
            safe_scr[kv] = (bound_sq <= MAX_SAFE_SHIFT ** 2).astype(jnp.int32)

    def bounded():
        items = [(hd, c) for hd in range(N_HEADS) for c in range(n_chunks)]
        scores = {}

        def issue_scores(j):
            hd, c = items[j]
            keys = kh_ref[hd // group, c * key_chunk:(c + 1) * key_chunk, :]
            scores[j] = _dot_nt(keys, q_ref[hd])

        for j in range(min(ATTN_LOOKAHEAD, len(items))):
            issue_scores(j)
        for i, (hd, c) in enumerate(items):
            if i + ATTN_LOOKAHEAD < len(items):
                issue_scores(i + ATTN_LOOKAHEAD)
            p = jnp.exp2(scores.pop(i) - shift_scr[hd // group][0:1, 0:1])
            ps = jnp.sum(p, axis=0, keepdims=True)
            pv = _dot(vt_ref[hd // group, :, c * key_chunk:(c + 1) * key_chunk], p.astype(BF16))
            denom, acc = (ps, pv) if c == 0 else (denom + ps, acc + pv)
            if c == n_chunks - 1:
                heads_scr[hd] = (acc * (1.0 / denom)).T.astype(BF16)

    def exact():
        def one_head(hd, carry):
            kv = hd // group
            s = _dot_nt(kh_ref[kv], q_ref[hd])
            p = jnp.exp2(s - jnp.max(s, axis=0, keepdims=True))
            denom = jnp.sum(p, axis=0, keepdims=True)
            o_t = _dot(vt_ref[kv], p.astype(BF16)) * (1.0 / denom)
            heads_scr[hd] = o_t.T.astype(BF16)
            return carry

        lax.fori_loop(0, N_HEADS, one_head, 0)

    safe = safe_scr[0]
    for kv in range(1, N_KV_HEADS):
        safe = jnp.minimum(safe, safe_scr[kv])
    pl.when(safe == 1)(bounded)
    pl.when(safe != 1)(exact)

    attn = jnp.concatenate([heads_scr[hd] for hd in range(N_HEADS)], axis=-1)
    o_ref[...] = x_ref[...] + mod_ref[2:3] * _dot(attn, w_o_ref[...])


def _attn_sublayer(q, kh, vt, x, mod, layer, row0, row_stride, w_o, q_gain, tile):
    n_seq, seq_len, _ = x.shape
    n_keys = kh.shape[2]
    return pl.pallas_call(
        functools.partial(_attn_kernel, key_chunk=min(n_keys, ATTN_KEY_CHUNK)),
        out_shape=jax.ShapeDtypeStruct(x.shape, F32),
        grid=(n_seq, seq_len // tile),
        in_specs=[
            pl.BlockSpec((None, N_HEADS, tile, HEAD_DIM), lambda s, t: (s, 0, t, 0)),
            pl.BlockSpec((None, N_KV_HEADS, n_keys, HEAD_DIM), lambda s, t: (s, 0, 0, 0)),
            pl.BlockSpec((None, N_KV_HEADS, HEAD_DIM, n_keys), lambda s, t: (s, 0, 0, 0)),
            pl.BlockSpec((None, tile, D_MODEL), lambda s, t: (s, t, 0)),
            _mod_spec(layer, row0, row_stride),
            _const_spec((D_MODEL, D_MODEL)),
            _const_spec((1, HEAD_DIM)),
        ],
        out_specs=pl.BlockSpec((None, tile, D_MODEL), lambda s, t: (s, t, 0)),
        scratch_shapes=[pltpu.VMEM((N_HEADS, tile, HEAD_DIM), BF16),
                        pltpu.VMEM((N_KV_HEADS, 8, HEAD_DIM), F32),
                        pltpu.SMEM((N_KV_HEADS,), jnp.int32)],
        compiler_params=_params(("arbitrary", "arbitrary")),
        name="attention",
    )(q, kh, vt, x, mod, w_o, q_gain)


def _rope_tables(seq_len):
    rows = seq_len // GRID_W
    r = np.repeat(np.arange(rows), GRID_W).astype(np.float32)
    col = np.tile(np.arange(GRID_W), rows).astype(np.float32)
    expo = -np.arange(0, AXIS_DIM, 2, dtype=np.float32) / np.float32(AXIS_DIM)
    inv = np.power(np.float32(ROPE_THETA), expo).astype(np.float32)
    ar = r[:, None] * inv
    ac = col[:, None] * inv
    ang = np.concatenate([ar, ar, ac, ac], axis=-1)
    cos, sin = np.cos(ang), np.sin(ang)
    first = (np.arange(HEAD_DIM) % AXIS_DIM) < AXIS_DIM // 2
    return (jnp.asarray(_rope_lane_order(cos), F32),
            jnp.asarray(_rope_lane_order(np.where(first, -sin, sin)), F32))


def _rope_weights(w_qkv, q_gain, k_gain):
    qk = (N_HEADS + N_KV_HEADS) * HEAD_DIM
    lead = w_qkv.shape[:-1]
    w_qk = _rope_lane_order(w_qkv[..., :qk].reshape(*lead, N_HEADS + N_KV_HEADS, HEAD_DIM))
    w = jnp.concatenate([w_qk.reshape(*lead, qk), w_qkv[..., qk:]], axis=-1)
    return w, _rope_lane_order(q_gain), _rope_lane_order(k_gain)


def _run_group(x, mod, row0, row_stride, wts, cache, rope_tables, tile, attn_tile):
    new_kv = []
    for l in range(DEPTH):
        i = l // 2
        g1 = wts["norm_g"][l, 0].reshape(1, D_MODEL)
        g2 = wts["norm_g"][l, 1].reshape(1, D_MODEL)
        if l % 2 == 0:
            x = _even_sublayer(x, mod, l, row0, row_stride, g1, wts["w_in_even"][i],
                               wts["pool_w"][i], wts["pool_scale"][i].reshape(1, POOL_WIDTH),
                               wts["conv_w"][i], wts["conv_b"][i].reshape(1, CONV_WIDTH),
                               wts["w_out_even"][i], tile)
        else:
            sfx = "" if rope_tables is None else "_rope"
            res = _qkv_proj(x, mod, l, row0, row_stride, g1, wts["w_qkv" + sfx][i],
                            wts["q_gain" + sfx][i].reshape(1, HEAD_DIM),
                            wts["k_gain" + sfx][i].reshape(1, HEAD_DIM), rope_tables, cache, i,
                            cache is None, tile)
            q, kh, vt = res[:3]
            if cache is None:
                new_kv.append(res[3:])
            x = _attn_sublayer(q, kh, vt, x, mod, l, row0, row_stride, wts["w_o"][i],
                               wts["q_gain"][i].reshape(1, HEAD_DIM), attn_tile)
        flat = x.reshape(1, -1, D_MODEL) if row_stride == 0 else x
        flat = _ffn_sublayer(flat, mod, l, row0, row_stride, g2, wts["w_ffn_in"], wts["w_ffn_out"],
                             wts["final_g"].reshape(1, D_MODEL), l == DEPTH - 1, FFN_TILE)
        x = flat.reshape(x.shape)
    return x, new_kv


def kernel(x_prompt, x_sample, cache_k, cache_v, c, c_ctx, w_ada, b_ada, norm_g, w_in_even, pool_w,
           pool_scale, conv_w, conv_b, w_out_even, w_qkv, q_gain, k_gain, w_o, w_ffn_in, w_ffn_out,
           final_g):
    n_ctx, ctx_len, _ = x_prompt.shape
    n_lat, lat_len, _ = x_sample.shape
    assert 1 + n_lat <= COND_ROWS

    cond = jnp.concatenate(
        [c_ctx[None, :], c, jnp.zeros((COND_ROWS - 1 - n_lat, D_MODEL), F32)], axis=0)
    mod = _ada(cond, w_ada, b_ada).reshape(DEPTH, COND_ROWS, 6, D_MODEL)

    wts = dict(
        norm_g=norm_g, final_g=final_g, pool_scale=pool_scale, conv_w=conv_w, conv_b=conv_b,
        q_gain=q_gain, k_gain=k_gain,
        w_in_even=w_in_even.astype(BF16), pool_w=pool_w.astype(BF16),
        w_out_even=w_out_even.astype(BF16), w_qkv=w_qkv.astype(BF16), w_o=w_o.astype(BF16),
        w_ffn_in=w_ffn_in.astype(BF16), w_ffn_out=w_ffn_out.astype(BF16),
    )
    w_rope, qg_rope, kg_rope = _rope_weights(wts["w_qkv"], q_gain, k_gain)
    wts.update(w_qkv_rope=w_rope, q_gain_rope=qg_rope, k_gain_rope=kg_rope)

    y_prompt, ctx_kv = _run_group(x_prompt, mod, 0, 0, wts, None, None,
                                  tile=ctx_len, attn_tile=ctx_len)
    new_cache_k = jnp.concatenate([k for k, _ in ctx_kv], axis=1)
    new_cache_v = jnp.concatenate([v for _, v in ctx_kv], axis=1)

    y_sample, _ = _run_group(x_sample, mod, 1, 1, wts, (cache_k, cache_v), _rope_tables(lat_len),
                             tile=512, attn_tile=256)
    return y_prompt, y_sample, new_cache_k, new_cache_v
```

```python
import functools
import math

import jax
import jax.numpy as jnp
import numpy as np
from jax import lax
from jax.experimental import pallas as pl
from jax.experimental.pallas import tpu as pltpu

D_MODEL = 1024
DEPTH = 2
GRID_W = 64
EPS = 1e-6
POOL_WIDTH = D_MODEL // 2
POOL_GROUPS = 4
POOL_GROUP_DIM = POOL_WIDTH // POOL_GROUPS
POOL_WINDOWS = (2, 4, 8, 16)
CONV_WIDTH = D_MODEL // 2
EVEN_IN = POOL_WIDTH + 3 * CONV_WIDTH
HEAD_DIM = 128
N_HEADS = D_MODEL // HEAD_DIM
N_KV_HEADS = N_HEADS // 4
KV_WIDTH = N_KV_HEADS * HEAD_DIM
QKV_WIDTH = (N_HEADS + 2 * N_KV_HEADS) * HEAD_DIM
ROPE_THETA = 10000.0
AXIS_DIM = HEAD_DIM // 2
D_FF = ((8 * D_MODEL // 3 + 255) // 256) * 256
SM_SCALE = HEAD_DIM ** -0.5
LOG2E = math.log2(math.e)

COND_ROWS = 8
HALO = 16
ADA_BLOCK_N = 1536
FF_CHUNKS = ((0, 1024), (1024, 1024), (2048, 768))
EVEN_TILE = 1024
EVEN_SUB_ROWS = 512
FFN_TILE = 1024
FFN_SUB_ROWS = 256
FFN_LOOKAHEAD = 1
ATTN_KEY_CHUNK = 512
ATTN_LOOKAHEAD = 2
BOUND_SLACK = 1.1
MAX_SAFE_SHIFT = 60.0
VMEM_LIMIT_BYTES = 56 * 1024 * 1024

BF16 = jnp.bfloat16
F32 = jnp.float32


def _dot(a, b):
    return jnp.dot(a, b, preferred_element_type=F32)


def _dot_nt(a, b):
    return lax.dot_general(a, b, (((1,), (1,)), ((), ())), preferred_element_type=F32)


def _silu(a):
    return a * (1.0 / (1.0 + jnp.exp(-a)))


def _rms(x):
    return x * lax.rsqrt(jnp.mean(x * x, axis=-1, keepdims=True) + EPS)


def _modulate(x, g, shift, scale):
    return (_rms(x) * g) * (1.0 + scale) + shift


def _params(sem):
    return pltpu.CompilerParams(dimension_semantics=sem, vmem_limit_bytes=VMEM_LIMIT_BYTES)


def _const_spec(shape):
    n = len(shape)
    return pl.BlockSpec(shape, lambda *_: (0,) * n, pipeline_mode=pl.Buffered(1))


def _layer_spec(shape, layer):
    n = len(shape)
    return pl.BlockSpec((None, *shape), lambda *_: (layer,) + (0,) * n,
                        pipeline_mode=pl.Buffered(1))


def _ada_kernel(cond_ref, w_ref, b_ref, o_ref):
    s = _silu(cond_ref[...]).astype(BF16)
    o_ref[...] = _dot(s, w_ref[...].astype(BF16)) + b_ref[...]


def _ada(cond, w_ada, b_ada):
    n_out = w_ada.shape[-1]
    return pl.pallas_call(
        _ada_kernel,
        out_shape=jax.ShapeDtypeStruct((DEPTH, COND_ROWS, n_out), F32),
        grid=(DEPTH, n_out // ADA_BLOCK_N),
        in_specs=[
            pl.BlockSpec((COND_ROWS, D_MODEL), lambda l, j: (0, 0)),
            pl.BlockSpec((None, D_MODEL, ADA_BLOCK_N), lambda l, j: (l, 0, j)),
            pl.BlockSpec((None, 1, ADA_BLOCK_N), lambda l, j: (l, 0, j)),
        ],
        out_specs=pl.BlockSpec((None, COND_ROWS, ADA_BLOCK_N), lambda l, j: (l, 0, j)),
        compiler_params=_params(("arbitrary", "arbitrary")),
        name="ada",
    )(cond, w_ada, b_ada.reshape(DEPTH, 1, n_out))


def _mod_spec(layer, row0, row_stride):
    return pl.BlockSpec((None, None, 6, D_MODEL),
                        lambda s, t: (layer, row0 + row_stride * s, 0, 0))


def _even_kernel(x_ref, xp_ref, xn_ref, mod_ref, g_ref, w_in_ref, pool_w_ref, pool_scale_ref,
                 conv_w_ref, conv_b_ref, w_out_ref, o_ref, *, tile, seq_len):
    t = pl.program_id(1)
    n_t = pl.num_programs(1)
    sub = min(EVEN_SUB_ROWS, tile)
    n_sub = tile // sub
    win = sub + 2 * HALO
    mod = mod_ref[...]
    shift, scale, gate = mod[0:1], mod[1:2], mod[2:3]
    g = g_ref[...]
    cw = conv_w_ref[...]

    def h_rows(lo, hi):
        pieces = []
        if lo < HALO:
            pieces.append(xp_ref[...])
        pieces.append(x_ref[max(lo, HALO) - HALO:min(hi, HALO + tile) - HALO, :])
        if hi > HALO + tile:
            pieces.append(xn_ref[...])
        return jnp.concatenate([_modulate(r, g, shift, scale).astype(BF16) for r in pieces], axis=0)

    def project(s):
        lo = 0 if s == 0 else s * sub + 2 * HALO
        return _dot(h_rows(lo, (s + 1) * sub + 2 * HALO), w_in_ref[...])

    def edge_masked(a, s):
        parts = [a[:HALO], a[HALO:HALO + sub], a[HALO + sub:]]
        if s == 0:
            parts[0] = jnp.where(t > 0, parts[0], 0.0)
        if s == n_sub - 1:
            parts[2] = jnp.where(t < n_t - 1, parts[2], 0.0)
        return jnp.concatenate(parts, axis=0)

    def shifted(a, k):
        return pltpu.roll(a, k % win, 0)

    def mixers(s, p):
        pos = t * tile + s * sub + lax.broadcasted_iota(jnp.int32, (sub, 1), 0)
        u = edge_masked(p[:, :POOL_WIDTH], s)
        ya = []
        for i, w in enumerate(POOL_WINDOWS):
            ug = u[:, i * POOL_GROUP_DIM:(i + 1) * POOL_GROUP_DIM]
            fwd, n = ug, 1
            while n < w // 2:
                fwd = fwd + shifted(fwd, -n)
                n *= 2
            wsum = shifted(fwd, w // 2) + fwd
            cnt = (jnp.clip(pos + w // 2, 0, seq_len)
                   - jnp.clip(pos - w // 2, 0, seq_len)).astype(F32)
            pooled = wsum[HALO:HALO + sub] / cnt - ug[HALO:HALO + sub]
            ya.append(_dot(pooled.astype(BF16), pool_w_ref[i]))
        ya = jnp.concatenate(ya, axis=-1) * pool_scale_ref[...]
        c0 = POOL_WIDTH
        bg = p[HALO:HALO + sub, c0:c0 + CONV_WIDTH]
        z = edge_masked(p[:, c0 + CONV_WIDTH:c0 + 2 * CONV_WIDTH] * p[:, c0 + 2 * CONV_WIDTH:], s)
        conv = (shifted(z, 1) * cw[0:1] + z * cw[1:2] + shifted(z, -1) * cw[2:3])[HALO:HALO + sub]
        yb = bg * (conv + conv_b_ref[...])
        y = _dot(jnp.concatenate([ya, yb], axis=-1).astype(BF16), w_out_ref[...])
        rows = slice(s * sub, (s + 1) * sub)
        o_ref[rows, :] = x_ref[rows, :] + gate * y

    block = project(0)
    window = None
    for s in range(n_sub):
        window = block if s == 0 else jnp.concatenate([window[sub:], block], axis=0)
        if s + 1 < n_sub:
            block = project(s + 1)
        mixers(s, window)


def _even_sublayer(x, mod, layer, row0, row_stride, g, w_in, pool_w, pool_scale, conv_w, conv_b,
                   w_out, tile):
    n_seq, seq_len, _ = x.shape
    n_t = seq_len // tile
    hb = tile // HALO
    n_hb = seq_len // HALO
    kern = functools.partial(_even_kernel, tile=tile, seq_len=seq_len)
    return pl.pallas_call(
        kern,
        out_shape=jax.ShapeDtypeStruct(x.shape, F32),
        grid=(n_seq, n_t),
        in_specs=[
            pl.BlockSpec((None, tile, D_MODEL), lambda s, t: (s, t, 0)),
            pl.BlockSpec((None, HALO, D_MODEL), lambda s, t: (s, jnp.maximum(t * hb - 1, 0), 0)),
            pl.BlockSpec((None, HALO, D_MODEL),
                         lambda s, t: (s, jnp.minimum((t + 1) * hb, n_hb - 1), 0)),
            _mod_spec(layer, row0, row_stride),
            _const_spec((1, D_MODEL)),
            _const_spec((D_MODEL, EVEN_IN)),
            _const_spec((POOL_GROUPS, POOL_GROUP_DIM, POOL_GROUP_DIM)),
            _const_spec((1, POOL_WIDTH)),
            _const_spec((3, CONV_WIDTH)),
            _const_spec((1, CONV_WIDTH)),
            _const_spec((POOL_WIDTH + CONV_WIDTH, D_MODEL)),
        ],
        out_specs=pl.BlockSpec((None, tile, D_MODEL), lambda s, t: (s, t, 0)),
        compiler_params=_params(("arbitrary", "arbitrary")),
        name="even_mixer",
    )(x, x, x, mod, g, w_in, pool_w, pool_scale, conv_w, conv_b, w_out)


def _ffn_kernel(x_ref, mod_ref, g_ref, w_in_ref, w_out_ref, fg_ref, o_ref, *, final_norm):
    mod = mod_ref[...]
    shift, scale, gate = mod[3:4], mod[4:5], mod[5:6]
    g = g_ref[...]
    n_sub = x_ref.shape[0] // FFN_SUB_ROWS
    items = [(sub, c) for sub in range(n_sub) for c in range(len(FF_CHUNKS))]
    hs, ups = {}, {}

    def issue_up(j):
        sub, c = items[j]
        if c == 0:
            rows = x_ref[sub * FFN_SUB_ROWS:(sub + 1) * FFN_SUB_ROWS, :]
            hs[sub] = _modulate(rows, g, shift, scale).astype(BF16)
        c0, cw = FF_CHUNKS[c]
        ups[j] = (_dot(hs[sub], w_in_ref[:, c0:c0 + cw]),
                  _dot(hs[sub], w_in_ref[:, D_FF + c0:D_FF + c0 + cw]))

    for j in range(min(FFN_LOOKAHEAD, len(items))):
        issue_up(j)
    for i, (sub, c) in enumerate(items):
        if i + FFN_LOOKAHEAD < len(items):
            issue_up(i + FFN_LOOKAHEAD)
        a, b = ups.pop(i)
        c0, cw = FF_CHUNKS[c]
        part = _dot((_silu(a) * b).astype(BF16), w_out_ref[c0:c0 + cw, :])
        acc = part if c == 0 else acc + part
        if c == len(FF_CHUNKS) - 1:
            rows = slice(sub * FFN_SUB_ROWS, (sub + 1) * FFN_SUB_ROWS)
            out = x_ref[rows, :] + gate * acc
            if final_norm:
                out = _rms(out) * fg_ref[...]
            o_ref[rows, :] = out


def _ffn_sublayer(x, mod, layer, row0, row_stride, g, w_in, w_out, final_g, final_norm, tile):
    n_seq, seq_len, _ = x.shape
    kern = functools.partial(_ffn_kernel, final_norm=final_norm)
    return pl.pallas_call(
        kern,
        out_shape=jax.ShapeDtypeStruct(x.shape, F32),
        grid=(n_seq, seq_len // tile),
        in_specs=[
            pl.BlockSpec((None, tile, D_MODEL), lambda s, t: (s, t, 0)),
            _mod_spec(layer, row0, row_stride),
            _const_spec((1, D_MODEL)),
            _layer_spec((D_MODEL, 2 * D_FF), layer),
            _layer_spec((D_FF, D_MODEL), layer),
            _const_spec((1, D_MODEL)),
        ],
        out_specs=pl.BlockSpec((None, tile, D_MODEL), lambda s, t: (s, t, 0)),
        compiler_params=_params(("arbitrary", "arbitrary")),
        name="ffn",
    )(x, mod, g, w_in, w_out, final_g)


def _rope_lane_order(a):
    lead = a.shape[:-1]
    return a.reshape(*lead, 2, 2, AXIS_DIM // 2).swapaxes(-3, -2).reshape(*lead, HEAD_DIM)


def _rope(a, cos, sin_signed):
    return a * cos + pltpu.roll(a, HEAD_DIM // 2, 1) * sin_signed


def _pair_mean_matrix():
    r = lax.broadcasted_iota(jnp.int32, (2 * HEAD_DIM, 2 * HEAD_DIM), 0) // HEAD_DIM
    c = lax.broadcasted_iota(jnp.int32, (2 * HEAD_DIM, 2 * HEAD_DIM), 1) // HEAD_DIM
    return jnp.where(r == c, 1.0 / HEAD_DIM, 0.0).astype(BF16)


def _rope_lane_matrix():
    src = lax.broadcasted_iota(jnp.int32, (HEAD_DIM, HEAD_DIM), 0)
    n = lax.broadcasted_iota(jnp.int32, (HEAD_DIM, HEAD_DIM), 1)
    quarter = AXIS_DIM // 2
    half, axis, c = n // AXIS_DIM, (n // quarter) % 2, n % quarter
    return jnp.where(src == axis * AXIS_DIM + half * quarter + c, 1.0, 0.0).astype(BF16)


def _qkv_kernel(*refs, rope, cached, keep_f32):
    it = iter(refs)
    x_ref, mod_ref, g_ref, w_ref, qg_ref, kg_ref = (next(it) for _ in range(6))
    rope_refs = [next(it) for _ in range(2)] if rope else None
    cache_refs = [next(it) for _ in range(2)] if cached else None
    q_ref, kh_ref, vt_ref = (next(it) for _ in range(3))
    f32_refs = [next(it) for _ in range(2)] if keep_f32 else None

    def project():
        mod = mod_ref[...]
        shift, scale = mod[0:1], mod[1:2]
        h = _modulate(x_ref[...], g_ref[...], shift, scale).astype(BF16)
        p = _dot(h, w_ref[...])
        if rope:
            tables = [r[...] for r in rope_refs]
        gains = [qg_ref[...] * (SM_SCALE * LOG2E)] * N_HEADS + [kg_ref[...]] * N_KV_HEADS
        pair_mean = _pair_mean_matrix()
        v0 = (N_HEADS + N_KV_HEADS) * HEAD_DIM
        for pair in range((N_HEADS + N_KV_HEADS) // 2):
            blk = p[:, pair * 2 * HEAD_DIM:(pair + 1) * 2 * HEAD_DIM]
            ms = _dot((blk * blk).astype(BF16), pair_mean)
            normed = blk * lax.rsqrt(ms + EPS)
            for sub in range(2):
                hd = 2 * pair + sub
                a = normed[:, sub * HEAD_DIM:(sub + 1) * HEAD_DIM] * gains[hd]
                if rope:
                    a = _rope(a, *tables)
                if hd < N_HEADS:
                    q_ref[hd] = a.astype(BF16)
                else:
                    kv = hd - N_HEADS
                    v = p[:, v0 + kv * HEAD_DIM:v0 + (kv + 1) * HEAD_DIM]
                    kh_ref[kv] = a.astype(BF16)
                    vt_ref[kv] = v.T.astype(BF16)
                    if keep_f32:
                        f32_refs[0][:, kv, :] = a
                        f32_refs[1][:, kv, :] = v

    if not cached:
        project()
        return

    t = pl.program_id(1)

    @pl.when(t == 0)
    def _():
        for kv in range(N_KV_HEADS):
            k = cache_refs[0][:, kv, :].astype(BF16)
            if rope:
                k = _dot(k, _rope_lane_matrix()).astype(BF16)
            kh_ref[kv] = k
            vt_ref[kv] = cache_refs[1][:, kv, :].T.astype(BF16)

    pl.when(t > 0)(project)


def _qkv_proj(x, mod, layer, row0, row_stride, g, w_qkv, q_gain, k_gain, rope_tables, cache,
              cache_layer, keep_f32, tile):
    n_seq, seq_len, _ = x.shape
    rope = rope_tables is not None
    cached = cache is not None
    lead = 0
    if cached:
        assert cache[0].shape[2] == tile
        lead = 1
    n_keys = seq_len + lead * tile
    tok = lambda t: jnp.maximum(t - lead, 0)
    in_specs = [
        pl.BlockSpec((None, tile, D_MODEL), lambda s, t: (s, tok(t), 0)),
        _mod_spec(layer, row0, row_stride),
        _const_spec((1, D_MODEL)),
        _const_spec((D_MODEL, QKV_WIDTH)),
        _const_spec((1, HEAD_DIM)),
        _const_spec((1, HEAD_DIM)),
    ]
    args = [x, mod, g, w_qkv, q_gain, k_gain]
    if rope:
        in_specs += [pl.BlockSpec((tile, HEAD_DIM), lambda s, t: (tok(t), 0))] * 2
        args += list(rope_tables)
    if cached:
        in_specs += [pl.BlockSpec((None, None, tile, N_KV_HEADS, HEAD_DIM),
                                  lambda s, t: (s, cache_layer, 0, 0, 0))] * 2
        args += list(cache)
    out_shape = [jax.ShapeDtypeStruct((n_seq, N_HEADS, seq_len, HEAD_DIM), BF16),
                 jax.ShapeDtypeStruct((n_seq, N_KV_HEADS, n_keys, HEAD_DIM), BF16),
                 jax.ShapeDtypeStruct((n_seq, N_KV_HEADS, HEAD_DIM, n_keys), BF16)]
    out_specs = [pl.BlockSpec((None, N_HEADS, tile, HEAD_DIM), lambda s, t: (s, 0, tok(t), 0)),
                 pl.BlockSpec((None, N_KV_HEADS, tile, HEAD_DIM), lambda s, t: (s, 0, t, 0)),
                 pl.BlockSpec((None, N_KV_HEADS, HEAD_DIM, tile), lambda s, t: (s, 0, 0, t))]
    if keep_f32:
        assert not cached
        out_shape += [jax.ShapeDtypeStruct((n_seq, 1, seq_len, N_KV_HEADS, HEAD_DIM), F32)] * 2
        out_specs += [pl.BlockSpec((None, None, tile, N_KV_HEADS, HEAD_DIM),
                                   lambda s, t: (s, 0, t, 0, 0))] * 2
    return pl.pallas_call(
        functools.partial(_qkv_kernel, rope=rope, cached=cached, keep_f32=keep_f32),
        out_shape=tuple(out_shape),
        grid=(n_seq, seq_len // tile + lead),
        in_specs=in_specs,
        out_specs=tuple(out_specs),
        compiler_params=_params(("arbitrary", "arbitrary")),
        name="qkv_proj",
    )(*args)


def _attn_kernel(q_ref, kh_ref, vt_ref, x_ref, mod_ref, w_o_ref, qg_ref, o_ref, heads_scr, shift_scr,
                 safe_scr, *, key_chunk):
    group = N_HEADS // N_KV_HEADS
    n_chunks = kh_ref.shape[1] // key_chunk

    @pl.when(pl.program_id(1) == 0)
    def _():
        qg = qg_ref[...]
        q_sq = HEAD_DIM * (SM_SCALE * LOG2E) ** 2 * jnp.max(qg * qg)
        for kv in range(N_KV_HEADS):
            k = kh_ref[kv].astype(F32)
            k_sq = jnp.max(jnp.sum(k * k, axis=1, keepdims=True))
            bound_sq = q_sq * k_sq * BOUND_SLACK
            shift_scr[kv] = jnp.full(shift_scr.shape[1:], jnp.sqrt(bound_sq), F32)
            safe_scr[kv] = (bound_sq <= MAX_SAFE_SHIFT ** 2).astype(jnp.int32)

    def bounded():
        items = [(hd, c) for hd in range(N_HEADS) for c in range(n_chunks)]
        scores = {}

        def issue_scores(j):
            hd, c = items[j]
            keys = kh_ref[hd // group, c * key_chunk:(c + 1) * key_chunk, :]
            scores[j] = _dot_nt(keys, q_ref[hd])

        for j in range(min(ATTN_LOOKAHEAD, len(items))):
            issue_scores(j)
        for i, (hd, c) in enumerate(items):
            if i + ATTN_LOOKAHEAD < len(items):
                issue_scores(i + ATTN_LOOKAHEAD)
            p = jnp.exp2(scores.pop(i) - shift_scr[hd // group][0:1, 0:1])
            ps = jnp.sum(p, axis=0, keepdims=True)
            pv = _dot(vt_ref[hd // group, :, c * key_chunk:(c + 1) * key_chunk], p.astype(BF16))
            denom, acc = (ps, pv) if c == 0 else (denom + ps, acc + pv)
            if c == n_chunks - 1:
                heads_scr[hd] = (acc * (1.0 / denom)).T.astype(BF16)

    def exact():
        def one_head(hd, carry):
            kv = hd // group
            s = _dot_nt(kh_ref[kv], q_ref[hd])
            p = jnp.exp2(s - jnp.max(s, axis=0, keepdims=True))
            denom = jnp.sum(p, axis=0, keepdims=True)
            o_t = _dot(vt_ref[kv], p.astype(BF16)) * (1.0 / denom)
            heads_scr[hd] = o_t.T.astype(BF16)
            return carry

        lax.fori_loop(0, N_HEADS, one_head, 0)

    safe = safe_scr[0]
    for kv in range(1, N_KV_HEADS):
        safe = jnp.minimum(safe, safe_scr[kv])
    pl.when(safe == 1)(bounded)
    pl.when(safe != 1)(exact)

    attn = jnp.concatenate([heads_scr[hd] for hd in range(N_HEADS)], axis=-1)
    o_ref[...] = x_ref[...] + mod_ref[2:3] * _dot(attn, w_o_ref[...])


def _attn_sublayer(q, kh, vt, x, mod, layer, row0, row_stride, w_o, q_gain, tile):
    n_seq, seq_len, _ = x.shape
    n_keys = kh.shape[2]
    return pl.pallas_call(
        functools.partial(_attn_kernel, key_chunk=min(n_keys, ATTN_KEY_CHUNK)),
        out_shape=jax.ShapeDtypeStruct(x.shape, F32),
        grid=(n_seq, seq_len // tile),
        in_specs=[
            pl.BlockSpec((None, N_HEADS, tile, HEAD_DIM), lambda s, t: (s, 0, t, 0)),
            pl.BlockSpec((None, N_KV_HEADS, n_keys, HEAD_DIM), lambda s, t: (s, 0, 0, 0)),
            pl.BlockSpec((None, N_KV_HEADS, HEAD_DIM, n_keys), lambda s, t: (s, 0, 0, 0)),
            pl.BlockSpec((None, tile, D_MODEL), lambda s, t: (s, t, 0)),
            _mod_spec(layer, row0, row_stride),
            _const_spec((D_MODEL, D_MODEL)),
            _const_spec((1, HEAD_DIM)),
        ],
        out_specs=pl.BlockSpec((None, tile, D_MODEL), lambda s, t: (s, t, 0)),
        scratch_shapes=[pltpu.VMEM((N_HEADS, tile, HEAD_DIM), BF16),
                        pltpu.VMEM((N_KV_HEADS, 8, HEAD_DIM), F32),
                        pltpu.SMEM((N_KV_HEADS,), jnp.int32)],
        compiler_params=_params(("arbitrary", "arbitrary")),
        name="attention",
    )(q, kh, vt, x, mod, w_o, q_gain)


def _rope_tables(seq_len):
    rows = seq_len // GRID_W
    r = np.repeat(np.arange(rows), GRID_W).astype(np.float32)
    col = np.tile(np.arange(GRID_W), rows).astype(np.float32)
    expo = -np.arange(0, AXIS_DIM, 2, dtype=np.float32) / np.float32(AXIS_DIM)
    inv = np.power(np.float32(ROPE_THETA), expo).astype(np.float32)
    ar = r[:, None] * inv
    ac = col[:, None] * inv
    ang = np.concatenate([ar, ar, ac, ac], axis=-1)
    cos, sin = np.cos(ang), np.sin(ang)
    first = (np.arange(HEAD_DIM) % AXIS_DIM) < AXIS_DIM // 2
    return (jnp.asarray(_rope_lane_order(cos), F32),
            jnp.asarray(_rope_lane_order(np.where(first, -sin, sin)), F32))


def _rope_weights(w_qkv, q_gain, k_gain):
    qk = (N_HEADS + N_KV_HEADS) * HEAD_DIM
    lead = w_qkv.shape[:-1]
    w_qk = _rope_lane_order(w_qkv[..., :qk].reshape(*lead, N_HEADS + N_KV_HEADS, HEAD_DIM))
    w = jnp.concatenate([w_qk.reshape(*lead, qk), w_qkv[..., qk:]], axis=-1)
    return w, _rope_lane_order(q_gain), _rope_lane_order(k_gain)


def _run_group(x, mod, row0, row_stride, wts, cache, rope_tables, tile, attn_tile):
    new_kv = []
    for l in range(DEPTH):
        i = l // 2
        g1 = wts["norm_g"][l, 0].reshape(1, D_MODEL)
        g2 = wts["norm_g"][l, 1].reshape(1, D_MODEL)
        if l % 2 == 0:
            x = _even_sublayer(x, mod, l, row0, row_stride, g1, wts["w_in_even"][i],
                               wts["pool_w"][i], wts["pool_scale"][i].reshape(1, POOL_WIDTH),
                               wts["conv_w"][i], wts["conv_b"][i].reshape(1, CONV_WIDTH),
                               wts["w_out_even"][i], min(EVEN_TILE, x.shape[1]))
        else:
            sfx = "" if rope_tables is None else "_rope"
            res = _qkv_proj(x, mod, l, row0, row_stride, g1, wts["w_qkv" + sfx][i],
                            wts["q_gain" + sfx][i].reshape(1, HEAD_DIM),
                            wts["k_gain" + sfx][i].reshape(1, HEAD_DIM), rope_tables, cache, i,
                            cache is None, tile)
            q, kh, vt = res[:3]
            if cache is None:
                new_kv.append(res[3:])
            x = _attn_sublayer(q, kh, vt, x, mod, l, row0, row_stride, wts["w_o"][i],
                               wts["q_gain"][i].reshape(1, HEAD_DIM), attn_tile)
        flat = x.reshape(1, -1, D_MODEL) if row_stride == 0 else x
        flat = _ffn_sublayer(flat, mod, l, row0, row_stride, g2, wts["w_ffn_in"], wts["w_ffn_out"],
                             wts["final_g"].reshape(1, D_MODEL), l == DEPTH - 1, FFN_TILE)
        x = flat.reshape(x.shape)
    return x, new_kv


def kernel(x_prompt, x_sample, cache_k, cache_v, c, c_ctx, w_ada, b_ada, norm_g, w_in_even, pool_w,
           pool_scale, conv_w, conv_b, w_out_even, w_qkv, q_gain, k_gain, w_o, w_ffn_in, w_ffn_out,
           final_g):
    n_ctx, ctx_len, _ = x_prompt.shape
    n_lat, lat_len, _ = x_sample.shape
    assert 1 + n_lat <= COND_ROWS

    cond = jnp.concatenate(
        [c_ctx[None, :], c, jnp.zeros((COND_ROWS - 1 - n_lat, D_MODEL), F32)], axis=0)
    mod = _ada(cond, w_ada, b_ada).reshape(DEPTH, COND_ROWS, 6, D_MODEL)

    wts = dict(
        norm_g=norm_g, final_g=final_g, pool_scale=pool_scale, conv_w=conv_w, conv_b=conv_b,
        q_gain=q_gain, k_gain=k_gain,
        w_in_even=w_in_even.astype(BF16), pool_w=pool_w.astype(BF16),
        w_out_even=w_out_even.astype(BF16), w_qkv=w_qkv.astype(BF16), w_o=w_o.astype(BF16),
        w_ffn_in=w_ffn_in.astype(BF16), w_ffn_out=w_ffn_out.astype(BF16),
    )
    w_rope, qg_rope, kg_rope = _rope_weights(wts["w_qkv"], q_gain, k_gain)
    wts.update(w_qkv_rope=w_rope, q_gain_rope=qg_rope, k_gain_rope=kg_rope)

    y_prompt, ctx_kv = _run_group(x_prompt, mod, 0, 0, wts, None, None,
                                  tile=ctx_len, attn_tile=ctx_len)
    new_cache_k = jnp.concatenate([k for k, _ in ctx_kv], axis=1)
    new_cache_v = jnp.concatenate([v for _, v in ctx_kv], axis=1)

    y_sample, _ = _run_group(x_sample, mod, 1, 1, wts, (cache_k, cache_v), _rope_tables(lat_len),
                             tile=512, attn_tile=256)
    return y_prompt, y_sample, new_cache_k, new_cache_v
```

```python
import functools
import math

import jax
import jax.numpy as jnp
import numpy as np
from jax import lax
from jax.experimental import pallas as pl
from jax.experimental.pallas import tpu as pltpu

D_MODEL = 1024
DEPTH = 2
GRID_W = 64
EPS = 1e-6
POOL_WIDTH = D_MODEL // 2
POOL_GROUPS = 4
POOL_GROUP_DIM = POOL_WIDTH // POOL_GROUPS
POOL_WINDOWS = (2, 4, 8, 16)
CONV_WIDTH = D_MODEL // 2
EVEN_IN = POOL_WIDTH + 3 * CONV_WIDTH
HEAD_DIM = 128
N_HEADS = D_MODEL // HEAD_DIM
N_KV_HEADS = N_HEADS // 4
KV_WIDTH = N_KV_HEADS * HEAD_DIM
QKV_WIDTH = (N_HEADS + 2 * N_KV_HEADS) * HEAD_DIM
ROPE_THETA = 10000.0
AXIS_DIM = HEAD_DIM // 2
D_FF = ((8 * D_MODEL // 3 + 255) // 256) * 256
SM_SCALE = HEAD_DIM ** -0.5
LOG2E = math.log2(math.e)

COND_ROWS = 8
HALO = 16
ADA_BLOCK_N = 1536
FF_CHUNKS = ((0, 1024), (1024, 1024), (2048, 768))
QKV_TILE = 1024
QKV_SUB_ROWS = 128
EVEN_TILE = 1024
EVEN_SUB_ROWS = 512
FFN_TILE = 1024
FFN_SUB_ROWS = 256
FFN_LOOKAHEAD = 1
ATTN_TILE = 512
ATTN_KEY_CHUNK = 512
ATTN_LOOKAHEAD = 2
BOUND_SLACK = 1.1
MAX_SAFE_SHIFT = 60.0
VMEM_LIMIT_BYTES = 56 * 1024 * 1024

BF16 = jnp.bfloat16
F32 = jnp.float32


def _dot(a, b):
    return jnp.dot(a, b, preferred_element_type=F32)


def _dot_nt(a, b):
    return lax.dot_general(a, b, (((1,), (1,)), ((), ())), preferred_element_type=F32)


def _silu(a):
    return a * (1.0 / (1.0 + jnp.exp(-a)))


def _rms(x):
    return x * lax.rsqrt(jnp.mean(x * x, axis=-1, keepdims=True) + EPS)


def _modulate(x, g, shift, scale):
    return (_rms(x) * g) * (1.0 + scale) + shift


def _params(sem):
    return pltpu.CompilerParams(dimension_semantics=sem, vmem_limit_bytes=VMEM_LIMIT_BYTES)


def _const_spec(shape):
    n = len(shape)
    return pl.BlockSpec(shape, lambda *_: (0,) * n, pipeline_mode=pl.Buffered(1))


def _layer_spec(shape, layer):
    n = len(shape)
    return pl.BlockSpec((None, *shape), lambda *_: (layer,) + (0,) * n,
                        pipeline_mode=pl.Buffered(1))


def _ada_kernel(cond_ref, w_ref, b_ref, o_ref):
    s = _silu(cond_ref[...]).astype(BF16)
    o_ref[...] = _dot(s, w_ref[...].astype(BF16)) + b_ref[...]


def _ada(cond, w_ada, b_ada):
    n_out = w_ada.shape[-1]
    return pl.pallas_call(
        _ada_kernel,
        out_shape=jax.ShapeDtypeStruct((DEPTH, COND_ROWS, n_out), F32),
        grid=(DEPTH, n_out // ADA_BLOCK_N),
        in_specs=[
            pl.BlockSpec((COND_ROWS, D_MODEL), lambda l, j: (0, 0)),
            pl.BlockSpec((None, D_MODEL, ADA_BLOCK_N), lambda l, j: (l, 0, j)),
            pl.BlockSpec((None, 1, ADA_BLOCK_N), lambda l, j: (l, 0, j)),
        ],
        out_specs=pl.BlockSpec((None, COND_ROWS, ADA_BLOCK_N), lambda l, j: (l, 0, j)),
        compiler_params=_params(("arbitrary", "arbitrary")),
        name="ada",
    )(cond, w_ada, b_ada.reshape(DEPTH, 1, n_out))


def _mod_spec(layer, row0, row_stride):
    return pl.BlockSpec((None, None, 6, D_MODEL),
                        lambda s, t: (layer, row0 + row_stride * s, 0, 0))


def _even_kernel(x_ref, xp_ref, xn_ref, mod_ref, g_ref, w_in_ref, pool_w_ref, pool_scale_ref,
                 conv_w_ref, conv_b_ref, w_out_ref, o_ref, *, tile, seq_len):
    t = pl.program_id(1)
    n_t = pl.num_programs(1)
    sub = min(EVEN_SUB_ROWS, tile)
    n_sub = tile // sub
    win = sub + 2 * HALO
    mod = mod_ref[...]
    shift, scale, gate = mod[0:1], mod[1:2], mod[2:3]
    g = g_ref[...]
    cw = conv_w_ref[...]

    def h_rows(lo, hi):
        pieces = []
        if lo < HALO:
            pieces.append(xp_ref[...])
        pieces.append(x_ref[max(lo, HALO) - HALO:min(hi, HALO + tile) - HALO, :])
        if hi > HALO + tile:
            pieces.append(xn_ref[...])
        return jnp.concatenate([_modulate(r, g, shift, scale).astype(BF16) for r in pieces], axis=0)

    def project(s):
        lo = 0 if s == 0 else s * sub + 2 * HALO
        return _dot(h_rows(lo, (s + 1) * sub + 2 * HALO), w_in_ref[...])

    def edge_masked(a, s):
        parts = [a[:HALO], a[HALO:HALO + sub], a[HALO + sub:]]
        if s == 0:
            parts[0] = jnp.where(t > 0, parts[0], 0.0)
        if s == n_sub - 1:
            parts[2] = jnp.where(t < n_t - 1, parts[2], 0.0)
        return jnp.concatenate(parts, axis=0)

    def shifted(a, k):
        return pltpu.roll(a, k % win, 0)

    def mixers(s, p):
        pos = t * tile + s * sub + lax.broadcasted_iota(jnp.int32, (sub, 1), 0)
        u = edge_masked(p[:, :POOL_WIDTH], s)
        ya = []
        for i, w in enumerate(POOL_WINDOWS):
            ug = u[:, i * POOL_GROUP_DIM:(i + 1) * POOL_GROUP_DIM]
            fwd, n = ug, 1
            while n < w // 2:
                fwd = fwd + shifted(fwd, -n)
                n *= 2
            wsum = shifted(fwd, w // 2) + fwd
            cnt = (jnp.clip(pos + w // 2, 0, seq_len)
                   - jnp.clip(pos - w // 2, 0, seq_len)).astype(F32)
            pooled = wsum[HALO:HALO + sub] / cnt - ug[HALO:HALO + sub]
            ya.append(_dot(pooled.astype(BF16), pool_w_ref[i]))
        ya = jnp.concatenate(ya, axis=-1) * pool_scale_ref[...]
        c0 = POOL_WIDTH
        bg = p[HALO:HALO + sub, c0:c0 + CONV_WIDTH]
        z = edge_masked(p[:, c0 + CONV_WIDTH:c0 + 2 * CONV_WIDTH] * p[:, c0 + 2 * CONV_WIDTH:], s)
        conv = (shifted(z, 1) * cw[0:1] + z * cw[1:2] + shifted(z, -1) * cw[2:3])[HALO:HALO + sub]
        yb = bg * (conv + conv_b_ref[...])
        y = _dot(jnp.concatenate([ya, yb], axis=-1).astype(BF16), w_out_ref[...])
        rows = slice(s * sub, (s + 1) * sub)
        o_ref[rows, :] = x_ref[rows, :] + gate * y

    block = project(0)
    window = None
    for s in range(n_sub):
        window = block if s == 0 else jnp.concatenate([window[sub:], block], axis=0)
        if s + 1 < n_sub:
            block = project(s + 1)
        mixers(s, window)


def _even_sublayer(x, mod, layer, row0, row_stride, g, w_in, pool_w, pool_scale, conv_w, conv_b,
                   w_out, tile):
    n_seq, seq_len, _ = x.shape
    n_t = seq_len // tile
    hb = tile // HALO
    n_hb = seq_len // HALO
    kern = functools.partial(_even_kernel, tile=tile, seq_len=seq_len)
    return pl.pallas_call(
        kern,
        out_shape=jax.ShapeDtypeStruct(x.shape, F32),
        grid=(n_seq, n_t),
        in_specs=[
            pl.BlockSpec((None, tile, D_MODEL), lambda s, t: (s, t, 0)),
            pl.BlockSpec((None, HALO, D_MODEL), lambda s, t: (s, jnp.maximum(t * hb - 1, 0), 0)),
            pl.BlockSpec((None, HALO, D_MODEL),
                         lambda s, t: (s, jnp.minimum((t + 1) * hb, n_hb - 1), 0)),
            _mod_spec(layer, row0, row_stride),
            _const_spec((1, D_MODEL)),
            _const_spec((D_MODEL, EVEN_IN)),
            _const_spec((POOL_GROUPS, POOL_GROUP_DIM, POOL_GROUP_DIM)),
            _const_spec((1, POOL_WIDTH)),
            _const_spec((3, CONV_WIDTH)),
            _const_spec((1, CONV_WIDTH)),
            _const_spec((POOL_WIDTH + CONV_WIDTH, D_MODEL)),
        ],
        out_specs=pl.BlockSpec((None, tile, D_MODEL), lambda s, t: (s, t, 0)),
        compiler_params=_params(("arbitrary", "arbitrary")),
        name="even_mixer",
    )(x, x, x, mod, g, w_in, pool_w, pool_scale, conv_w, conv_b, w_out)


def _ffn_kernel(x_ref, mod_ref, g_ref, w_in_ref, w_out_ref, fg_ref, o_ref, *, final_norm):
    mod = mod_ref[...]
    shift, scale, gate = mod[3:4], mod[4:5], mod[5:6]
    g = g_ref[...]
    n_sub = x_ref.shape[0] // FFN_SUB_ROWS
    items = [(sub, c) for sub in range(n_sub) for c in range(len(FF_CHUNKS))]
    hs, ups = {}, {}

    def issue_up(j):
        sub, c = items[j]
        if c == 0:
            rows = x_ref[sub * FFN_SUB_ROWS:(sub + 1) * FFN_SUB_ROWS, :]
            hs[sub] = _modulate(rows, g, shift, scale).astype(BF16)
        c0, cw = FF_CHUNKS[c]
        ups[j] = (_dot(hs[sub], w_in_ref[:, c0:c0 + cw]),
                  _dot(hs[sub], w_in_ref[:, D_FF + c0:D_FF + c0 + cw]))

    for j in range(min(FFN_LOOKAHEAD, len(items))):
        issue_up(j)
    for i, (sub, c) in enumerate(items):
        if i + FFN_LOOKAHEAD < len(items):
            issue_up(i + FFN_LOOKAHEAD)
        a, b = ups.pop(i)
        c0, cw = FF_CHUNKS[c]
        part = _dot((_silu(a) * b).astype(BF16), w_out_ref[c0:c0 + cw, :])
        acc = part if c == 0 else acc + part
        if c == len(FF_CHUNKS) - 1:
            rows = slice(sub * FFN_SUB_ROWS, (sub + 1) * FFN_SUB_ROWS)
            out = x_ref[rows, :] + gate * acc
            if final_norm:
                out = _rms(out) * fg_ref[...]
            o_ref[rows, :] = out


def _ffn_sublayer(x, mod, layer, row0, row_stride, g, w_in, w_out, final_g, final_norm, tile):
    n_seq, seq_len, _ = x.shape
    kern = functools.partial(_ffn_kernel, final_norm=final_norm)
    return pl.pallas_call(
        kern,
        out_shape=jax.ShapeDtypeStruct(x.shape, F32),
        grid=(n_seq, seq_len // tile),
        in_specs=[
            pl.BlockSpec((None, tile, D_MODEL), lambda s, t: (s, t, 0)),
            _mod_spec(layer, row0, row_stride),
            _const_spec((1, D_MODEL)),
            _layer_spec((D_MODEL, 2 * D_FF), layer),
            _layer_spec((D_FF, D_MODEL), layer),
            _const_spec((1, D_MODEL)),
        ],
        out_specs=pl.BlockSpec((None, tile, D_MODEL), lambda s, t: (s, t, 0)),
        compiler_params=_params(("arbitrary", "arbitrary")),
        name="ffn",
    )(x, mod, g, w_in, w_out, final_g)


def _rope_lane_order(a):
    lead = a.shape[:-1]
    return a.reshape(*lead, 2, 2, AXIS_DIM // 2).swapaxes(-3, -2).reshape(*lead, HEAD_DIM)


def _rope(a, cos, sin_signed):
    return a * cos + pltpu.roll(a, HEAD_DIM // 2, 1) * sin_signed


def _pair_mean_matrix():
    r = lax.broadcasted_iota(jnp.int32, (2 * HEAD_DIM, 2 * HEAD_DIM), 0) // HEAD_DIM
    c = lax.broadcasted_iota(jnp.int32, (2 * HEAD_DIM, 2 * HEAD_DIM), 1) // HEAD_DIM
    return jnp.where(r == c, 1.0 / HEAD_DIM, 0.0).astype(BF16)


def _rope_lane_matrix():
    src = lax.broadcasted_iota(jnp.int32, (HEAD_DIM, HEAD_DIM), 0)
    n = lax.broadcasted_iota(jnp.int32, (HEAD_DIM, HEAD_DIM), 1)
    quarter = AXIS_DIM // 2
    half, axis, c = n // AXIS_DIM, (n // quarter) % 2, n % quarter
    return jnp.where(src == axis * AXIS_DIM + half * quarter + c, 1.0, 0.0).astype(BF16)


def _qkv_kernel(*refs, rope, cached, keep_f32):
    it = iter(refs)
    x_ref, mod_ref, g_ref, w_ref, qg_ref, kg_ref = (next(it) for _ in range(6))
    rope_refs = [next(it) for _ in range(2)] if rope else None
    cache_refs = [next(it) for _ in range(2)] if cached else None
    q_ref, kh_ref, vt_ref = (next(it) for _ in range(3))
    cache_out_refs = [next(it) for _ in range(2)] if cached else None
    f32_refs = [next(it) for _ in range(2)] if keep_f32 else None

    if cached:
        @pl.when(pl.program_id(1) == 0)
        def _():
            for kv in range(N_KV_HEADS):
                k = cache_refs[0][:, kv, :].astype(BF16)
                if rope:
                    k = _dot(k, _rope_lane_matrix()).astype(BF16)
                cache_out_refs[0][kv] = k
                cache_out_refs[1][kv] = cache_refs[1][:, kv, :].T.astype(BF16)

    mod = mod_ref[...]
    shift, scale = mod[0:1], mod[1:2]
    g = g_ref[...]
    gains = [qg_ref[...] * (SM_SCALE * LOG2E)] * N_HEADS + [kg_ref[...]] * N_KV_HEADS
    pair_mean = _pair_mean_matrix()
    v0 = (N_HEADS + N_KV_HEADS) * HEAD_DIM
    sub_rows = min(QKV_SUB_ROWS, x_ref.shape[0])
    n_sub = x_ref.shape[0] // sub_rows

    def rows_of(sub):
        return slice(sub * sub_rows, (sub + 1) * sub_rows)

    def projection(sub):
        h = _modulate(x_ref[rows_of(sub), :], g, shift, scale).astype(BF16)
        return _dot(h, w_ref[...])

    def finish(sub, p):
        rows = rows_of(sub)
        if rope:
            tables = [r[rows, :] for r in rope_refs]
        for pair in range((N_HEADS + N_KV_HEADS) // 2):
            blk = p[:, pair * 2 * HEAD_DIM:(pair + 1) * 2 * HEAD_DIM]
            ms = _dot((blk * blk).astype(BF16), pair_mean)
            normed = blk * lax.rsqrt(ms + EPS)
            for half in range(2):
                hd = 2 * pair + half
                a = normed[:, half * HEAD_DIM:(half + 1) * HEAD_DIM] * gains[hd]
                if rope:
                    a = _rope(a, *tables)
                if hd < N_HEADS:
                    q_ref[hd, rows, :] = a.astype(BF16)
                else:
                    kv = hd - N_HEADS
                    v = p[:, v0 + kv * HEAD_DIM:v0 + (kv + 1) * HEAD_DIM]
                    kh_ref[kv, rows, :] = a.astype(BF16)
                    vt_ref[kv, :, rows] = v.T.astype(BF16)
                    if keep_f32:
                        f32_refs[0][rows, kv, :] = a
                        f32_refs[1][rows, kv, :] = v

    p_next = projection(0)
    for sub in range(n_sub):
        p_cur = p_next
        if sub + 1 < n_sub:
            p_next = projection(sub + 1)
        finish(sub, p_cur)


def _qkv_proj(x, mod, layer, row0, row_stride, g, w_qkv, q_gain, k_gain, rope_tables, cache,
              cache_layer, keep_f32, tile):
    n_seq, seq_len, _ = x.shape
    rope = rope_tables is not None
    cached = cache is not None
    in_specs = [
        pl.BlockSpec((None, tile, D_MODEL), lambda s, t: (s, t, 0)),
        _mod_spec(layer, row0, row_stride),
        _const_spec((1, D_MODEL)),
        _const_spec((D_MODEL, QKV_WIDTH)),
        _const_spec((1, HEAD_DIM)),
        _const_spec((1, HEAD_DIM)),
    ]
    args = [x, mod, g, w_qkv, q_gain, k_gain]
    if rope:
        in_specs += [pl.BlockSpec((tile, HEAD_DIM), lambda s, t: (t, 0))] * 2
        args += list(rope_tables)
    out_shape = [jax.ShapeDtypeStruct((n_seq, N_HEADS, seq_len, HEAD_DIM), BF16),
                 jax.ShapeDtypeStruct((n_seq, N_KV_HEADS, seq_len, HEAD_DIM), BF16),
                 jax.ShapeDtypeStruct((n_seq, N_KV_HEADS, HEAD_DIM, seq_len), BF16)]
    out_specs = [pl.BlockSpec((None, N_HEADS, tile, HEAD_DIM), lambda s, t: (s, 0, t, 0)),
                 pl.BlockSpec((None, N_KV_HEADS, tile, HEAD_DIM), lambda s, t: (s, 0, t, 0)),
                 pl.BlockSpec((None, N_KV_HEADS, HEAD_DIM, tile), lambda s, t: (s, 0, 0, t))]
    if cached:
        past = cache[0].shape[2]
        in_specs += [pl.BlockSpec((None, None, past, N_KV_HEADS, HEAD_DIM),
                                  lambda s, t: (s, cache_layer, 0, 0, 0))] * 2
        args += list(cache)
        out_shape += [jax.ShapeDtypeStruct((n_seq, N_KV_HEADS, past, HEAD_DIM), BF16),
                      jax.ShapeDtypeStruct((n_seq, N_KV_HEADS, HEAD_DIM, past), BF16)]
        out_specs += [pl.BlockSpec((None, N_KV_HEADS, past, HEAD_DIM), lambda s, t: (s, 0, 0, 0)),
                      pl.BlockSpec((None, N_KV_HEADS, HEAD_DIM, past), lambda s, t: (s, 0, 0, 0))]
    if keep_f32:
        out_shape += [jax.ShapeDtypeStruct((n_seq, 1, seq_len, N_KV_HEADS, HEAD_DIM), F32)] * 2
        out_specs += [pl.BlockSpec((None, None, tile, N_KV_HEADS, HEAD_DIM),
                                   lambda s, t: (s, 0, t, 0, 0))] * 2
    return pl.pallas_call(
        functools.partial(_qkv_kernel, rope=rope, cached=cached, keep_f32=keep_f32),
        out_shape=tuple(out_shape),
        grid=(n_seq, seq_len // tile),
        in_specs=in_specs,
        out_specs=tuple(out_specs),
        compiler_params=_params(("arbitrary", "arbitrary")),
        name="qkv_proj",
    )(*args)


def _attn_kernel(*refs, n_segments, key_chunk):
    q_ref = refs[0]
    segments = [(refs[1 + 2 * i], refs[2 + 2 * i]) for i in range(n_segments)]
    x_ref, mod_ref, w_o_ref, qg_ref, o_ref, heads_scr, shift_scr, safe_scr = refs[1 + 2 * n_segments:]
    group = N_HEADS // N_KV_HEADS
    chunks = [(seg, c0, min(key_chunk, k_ref.shape[1] - c0))
              for seg, (k_ref, _) in enumerate(segments)
              for c0 in range(0, k_ref.shape[1], key_chunk)]

    @pl.when(pl.program_id(1) == 0)
    def _():
        qg = qg_ref[...]
        q_sq = HEAD_DIM * (SM_SCALE * LOG2E) ** 2 * jnp.max(qg * qg)
        for kv in range(N_KV_HEADS):
            k_sq = None
            for k_ref, _ in segments:
                k = k_ref[kv].astype(F32)
                seg_sq = jnp.max(jnp.sum(k * k, axis=1, keepdims=True))
                k_sq = seg_sq if k_sq is None else jnp.maximum(k_sq, seg_sq)
            bound_sq = q_sq * k_sq * BOUND_SLACK
            shift_scr[kv] = jnp.full(shift_scr.shape[1:], jnp.sqrt(bound_sq), F32)
            safe_scr[kv] = (bound_sq <= MAX_SAFE_SHIFT ** 2).astype(jnp.int32)

    def bounded():
        items = [(hd, ci) for hd in range(N_HEADS) for ci in range(len(chunks))]
        scores = {}

        def issue_scores(j):
            hd, ci = items[j]
            seg, c0, rows = chunks[ci]
            keys = segments[seg][0][hd // group, c0:c0 + rows, :]
            scores[j] = _dot_nt(keys, q_ref[hd])

        for j in range(min(ATTN_LOOKAHEAD, len(items))):
            issue_scores(j)
        for i, (hd, ci) in enumerate(items):
            if i + ATTN_LOOKAHEAD < len(items):
                issue_scores(i + ATTN_LOOKAHEAD)
            seg, c0, rows = chunks[ci]
            p = jnp.exp2(scores.pop(i) - shift_scr[hd // group][0:1, 0:1])
            ps = jnp.sum(p, axis=0, keepdims=True)
            pv = _dot(segments[seg][1][hd // group, :, c0:c0 + rows], p.astype(BF16))
            denom, acc = (ps, pv) if ci == 0 else (denom + ps, acc + pv)
            if ci == len(chunks) - 1:
                heads_scr[hd] = (acc * (1.0 / denom)).T.astype(BF16)

    def exact():
        def one_head(hd, carry):
            kv = hd // group
            q = q_ref[hd]
            scores = [_dot_nt(k_ref[kv], q) for k_ref, _ in segments]
            m = None
            for s in scores:
                sm = jnp.max(s, axis=0, keepdims=True)
                m = sm if m is None else jnp.maximum(m, sm)
            denom = o_t = None
            for s, (_, vt_ref) in zip(scores, segments):
                p = jnp.exp2(s - m)
                ps = jnp.sum(p, axis=0, keepdims=True)
                pv = _dot(vt_ref[kv], p.astype(BF16))
                denom, o_t = (ps, pv) if denom is None else (denom + ps, o_t + pv)
            heads_scr[hd] = (o_t * (1.0 / denom)).T.astype(BF16)
            return carry

        lax.fori_loop(0, N_HEADS, one_head, 0)

    safe = safe_scr[0]
    for kv in range(1, N_KV_HEADS):
        safe = jnp.minimum(safe, safe_scr[kv])
    pl.when(safe == 1)(bounded)
    pl.when(safe != 1)(exact)

    attn = jnp.concatenate([heads_scr[hd] for hd in range(N_HEADS)], axis=-1)
    o_ref[...] = x_ref[...] + mod_ref[2:3] * _dot(attn, w_o_ref[...])


def _attn_sublayer(q, key_segments, x, mod, layer, row0, row_stride, w_o, q_gain, tile):
    n_seq, seq_len, _ = x.shape
    in_specs = [pl.BlockSpec((None, N_HEADS, tile, HEAD_DIM), lambda s, t: (s, 0, t, 0))]
    args = [q]
    for k, vt in key_segments:
        n_keys = k.shape[2]
        in_specs += [pl.BlockSpec((None, N_KV_HEADS, n_keys, HEAD_DIM), lambda s, t: (s, 0, 0, 0)),
                     pl.BlockSpec((None, N_KV_HEADS, HEAD_DIM, n_keys), lambda s, t: (s, 0, 0, 0))]
        args += [k, vt]
    in_specs += [
        pl.BlockSpec((None, tile, D_MODEL), lambda s, t: (s, t, 0)),
        _mod_spec(layer, row0, row_stride),
        _const_spec((D_MODEL, D_MODEL)),
        _const_spec((1, HEAD_DIM)),
    ]
    args += [x, mod, w_o, q_gain]
    return pl.pallas_call(
        functools.partial(_attn_kernel, n_segments=len(key_segments), key_chunk=ATTN_KEY_CHUNK),
        out_shape=jax.ShapeDtypeStruct(x.shape, F32),
        grid=(n_seq, seq_len // tile),
        in_specs=in_specs,
        out_specs=pl.BlockSpec((None, tile, D_MODEL), lambda s, t: (s, t, 0)),
        scratch_shapes=[pltpu.VMEM((N_HEADS, tile, HEAD_DIM), BF16),
                        pltpu.VMEM((N_KV_HEADS, 8, HEAD_DIM), F32),
                        pltpu.SMEM((N_KV_HEADS,), jnp.int32)],
        compiler_params=_params(("arbitrary", "arbitrary")),
        name="attention",
    )(*args)


def _rope_tables(seq_len):
    rows = seq_len // GRID_W
    r = np.repeat(np.arange(rows), GRID_W).astype(np.float32)
    col = np.tile(np.arange(GRID_W), rows).astype(np.float32)
    expo = -np.arange(0, AXIS_DIM, 2, dtype=np.float32) / np.float32(AXIS_DIM)
    inv = np.power(np.float32(ROPE_THETA), expo).astype(np.float32)
    ar = r[:, None] * inv
    ac = col[:, None] * inv
    ang = np.concatenate([ar, ar, ac, ac], axis=-1)
    cos, sin = np.cos(ang), np.sin(ang)
    first = (np.arange(HEAD_DIM) % AXIS_DIM) < AXIS_DIM // 2
    return (jnp.asarray(_rope_lane_order(cos), F32),
            jnp.asarray(_rope_lane_order(np.where(first, -sin, sin)), F32))


def _rope_weights(w_qkv, q_gain, k_gain):
    qk = (N_HEADS + N_KV_HEADS) * HEAD_DIM
    lead = w_qkv.shape[:-1]
    w_qk = _rope_lane_order(w_qkv[..., :qk].reshape(*lead, N_HEADS + N_KV_HEADS, HEAD_DIM))
    w = jnp.concatenate([w_qk.reshape(*lead, qk), w_qkv[..., qk:]], axis=-1)
    return w, _rope_lane_order(q_gain), _rope_lane_order(k_gain)


def _run_group(x, mod, row0, row_stride, wts, cache, rope_tables, attn_tile):
    new_kv = []
    for l in range(DEPTH):
        i = l // 2
        g1 = wts["norm_g"][l, 0].reshape(1, D_MODEL)
        g2 = wts["norm_g"][l, 1].reshape(1, D_MODEL)
        if l % 2 == 0:
            x = _even_sublayer(x, mod, l, row0, row_stride, g1, wts["w_in_even"][i],
                               wts["pool_w"][i], wts["pool_scale"][i].reshape(1, POOL_WIDTH),
                               wts["conv_w"][i], wts["conv_b"][i].reshape(1, CONV_WIDTH),
                               wts["w_out_even"][i], min(EVEN_TILE, x.shape[1]))
        else:
            sfx = "" if rope_tables is None else "_rope"
            res = _qkv_proj(x, mod, l, row0, row_stride, g1, wts["w_qkv" + sfx][i],
                            wts["q_gain" + sfx][i].reshape(1, HEAD_DIM),
                            wts["k_gain" + sfx][i].reshape(1, HEAD_DIM), rope_tables, cache, i,
                            cache is None, min(QKV_TILE, x.shape[1]))
            q, segments = res[0], [res[1:3]]
            if cache is None:
                new_kv.append(res[3:])
            else:
                segments.append(res[3:5])
            x = _attn_sublayer(q, segments, x, mod, l, row0, row_stride, wts["w_o"][i],
                               wts["q_gain"][i].reshape(1, HEAD_DIM), attn_tile)
        flat = x.reshape(1, -1, D_MODEL) if row_stride == 0 else x
        flat = _ffn_sublayer(flat, mod, l, row0, row_stride, g2, wts["w_ffn_in"], wts["w_ffn_out"],
                             wts["final_g"].reshape(1, D_MODEL), l == DEPTH - 1, FFN_TILE)
        x = flat.reshape(x.shape)
    return x, new_kv


def kernel(x_prompt, x_sample, cache_k, cache_v, c, c_ctx, w_ada, b_ada, norm_g, w_in_even, pool_w,
           pool_scale, conv_w, conv_b, w_out_even, w_qkv, q_gain, k_gain, w_o, w_ffn_in, w_ffn_out,
           final_g):
    n_ctx, ctx_len, _ = x_prompt.shape
    n_lat, lat_len, _ = x_sample.shape
    assert 1 + n_lat <= COND_ROWS

    cond = jnp.concatenate(
        [c_ctx[None, :], c, jnp.zeros((COND_ROWS - 1 - n_lat, D_MODEL), F32)], axis=0)
    mod = _ada(cond, w_ada, b_ada).reshape(DEPTH, COND_ROWS, 6, D_MODEL)

    wts = dict(
        norm_g=norm_g, final_g=final_g, pool_scale=pool_scale, conv_w=conv_w, conv_b=conv_b,
        q_gain=q_gain, k_gain=k_gain,
        w_in_even=w_in_even.astype(BF16), pool_w=pool_w.astype(BF16),
        w_out_even=w_out_even.astype(BF16), w_qkv=w_qkv.astype(BF16), w_o=w_o.astype(BF16),
        w_ffn_in=w_ffn_in.astype(BF16), w_ffn_out=w_ffn_out.astype(BF16),
    )
    w_rope, qg_rope, kg_rope = _rope_weights(wts["w_qkv"], q_gain, k_gain)
    wts.update(w_qkv_rope=w_rope, q_gain_rope=qg_rope, k_gain_rope=kg_rope)

    y_prompt, ctx_kv = _run_group(x_prompt, mod, 0, 0, wts, None, None, attn_tile=ctx_len)
    new_cache_k = jnp.concatenate([k for k, _ in ctx_kv], axis=1)
    new_cache_v = jnp.concatenate([v for _, v in ctx_kv], axis=1)

    y_sample, _ = _run_group(x_sample, mod, 1, 1, wts, (cache_k, cache_v), _rope_tables(lat_len),
                             attn_tile=ATTN_TILE)
    return y_prompt, y_sample, new_cache_k, new_cache_v
```

```python
import functools
import math

import jax
import jax.numpy as jnp
import numpy as np
from jax import lax
from jax.experimental import pallas as pl
from jax.experimental.pallas import tpu as pltpu

D_MODEL = 1024
DEPTH = 2
GRID_W = 64
EPS = 1e-6
POOL_WIDTH = D_MODEL // 2
POOL_GROUPS = 4
POOL_GROUP_DIM = POOL_WIDTH // POOL_GROUPS
POOL_WINDOWS = (2, 4, 8, 16)
CONV_WIDTH = D_MODEL // 2
EVEN_IN = POOL_WIDTH + 3 * CONV_WIDTH
HEAD_DIM = 128
N_HEADS = D_MODEL // HEAD_DIM
N_KV_HEADS = N_HEADS // 4
KV_WIDTH = N_KV_HEADS * HEAD_DIM
QKV_WIDTH = (N_HEADS + 2 * N_KV_HEADS) * HEAD_DIM
ROPE_THETA = 10000.0
AXIS_DIM = HEAD_DIM // 2
D_FF = ((8 * D_MODEL // 3 + 255) // 256) * 256
SM_SCALE = HEAD_DIM ** -0.5
LOG2E = math.log2(math.e)

COND_ROWS = 8
HALO = 16
ADA_BLOCK_N = 1536
FF_CHUNKS = ((0, 1024), (1024, 1024), (2048, 768))
QKV_TILE = 1024
QKV_SUB_ROWS = 128
EVEN_TILE = 1024
EVEN_SUB_ROWS = 512
FFN_TILE = 1024
FFN_SUB_ROWS = 256
FFN_LOOKAHEAD = 1
ATTN_TILE = 256
ATTN_KEY_CHUNK = 512
ATTN_LOOKAHEAD = 2
BOUND_SLACK = 1.1
MAX_SAFE_SHIFT = 60.0
VMEM_LIMIT_BYTES = 56 * 1024 * 1024

BF16 = jnp.bfloat16
F32 = jnp.float32


def _dot(a, b):
    return jnp.dot(a, b, preferred_element_type=F32)


def _dot_nt(a, b):
    return lax.dot_general(a, b, (((1,), (1,)), ((), ())), preferred_element_type=F32)


def _silu(a):
    return a * (1.0 / (1.0 + jnp.exp(-a)))


def _rms(x):
    return x * lax.rsqrt(jnp.mean(x * x, axis=-1, keepdims=True) + EPS)


def _modulate(x, g, shift, scale):
    return (_rms(x) * g) * (1.0 + scale) + shift


def _params(sem):
    return pltpu.CompilerParams(dimension_semantics=sem, vmem_limit_bytes=VMEM_LIMIT_BYTES)


def _const_spec(shape):
    n = len(shape)
    return pl.BlockSpec(shape, lambda *_: (0,) * n, pipeline_mode=pl.Buffered(1))


def _layer_spec(shape, layer):
    n = len(shape)
    return pl.BlockSpec((None, *shape), lambda *_: (layer,) + (0,) * n,
                        pipeline_mode=pl.Buffered(1))


def _ada_kernel(cond_ref, w_ref, b_ref, o_ref):
    s = _silu(cond_ref[...]).astype(BF16)
    o_ref[...] = _dot(s, w_ref[...].astype(BF16)) + b_ref[...]


def _ada(cond, w_ada, b_ada):
    n_out = w_ada.shape[-1]
    return pl.pallas_call(
        _ada_kernel,
        out_shape=jax.ShapeDtypeStruct((DEPTH, COND_ROWS, n_out), F32),
        grid=(DEPTH, n_out // ADA_BLOCK_N),
        in_specs=[
            pl.BlockSpec((COND_ROWS, D_MODEL), lambda l, j: (0, 0)),
            pl.BlockSpec((None, D_MODEL, ADA_BLOCK_N), lambda l, j: (l, 0, j)),
            pl.BlockSpec((None, 1, ADA_BLOCK_N), lambda l, j: (l, 0, j)),
        ],
        out_specs=pl.BlockSpec((None, COND_ROWS, ADA_BLOCK_N), lambda l, j: (l, 0, j)),
        compiler_params=_params(("arbitrary", "arbitrary")),
        name="ada",
    )(cond, w_ada, b_ada.reshape(DEPTH, 1, n_out))


def _mod_spec(layer, row0, row_stride):
    return pl.BlockSpec((None, None, 6, D_MODEL),
                        lambda s, t: (layer, row0 + row_stride * s, 0, 0))


def _even_kernel(x_ref, xp_ref, xn_ref, mod_ref, g_ref, w_in_ref, pool_w_ref, pool_scale_ref,
                 conv_w_ref, conv_b_ref, w_out_ref, o_ref, *, tile, seq_len):
    t = pl.program_id(1)
    n_t = pl.num_programs(1)
    sub = min(EVEN_SUB_ROWS, tile)
    n_sub = tile // sub
    win = sub + 2 * HALO
    mod = mod_ref[...]
    shift, scale, gate = mod[0:1], mod[1:2], mod[2:3]
    g = g_ref[...]
    cw = conv_w_ref[...]

    def h_rows(lo, hi):
        pieces = []
        if lo < HALO:
            pieces.append(xp_ref[...])
        pieces.append(x_ref[max(lo, HALO) - HALO:min(hi, HALO + tile) - HALO, :])
        if hi > HALO + tile:
            pieces.append(xn_ref[...])
        return jnp.concatenate([_modulate(r, g, shift, scale).astype(BF16) for r in pieces], axis=0)

    def project(s):
        lo = 0 if s == 0 else s * sub + 2 * HALO
        return _dot(h_rows(lo, (s + 1) * sub + 2 * HALO), w_in_ref[...])

    def edge_masked(a, s):
        parts = [a[:HALO], a[HALO:HALO + sub], a[HALO + sub:]]
        if s == 0:
            parts[0] = jnp.where(t > 0, parts[0], 0.0)
        if s == n_sub - 1:
            parts[2] = jnp.where(t < n_t - 1, parts[2], 0.0)
        return jnp.concatenate(parts, axis=0)

    def shifted(a, k):
        return pltpu.roll(a, k % win, 0)

    def mixers(s, p):
        pos = t * tile + s * sub + lax.broadcasted_iota(jnp.int32, (sub, 1), 0)
        u = edge_masked(p[:, :POOL_WIDTH], s)
        ya = []
        for i, w in enumerate(POOL_WINDOWS):
            ug = u[:, i * POOL_GROUP_DIM:(i + 1) * POOL_GROUP_DIM]
            fwd, n = ug, 1
            while n < w // 2:
                fwd = fwd + shifted(fwd, -n)
                n *= 2
            wsum = shifted(fwd, w // 2) + fwd
            cnt = (jnp.clip(pos + w // 2, 0, seq_len)
                   - jnp.clip(pos - w // 2, 0, seq_len)).astype(F32)
            pooled = wsum[HALO:HALO + sub] / cnt - ug[HALO:HALO + sub]
            ya.append(_dot(pooled.astype(BF16), pool_w_ref[i]))
        ya = jnp.concatenate(ya, axis=-1) * pool_scale_ref[...]
        c0 = POOL_WIDTH
        bg = p[HALO:HALO + sub, c0:c0 + CONV_WIDTH]
        z = edge_masked(p[:, c0 + CONV_WIDTH:c0 + 2 * CONV_WIDTH] * p[:, c0 + 2 * CONV_WIDTH:], s)
        conv = (shifted(z, 1) * cw[0:1] + z * cw[1:2] + shifted(z, -1) * cw[2:3])[HALO:HALO + sub]
        yb = bg * (conv + conv_b_ref[...])
        y = _dot(jnp.concatenate([ya, yb], axis=-1).astype(BF16), w_out_ref[...])
        rows = slice(s * sub, (s + 1) * sub)
        o_ref[rows, :] = x_ref[rows, :] + gate * y

    block = project(0)
    window = None
    for s in range(n_sub):
        window = block if s == 0 else jnp.concatenate([window[sub:], block], axis=0)
        if s + 1 < n_sub:
            block = project(s + 1)
        mixers(s, window)


def _even_sublayer(x, mod, layer, row0, row_stride, g, w_in, pool_w, pool_scale, conv_w, conv_b,
                   w_out, tile):
    n_seq, seq_len, _ = x.shape
    n_t = seq_len // tile
    hb = tile // HALO
    n_hb = seq_len // HALO
    kern = functools.partial(_even_kernel, tile=tile, seq_len=seq_len)
    return pl.pallas_call(
        kern,
        out_shape=jax.ShapeDtypeStruct(x.shape, F32),
        grid=(n_seq, n_t),
        in_specs=[
            pl.BlockSpec((None, tile, D_MODEL), lambda s, t: (s, t, 0)),
            pl.BlockSpec((None, HALO, D_MODEL), lambda s, t: (s, jnp.maximum(t * hb - 1, 0), 0)),
            pl.BlockSpec((None, HALO, D_MODEL),
                         lambda s, t: (s, jnp.minimum((t + 1) * hb, n_hb - 1), 0)),
            _mod_spec(layer, row0, row_stride),
            _const_spec((1, D_MODEL)),
            _const_spec((D_MODEL, EVEN_IN)),
            _const_spec((POOL_GROUPS, POOL_GROUP_DIM, POOL_GROUP_DIM)),
            _const_spec((1, POOL_WIDTH)),
            _const_spec((3, CONV_WIDTH)),
            _const_spec((1, CONV_WIDTH)),
            _const_spec((POOL_WIDTH + CONV_WIDTH, D_MODEL)),
        ],
        out_specs=pl.BlockSpec((None, tile, D_MODEL), lambda s, t: (s, t, 0)),
        compiler_params=_params(("arbitrary", "arbitrary")),
        name="even_mixer",
    )(x, x, x, mod, g, w_in, pool_w, pool_scale, conv_w, conv_b, w_out)


def _ffn_kernel(x_ref, mod_ref, g_ref, w_in_ref, w_out_ref, fg_ref, o_ref, *, final_norm):
    mod = mod_ref[...]
    shift, scale, gate = mod[3:4], mod[4:5], mod[5:6]
    g = g_ref[...]
    n_sub = x_ref.shape[0] // FFN_SUB_ROWS
    items = [(sub, c) for sub in range(n_sub) for c in range(len(FF_CHUNKS))]
    hs, ups = {}, {}

    def issue_up(j):
        sub, c = items[j]
        if c == 0:
            rows = x_ref[sub * FFN_SUB_ROWS:(sub + 1) * FFN_SUB_ROWS, :]
            hs[sub] = _modulate(rows, g, shift, scale).astype(BF16)
        c0, cw = FF_CHUNKS[c]
        ups[j] = (_dot(hs[sub], w_in_ref[:, c0:c0 + cw]),
                  _dot(hs[sub], w_in_ref[:, D_FF + c0:D_FF + c0 + cw]))

    for j in range(min(FFN_LOOKAHEAD, len(items))):
        issue_up(j)
    for i, (sub, c) in enumerate(items):
        if i + FFN_LOOKAHEAD < len(items):
            issue_up(i + FFN_LOOKAHEAD)
        a, b = ups.pop(i)
        c0, cw = FF_CHUNKS[c]
        part = _dot((_silu(a) * b).astype(BF16), w_out_ref[c0:c0 + cw, :])
        acc = part if c == 0 else acc + part
        if c == len(FF_CHUNKS) - 1:
            rows = slice(sub * FFN_SUB_ROWS, (sub + 1) * FFN_SUB_ROWS)
            out = x_ref[rows, :] + gate * acc
            if final_norm:
                out = _rms(out) * fg_ref[...]
            o_ref[rows, :] = out


def _ffn_sublayer(x, mod, layer, row0, row_stride, g, w_in, w_out, final_g, final_norm, tile):
    n_seq, seq_len, _ = x.shape
    kern = functools.partial(_ffn_kernel, final_norm=final_norm)
    return pl.pallas_call(
        kern,
        out_shape=jax.ShapeDtypeStruct(x.shape, F32),
        grid=(n_seq, seq_len // tile),
        in_specs=[
            pl.BlockSpec((None, tile, D_MODEL), lambda s, t: (s, t, 0)),
            _mod_spec(layer, row0, row_stride),
            _const_spec((1, D_MODEL)),
            _layer_spec((D_MODEL, 2 * D_FF), layer),
            _layer_spec((D_FF, D_MODEL), layer),
            _const_spec((1, D_MODEL)),
        ],
        out_specs=pl.BlockSpec((None, tile, D_MODEL), lambda s, t: (s, t, 0)),
        compiler_params=_params(("arbitrary", "arbitrary")),
        name="ffn",
    )(x, mod, g, w_in, w_out, final_g)


def _rope_lane_order(a):
    lead = a.shape[:-1]
    return a.reshape(*lead, 2, 2, AXIS_DIM // 2).swapaxes(-3, -2).reshape(*lead, HEAD_DIM)


def _rope(a, cos, sin_signed):
    return a * cos + pltpu.roll(a, HEAD_DIM // 2, 1) * sin_signed


def _pair_mean_matrix():
    r = lax.broadcasted_iota(jnp.int32, (2 * HEAD_DIM, 2 * HEAD_DIM), 0) // HEAD_DIM
    c = lax.broadcasted_iota(jnp.int32, (2 * HEAD_DIM, 2 * HEAD_DIM), 1) // HEAD_DIM
    return jnp.where(r == c, 1.0 / HEAD_DIM, 0.0).astype(BF16)


def _rope_lane_matrix():
    src = lax.broadcasted_iota(jnp.int32, (HEAD_DIM, HEAD_DIM), 0)
    n = lax.broadcasted_iota(jnp.int32, (HEAD_DIM, HEAD_DIM), 1)
    quarter = AXIS_DIM // 2
    half, axis, c = n // AXIS_DIM, (n // quarter) % 2, n % quarter
    return jnp.where(src == axis * AXIS_DIM + half * quarter + c, 1.0, 0.0).astype(BF16)


def _qkv_kernel(*refs, rope, cached, keep_f32):
    it = iter(refs)
    x_ref, mod_ref, g_ref, w_ref, qg_ref, kg_ref = (next(it) for _ in range(6))
    rope_refs = [next(it) for _ in range(2)] if rope else None
    cache_refs = [next(it) for _ in range(2)] if cached else None
    q_ref, kh_ref, vt_ref = (next(it) for _ in range(3))
    cache_out_refs = [next(it) for _ in range(2)] if cached else None
    f32_refs = [next(it) for _ in range(2)] if keep_f32 else None

    if cached:
        @pl.when(pl.program_id(1) == 0)
        def _():
            for kv in range(N_KV_HEADS):
                k = cache_refs[0][:, kv, :].astype(BF16)
                if rope:
                    k = _dot(k, _rope_lane_matrix()).astype(BF16)
                cache_out_refs[0][kv] = k
                cache_out_refs[1][kv] = cache_refs[1][:, kv, :].T.astype(BF16)

    mod = mod_ref[...]
    shift, scale = mod[0:1], mod[1:2]
    g = g_ref[...]
    gains = [qg_ref[...] * (SM_SCALE * LOG2E)] * N_HEADS + [kg_ref[...]] * N_KV_HEADS
    pair_mean = _pair_mean_matrix()
    v0 = (N_HEADS + N_KV_HEADS) * HEAD_DIM
    sub_rows = min(QKV_SUB_ROWS, x_ref.shape[0])
    n_sub = x_ref.shape[0] // sub_rows

    def rows_of(sub):
        return slice(sub * sub_rows, (sub + 1) * sub_rows)

    def projection(sub):
        h = _modulate(x_ref[rows_of(sub), :], g, shift, scale).astype(BF16)
        return _dot(h, w_ref[...])

    def finish(sub, p):
        rows = rows_of(sub)
        if rope:
            tables = [r[rows, :] for r in rope_refs]
        for pair in range((N_HEADS + N_KV_HEADS) // 2):
            blk = p[:, pair * 2 * HEAD_DIM:(pair + 1) * 2 * HEAD_DIM]
            ms = _dot((blk * blk).astype(BF16), pair_mean)
            normed = blk * lax.rsqrt(ms + EPS)
            for half in range(2):
                hd = 2 * pair + half
                a = normed[:, half * HEAD_DIM:(half + 1) * HEAD_DIM] * gains[hd]
                if rope:
                    a = _rope(a, *tables)
                if hd < N_HEADS:
                    q_ref[hd, rows, :] = a.astype(BF16)
                else:
                    kv = hd - N_HEADS
                    v = p[:, v0 + kv * HEAD_DIM:v0 + (kv + 1) * HEAD_DIM]
                    kh_ref[kv, rows, :] = a.astype(BF16)
                    vt_ref[kv, :, rows] = v.T.astype(BF16)
                    if keep_f32:
                        f32_refs[0][rows, kv, :] = a
                        f32_refs[1][rows, kv, :] = v

    p_next = projection(0)
    for sub in range(n_sub):
        p_cur = p_next
        if sub + 1 < n_sub:
            p_next = projection(sub + 1)
        finish(sub, p_cur)


def _qkv_proj(x, mod, layer, row0, row_stride, g, w_qkv, q_gain, k_gain, rope_tables, cache,
              cache_layer, keep_f32, tile):
    n_seq, seq_len, _ = x.shape
    rope = rope_tables is not None
    cached = cache is not None
    in_specs = [
        pl.BlockSpec((None, tile, D_MODEL), lambda s, t: (s, t, 0)),
        _mod_spec(layer, row0, row_stride),
        _const_spec((1, D_MODEL)),
        _const_spec((D_MODEL, QKV_WIDTH)),
        _const_spec((1, HEAD_DIM)),
        _const_spec((1, HEAD_DIM)),
    ]
    args = [x, mod, g, w_qkv, q_gain, k_gain]
    if rope:
        in_specs += [pl.BlockSpec((tile, HEAD_DIM), lambda s, t: (t, 0))] * 2
        args += list(rope_tables)
    out_shape = [jax.ShapeDtypeStruct((n_seq, N_HEADS, seq_len, HEAD_DIM), BF16),
                 jax.ShapeDtypeStruct((n_seq, N_KV_HEADS, seq_len, HEAD_DIM), BF16),
                 jax.ShapeDtypeStruct((n_seq, N_KV_HEADS, HEAD_DIM, seq_len), BF16)]
    out_specs = [pl.BlockSpec((None, N_HEADS, tile, HEAD_DIM), lambda s, t: (s, 0, t, 0)),
                 pl.BlockSpec((None, N_KV_HEADS, tile, HEAD_DIM), lambda s, t: (s, 0, t, 0)),
                 pl.BlockSpec((None, N_KV_HEADS, HEAD_DIM, tile), lambda s, t: (s, 0, 0, t))]
    if cached:
        past = cache[0].shape[2]
        in_specs += [pl.BlockSpec((None, None, past, N_KV_HEADS, HEAD_DIM),
                                  lambda s, t: (s, cache_layer, 0, 0, 0))] * 2
        args += list(cache)
        out_shape += [jax.ShapeDtypeStruct((n_seq, N_KV_HEADS, past, HEAD_DIM), BF16),
                      jax.ShapeDtypeStruct((n_seq, N_KV_HEADS, HEAD_DIM, past), BF16)]
        out_specs += [pl.BlockSpec((None, N_KV_HEADS, past, HEAD_DIM), lambda s, t: (s, 0, 0, 0)),
                      pl.BlockSpec((None, N_KV_HEADS, HEAD_DIM, past), lambda s, t: (s, 0, 0, 0))]
    if keep_f32:
        out_shape += [jax.ShapeDtypeStruct((n_seq, 1, seq_len, N_KV_HEADS, HEAD_DIM), F32)] * 2
        out_specs += [pl.BlockSpec((None, None, tile, N_KV_HEADS, HEAD_DIM),
                                   lambda s, t: (s, 0, t, 0, 0))] * 2
    return pl.pallas_call(
        functools.partial(_qkv_kernel, rope=rope, cached=cached, keep_f32=keep_f32),
        out_shape=tuple(out_shape),
        grid=(n_seq, seq_len // tile),
        in_specs=in_specs,
        out_specs=tuple(out_specs),
        compiler_params=_params(("arbitrary", "arbitrary")),
        name="qkv_proj",
    )(*args)


def _attn_kernel(*refs, n_segments, key_chunk):
    q_ref = refs[0]
    segments = [(refs[1 + 2 * i], refs[2 + 2 * i]) for i in range(n_segments)]
    x_ref, mod_ref, w_o_ref, qg_ref, o_ref, heads_scr, shift_scr, safe_scr = refs[1 + 2 * n_segments:]
    group = N_HEADS // N_KV_HEADS
    chunks = [(seg, c0, min(key_chunk, k_ref.shape[1] - c0))
              for seg, (k_ref, _) in enumerate(segments)
              for c0 in range(0, k_ref.shape[1], key_chunk)]

    @pl.when(pl.program_id(1) == 0)
    def _():
        qg = qg_ref[...]
        q_sq = HEAD_DIM * (SM_SCALE * LOG2E) ** 2 * jnp.max(qg * qg)
        for kv in range(N_KV_HEADS):
            k_sq = None
            for k_ref, _ in segments:
                k = k_ref[kv].astype(F32)
                seg_sq = jnp.max(jnp.sum(k * k, axis=1, keepdims=True))
                k_sq = seg_sq if k_sq is None else jnp.maximum(k_sq, seg_sq)
            bound_sq = q_sq * k_sq * BOUND_SLACK
            shift_scr[kv] = jnp.full(shift_scr.shape[1:], jnp.sqrt(bound_sq), F32)
            safe_scr[kv] = (bound_sq <= MAX_SAFE_SHIFT ** 2).astype(jnp.int32)

    def bounded():
        items = [(hd, ci) for hd in range(N_HEADS) for ci in range(len(chunks))]
        scores = {}

        def issue_scores(j):
            hd, ci = items[j]
            seg, c0, rows = chunks[ci]
            keys = segments[seg][0][hd // group, c0:c0 + rows, :]
            scores[j] = _dot_nt(keys, q_ref[hd])

        for j in range(min(ATTN_LOOKAHEAD, len(items))):
            issue_scores(j)
        for i, (hd, ci) in enumerate(items):
            if i + ATTN_LOOKAHEAD < len(items):
                issue_scores(i + ATTN_LOOKAHEAD)
            seg, c0, rows = chunks[ci]
            p = jnp.exp2(scores.pop(i) - shift_scr[hd // group][0:1, 0:1])
            ps = jnp.sum(p, axis=0, keepdims=True)
            pv = _dot(segments[seg][1][hd // group, :, c0:c0 + rows], p.astype(BF16))
            denom, acc = (ps, pv) if ci == 0 else (denom + ps, acc + pv)
            if ci == len(chunks) - 1:
                heads_scr[hd] = (acc * (1.0 / denom)).T.astype(BF16)

    def exact():
        def one_head(hd, carry):
            kv = hd // group
            q = q_ref[hd]
            scores = [_dot_nt(k_ref[kv], q) for k_ref, _ in segments]
            m = None
            for s in scores:
                sm = jnp.max(s, axis=0, keepdims=True)
                m = sm if m is None else jnp.maximum(m, sm)
            denom = o_t = None
            for s, (_, vt_ref) in zip(scores, segments):
                p = jnp.exp2(s - m)
                ps = jnp.sum(p, axis=0, keepdims=True)
                pv = _dot(vt_ref[kv], p.astype(BF16))
                denom, o_t = (ps, pv) if denom is None else (denom + ps, o_t + pv)
            heads_scr[hd] = (o_t * (1.0 / denom)).T.astype(BF16)
            return carry

        lax.fori_loop(0, N_HEADS, one_head, 0)

    safe = safe_scr[0]
    for kv in range(1, N_KV_HEADS):
        safe = jnp.minimum(safe, safe_scr[kv])
    pl.when(safe == 1)(bounded)
    pl.when(safe != 1)(exact)

    attn = jnp.concatenate([heads_scr[hd] for hd in range(N_HEADS)], axis=-1)
    o_ref[...] = x_ref[...] + mod_ref[2:3] * _dot(attn, w_o_ref[...])


def _attn_sublayer(q, key_segments, x, mod, layer, row0, row_stride, w_o, q_gain, tile):
    n_seq, seq_len, _ = x.shape
    in_specs = [pl.BlockSpec((None, N_HEADS, tile, HEAD_DIM), lambda s, t: (s, 0, t, 0))]
    args = [q]
    for k, vt in key_segments:
        n_keys = k.shape[2]
        in_specs += [pl.BlockSpec((None, N_KV_HEADS, n_keys, HEAD_DIM), lambda s, t: (s, 0, 0, 0)),
                     pl.BlockSpec((None, N_KV_HEADS, HEAD_DIM, n_keys), lambda s, t: (s, 0, 0, 0))]
        args += [k, vt]
    in_specs += [
        pl.BlockSpec((None, tile, D_MODEL), lambda s, t: (s, t, 0)),
        _mod_spec(layer, row0, row_stride),
        _const_spec((D_MODEL, D_MODEL)),
        _const_spec((1, HEAD_DIM)),
    ]
    args += [x, mod, w_o, q_gain]
    return pl.pallas_call(
        functools.partial(_attn_kernel, n_segments=len(key_segments), key_chunk=ATTN_KEY_CHUNK),
        out_shape=jax.ShapeDtypeStruct(x.shape, F32),
        grid=(n_seq, seq_len // tile),
        in_specs=in_specs,
        out_specs=pl.BlockSpec((None, tile, D_MODEL), lambda s, t: (s, t, 0)),
        scratch_shapes=[pltpu.VMEM((N_HEADS, tile, HEAD_DIM), BF16),
                        pltpu.VMEM((N_KV_HEADS, 8, HEAD_DIM), F32),
                        pltpu.SMEM((N_KV_HEADS,), jnp.int32)],
        compiler_params=_params(("arbitrary", "arbitrary")),
        name="attention",
    )(*args)


def _rope_tables(seq_len):
    rows = seq_len // GRID_W
    r = np.repeat(np.arange(rows), GRID_W).astype(np.float32)
    col = np.tile(np.arange(GRID_W), rows).astype(np.float32)
    expo = -np.arange(0, AXIS_DIM, 2, dtype=np.float32) / np.float32(AXIS_DIM)
    inv = np.power(np.float32(ROPE_THETA), expo).astype(np.float32)
    ar = r[:, None] * inv
    ac = col[:, None] * inv
    ang = np.concatenate([ar, ar, ac, ac], axis=-1)
    cos, sin = np.cos(ang), np.sin(ang)
    first = (np.arange(HEAD_DIM) % AXIS_DIM) < AXIS_DIM // 2
    return (jnp.asarray(_rope_lane_order(cos), F32),
            jnp.asarray(_rope_lane_order(np.where(first, -sin, sin)), F32))


def _rope_weights(w_qkv, q_gain, k_gain):
    qk = (N_HEADS + N_KV_HEADS) * HEAD_DIM
    lead = w_qkv.shape[:-1]
    w_qk = _rope_lane_order(w_qkv[..., :qk].reshape(*lead, N_HEADS + N_KV_HEADS, HEAD_DIM))
    w = jnp.concatenate([w_qk.reshape(*lead, qk), w_qkv[..., qk:]], axis=-1)
    return w, _rope_lane_order(q_gain), _rope_lane_order(k_gain)


def _run_group(x, mod, row0, row_stride, wts, cache, rope_tables, attn_tile):
    new_kv = []
    for l in range(DEPTH):
        i = l // 2
        g1 = wts["norm_g"][l, 0].reshape(1, D_MODEL)
        g2 = wts["norm_g"][l, 1].reshape(1, D_MODEL)
        if l % 2 == 0:
            x = _even_sublayer(x, mod, l, row0, row_stride, g1, wts["w_in_even"][i],
                               wts["pool_w"][i], wts["pool_scale"][i].reshape(1, POOL_WIDTH),
                               wts["conv_w"][i], wts["conv_b"][i].reshape(1, CONV_WIDTH),
                               wts["w_out_even"][i], min(EVEN_TILE, x.shape[1]))
        else:
            sfx = "" if rope_tables is None else "_rope"
            res = _qkv_proj(x, mod, l, row0, row_stride, g1, wts["w_qkv" + sfx][i],
                            wts["q_gain" + sfx][i].reshape(1, HEAD_DIM),
                            wts["k_gain" + sfx][i].reshape(1, HEAD_DIM), rope_tables, cache, i,
                            cache is None, min(QKV_TILE, x.shape[1]))
            q, segments = res[0], [res[1:3]]
            if cache is None:
                new_kv.append(res[3:])
            else:
                segments.append(res[3:5])
            x = _attn_sublayer(q, segments, x, mod, l, row0, row_stride, wts["w_o"][i],
                               wts["q_gain"][i].reshape(1, HEAD_DIM), attn_tile)
        flat = x.reshape(1, -1, D_MODEL) if row_stride == 0 else x
        flat = _ffn_sublayer(flat, mod, l, row0, row_stride, g2, wts["w_ffn_in"], wts["w_ffn_out"],
                             wts["final_g"].reshape(1, D_MODEL), l == DEPTH - 1, FFN_TILE)
        x = flat.reshape(x.shape)
    return x, new_kv


def kernel(x_prompt, x_sample, cache_k, cache_v, c, c_ctx, w_ada, b_ada, norm_g, w_in_even, pool_w,
           pool_scale, conv_w, conv_b, w_out_even, w_qkv, q_gain, k_gain, w_o, w_ffn_in, w_ffn_out,
           final_g):
    n_ctx, ctx_len, _ = x_prompt.shape
    n_lat, lat_len, _ = x_sample.shape
    assert 1 + n_lat <= COND_ROWS

    cond = jnp.concatenate(
        [c_ctx[None, :], c, jnp.zeros((COND_ROWS - 1 - n_lat, D_MODEL), F32)], axis=0)
    mod = _ada(cond, w_ada, b_ada).reshape(DEPTH, COND_ROWS, 6, D_MODEL)

    wts = dict(
        norm_g=norm_g, final_g=final_g, pool_scale=pool_scale, conv_w=conv_w, conv_b=conv_b,
        q_gain=q_gain, k_gain=k_gain,
        w_in_even=w_in_even.astype(BF16), pool_w=pool_w.astype(BF16),
        w_out_even=w_out_even.astype(BF16), w_qkv=w_qkv.astype(BF16), w_o=w_o.astype(BF16),
        w_ffn_in=w_ffn_in.astype(BF16), w_ffn_out=w_ffn_out.astype(BF16),
    )
    w_rope, qg_rope, kg_rope = _rope_weights(wts["w_qkv"], q_gain, k_gain)
    wts.update(w_qkv_rope=w_rope, q_gain_rope=qg_rope, k_gain_rope=kg_rope)

    y_prompt, ctx_kv = _run_group(x_prompt, mod, 0, 0, wts, None, None, attn_tile=ctx_len)
    new_cache_k = jnp.concatenate([k for k, _ in ctx_kv], axis=1)
    new_cache_v = jnp.concatenate([v for _, v in ctx_kv], axis=1)

    y_sample, _ = _run_group(x_sample, mod, 1, 1, wts, (cache_k, cache_v), _rope_tables(lat_len),
                             attn_tile=ATTN_TILE)
    return y_prompt, y_sample, new_cache_k, new_cache_v
```

```python
import functools
import math

import jax
import jax.numpy as jnp
import numpy as np
from jax import lax
from jax.experimental import pallas as pl
from jax.experimental.pallas import tpu as pltpu

D_MODEL = 1024
DEPTH = 2
GRID_W = 64
EPS = 1e-6
POOL_WIDTH = D_MODEL // 2
POOL_GROUPS = 4
POOL_GROUP_DIM = POOL_WIDTH // POOL_GROUPS
POOL_WINDOWS = (2, 4, 8, 16)
CONV_WIDTH = D_MODEL // 2
EVEN_IN = POOL_WIDTH + 3 * CONV_WIDTH
HEAD_DIM = 128
N_HEADS = D_MODEL // HEAD_DIM
N_KV_HEADS = N_HEADS // 4
KV_WIDTH = N_KV_HEADS * HEAD_DIM
QKV_WIDTH = (N_HEADS + 2 * N_KV_HEADS) * HEAD_DIM
ROPE_THETA = 10000.0
AXIS_DIM = HEAD_DIM // 2
D_FF = ((8 * D_MODEL // 3 + 255) // 256) * 256
SM_SCALE = HEAD_DIM ** -0.5
LOG2E = math.log2(math.e)

COND_ROWS = 8
HALO = 16
ADA_BLOCK_N = 1536
FF_CHUNKS = ((0, 1024), (1024, 1024), (2048, 768))
QKV_TILE = 1024
QKV_SUB_ROWS = 128
EVEN_TILE = 1024
EVEN_SUB_ROWS = 512
FFN_TILE = 1024
FFN_SUB_ROWS = 256
FFN_LOOKAHEAD = 1
ATTN_TILE = 256
ATTN_KEY_CHUNK = 512
ATTN_LOOKAHEAD = 2
BOUND_SLACK = 1.1
MAX_SAFE_SHIFT = 60.0
VMEM_LIMIT_BYTES = 56 * 1024 * 1024

BF16 = jnp.bfloat16
F32 = jnp.float32


def _dot(a, b):
    return jnp.dot(a, b, preferred_element_type=F32)


def _dot_nt(a, b):
    return lax.dot_general(a, b, (((1,), (1,)), ((), ())), preferred_element_type=F32)


def _silu(a):
    return a * (1.0 / (1.0 + jnp.exp(-a)))


def _rms(x):
    return x * lax.rsqrt(jnp.mean(x * x, axis=-1, keepdims=True) + EPS)


def _modulate(x, g, shift, scale):
    return (_rms(x) * g) * (1.0 + scale) + shift


def _params(sem):
    return pltpu.CompilerParams(dimension_semantics=sem, vmem_limit_bytes=VMEM_LIMIT_BYTES)


def _const_spec(shape):
    n = len(shape)
    return pl.BlockSpec(shape, lambda *_: (0,) * n, pipeline_mode=pl.Buffered(1))


def _riding_casts(casts, n_seq, n_t):
    steps = n_seq * n_t
    in_specs, out_specs, out_shapes = [], [], []
    for w, layer in casts:
        rows = w.shape[1] // steps
        in_specs.append(pl.BlockSpec((None, rows, w.shape[2]),
                                     lambda s, t, layer=layer: (layer, s * n_t + t, 0)))
        out_specs.append(pl.BlockSpec((rows, w.shape[2]), lambda s, t: (s * n_t + t, 0)))
        out_shapes.append(jax.ShapeDtypeStruct(w.shape[1:], BF16))
    return in_specs, out_specs, out_shapes


def _can_ride(w, steps):
    return w.shape[1] % steps == 0 and (w.shape[1] // steps) % 16 == 0


def _with_casts(body, n_in, n_out, n_cast):
    def run(*refs):
        ins, cast_ins = refs[:n_in], refs[n_in:n_in + n_cast]
        outs = refs[n_in + n_cast:n_in + n_cast + n_out]
        cast_outs = refs[n_in + n_cast + n_out:n_in + 2 * n_cast + n_out]
        for src, dst in zip(cast_ins, cast_outs):
            dst[...] = src[...].astype(dst.dtype)
        body(*ins, *outs, *refs[n_in + 2 * n_cast + n_out:])
    return run


def _ada_kernel(cond_ref, w_ref, b_ref, o_ref):
    s = _silu(cond_ref[...]).astype(BF16)
    o_ref[...] = _dot(s, w_ref[...].astype(BF16)) + b_ref[...]


def _ada(cond, w_ada, b_ada):
    n_out = w_ada.shape[-1]
    return pl.pallas_call(
        _ada_kernel,
        out_shape=jax.ShapeDtypeStruct((DEPTH, COND_ROWS, n_out), F32),
        grid=(DEPTH, n_out // ADA_BLOCK_N),
        in_specs=[
            pl.BlockSpec((COND_ROWS, D_MODEL), lambda l, j: (0, 0)),
            pl.BlockSpec((None, D_MODEL, ADA_BLOCK_N), lambda l, j: (l, 0, j)),
            pl.BlockSpec((None, 1, ADA_BLOCK_N), lambda l, j: (l, 0, j)),
        ],
        out_specs=pl.BlockSpec((None, COND_ROWS, ADA_BLOCK_N), lambda l, j: (l, 0, j)),
        compiler_params=_params(("arbitrary", "arbitrary")),
        name="ada",
    )(cond, w_ada, b_ada.reshape(DEPTH, 1, n_out))


def _mod_spec(layer, row0, row_stride):
    return pl.BlockSpec((None, None, 6, D_MODEL),
                        lambda s, t: (layer, row0 + row_stride * s, 0, 0))


def _even_kernel(x_ref, xp_ref, xn_ref, mod_ref, g_ref, w_in_ref, pool_w_ref, pool_scale_ref,
                 conv_w_ref, conv_b_ref, w_out_ref, o_ref, *, tile, seq_len):
    t = pl.program_id(1)
    n_t = pl.num_programs(1)
    sub = min(EVEN_SUB_ROWS, tile)
    n_sub = tile // sub
    win = sub + 2 * HALO
    mod = mod_ref[...]
    shift, scale, gate = mod[0:1], mod[1:2], mod[2:3]
    g = g_ref[...]
    cw = conv_w_ref[...]

    def h_rows(lo, hi):
        pieces = []
        if lo < HALO:
            pieces.append(xp_ref[...])
        pieces.append(x_ref[max(lo, HALO) - HALO:min(hi, HALO + tile) - HALO, :])
        if hi > HALO + tile:
            pieces.append(xn_ref[...])
        return jnp.concatenate([_modulate(r, g, shift, scale).astype(BF16) for r in pieces], axis=0)

    def project(s):
        lo = 0 if s == 0 else s * sub + 2 * HALO
        return _dot(h_rows(lo, (s + 1) * sub + 2 * HALO), w_in_ref[...])

    def edge_masked(a, s):
        parts = [a[:HALO], a[HALO:HALO + sub], a[HALO + sub:]]
        if s == 0:
            parts[0] = jnp.where(t > 0, parts[0], 0.0)
        if s == n_sub - 1:
            parts[2] = jnp.where(t < n_t - 1, parts[2], 0.0)
        return jnp.concatenate(parts, axis=0)

    def shifted(a, k):
        return pltpu.roll(a, k % win, 0)

    def mixers(s, p):
        pos = t * tile + s * sub + lax.broadcasted_iota(jnp.int32, (sub, 1), 0)
        u = edge_masked(p[:, :POOL_WIDTH], s)
        ya = []
        for i, w in enumerate(POOL_WINDOWS):
            ug = u[:, i * POOL_GROUP_DIM:(i + 1) * POOL_GROUP_DIM]
            fwd, n = ug, 1
            while n < w // 2:
                fwd = fwd + shifted(fwd, -n)
                n *= 2
            wsum = shifted(fwd, w // 2) + fwd
            cnt = (jnp.clip(pos + w // 2, 0, seq_len)
                   - jnp.clip(pos - w // 2, 0, seq_len)).astype(F32)
            pooled = wsum[HALO:HALO + sub] / cnt - ug[HALO:HALO + sub]
            ya.append(_dot(pooled.astype(BF16), pool_w_ref[i]))
        ya = jnp.concatenate(ya, axis=-1) * pool_scale_ref[...]
        c0 = POOL_WIDTH
        bg = p[HALO:HALO + sub, c0:c0 + CONV_WIDTH]
        z = edge_masked(p[:, c0 + CONV_WIDTH:c0 + 2 * CONV_WIDTH] * p[:, c0 + 2 * CONV_WIDTH:], s)
        conv = (shifted(z, 1) * cw[0:1] + z * cw[1:2] + shifted(z, -1) * cw[2:3])[HALO:HALO + sub]
        yb = bg * (conv + conv_b_ref[...])
        y = _dot(jnp.concatenate([ya, yb], axis=-1).astype(BF16), w_out_ref[...])
        rows = slice(s * sub, (s + 1) * sub)
        o_ref[rows, :] = x_ref[rows, :] + gate * y

    block = project(0)
    window = None
    for s in range(n_sub):
        window = block if s == 0 else jnp.concatenate([window[sub:], block], axis=0)
        if s + 1 < n_sub:
            block = project(s + 1)
        mixers(s, window)


def _even_sublayer(x, mod, layer, row0, row_stride, g, w_in, pool_w, pool_scale, conv_w, conv_b,
                   w_out, tile, casts=()):
    n_seq, seq_len, _ = x.shape
    n_t = seq_len // tile
    hb = tile // HALO
    n_hb = seq_len // HALO
    cast_in, cast_out, cast_shapes = _riding_casts(casts, n_seq, n_t)
    kern = _with_casts(functools.partial(_even_kernel, tile=tile, seq_len=seq_len), 11, 1,
                       len(casts))
    res = pl.pallas_call(
        kern,
        out_shape=(jax.ShapeDtypeStruct(x.shape, F32), *cast_shapes),
        grid=(n_seq, n_t),
        in_specs=[
            pl.BlockSpec((None, tile, D_MODEL), lambda s, t: (s, t, 0)),
            pl.BlockSpec((None, HALO, D_MODEL), lambda s, t: (s, jnp.maximum(t * hb - 1, 0), 0)),
            pl.BlockSpec((None, HALO, D_MODEL),
                         lambda s, t: (s, jnp.minimum((t + 1) * hb, n_hb - 1), 0)),
            _mod_spec(layer, row0, row_stride),
            _const_spec((1, D_MODEL)),
            _const_spec((D_MODEL, EVEN_IN)),
            _const_spec((POOL_GROUPS, POOL_GROUP_DIM, POOL_GROUP_DIM)),
            _const_spec((1, POOL_WIDTH)),
            _const_spec((3, CONV_WIDTH)),
            _const_spec((1, CONV_WIDTH)),
            _const_spec((POOL_WIDTH + CONV_WIDTH, D_MODEL)),
            *cast_in,
        ],
        out_specs=(pl.BlockSpec((None, tile, D_MODEL), lambda s, t: (s, t, 0)), *cast_out),
        compiler_params=_params(("arbitrary", "arbitrary")),
        name="even_mixer",
    )(x, x, x, mod, g, w_in, pool_w, pool_scale, conv_w, conv_b, w_out, *[w for w, _ in casts])
    return res[0], list(res[1:])


def _ffn_kernel(x_ref, mod_ref, g_ref, w_in_ref, w_out_ref, fg_ref, o_ref, *, final_norm):
    mod = mod_ref[...]
    shift, scale, gate = mod[3:4], mod[4:5], mod[5:6]
    g = g_ref[...]
    n_sub = x_ref.shape[0] // FFN_SUB_ROWS
    items = [(sub, c) for sub in range(n_sub) for c in range(len(FF_CHUNKS))]
    hs, ups = {}, {}

    def issue_up(j):
        sub, c = items[j]
        if c == 0:
            rows = x_ref[sub * FFN_SUB_ROWS:(sub + 1) * FFN_SUB_ROWS, :]
            hs[sub] = _modulate(rows, g, shift, scale).astype(BF16)
        c0, cw = FF_CHUNKS[c]
        ups[j] = (_dot(hs[sub], w_in_ref[:, c0:c0 + cw]),
                  _dot(hs[sub], w_in_ref[:, D_FF + c0:D_FF + c0 + cw]))

    for j in range(min(FFN_LOOKAHEAD, len(items))):
        issue_up(j)
    for i, (sub, c) in enumerate(items):
        if i + FFN_LOOKAHEAD < len(items):
            issue_up(i + FFN_LOOKAHEAD)
        a, b = ups.pop(i)
        c0, cw = FF_CHUNKS[c]
        part = _dot((_silu(a) * b).astype(BF16), w_out_ref[c0:c0 + cw, :])
        acc = part if c == 0 else acc + part
        if c == len(FF_CHUNKS) - 1:
            rows = slice(sub * FFN_SUB_ROWS, (sub + 1) * FFN_SUB_ROWS)
            out = x_ref[rows, :] + gate * acc
            if final_norm:
                out = _rms(out) * fg_ref[...]
            o_ref[rows, :] = out


def _ffn_sublayer(x, mod, layer, row0, row_stride, g, w_in, w_out, final_g, final_norm, tile,
                  casts=()):
    n_seq, seq_len, _ = x.shape
    cast_in, cast_out, cast_shapes = _riding_casts(casts, n_seq, seq_len // tile)
    kern = _with_casts(functools.partial(_ffn_kernel, final_norm=final_norm), 6, 1, len(casts))
    res = pl.pallas_call(
        kern,
        out_shape=(jax.ShapeDtypeStruct(x.shape, F32), *cast_shapes),
        grid=(n_seq, seq_len // tile),
        in_specs=[
            pl.BlockSpec((None, tile, D_MODEL), lambda s, t: (s, t, 0)),
            _mod_spec(layer, row0, row_stride),
            _const_spec((1, D_MODEL)),
            _const_spec((D_MODEL, 2 * D_FF)),
            _const_spec((D_FF, D_MODEL)),
            _const_spec((1, D_MODEL)),
            *cast_in,
        ],
        out_specs=(pl.BlockSpec((None, tile, D_MODEL), lambda s, t: (s, t, 0)), *cast_out),
        compiler_params=_params(("arbitrary", "arbitrary")),
        name="ffn",
    )(x, mod, g, w_in, w_out, final_g, *[w for w, _ in casts])
    return res[0], list(res[1:])


def _rope_lane_order(a):
    lead = a.shape[:-1]
    return a.reshape(*lead, 2, 2, AXIS_DIM // 2).swapaxes(-3, -2).reshape(*lead, HEAD_DIM)


def _rope(a, cos, sin_signed):
    return a * cos + pltpu.roll(a, HEAD_DIM // 2, 1) * sin_signed


def _pair_mean_matrix():
    r = lax.broadcasted_iota(jnp.int32, (2 * HEAD_DIM, 2 * HEAD_DIM), 0) // HEAD_DIM
    c = lax.broadcasted_iota(jnp.int32, (2 * HEAD_DIM, 2 * HEAD_DIM), 1) // HEAD_DIM
    return jnp.where(r == c, 1.0 / HEAD_DIM, 0.0).astype(BF16)


def _rope_lane_matrix():
    src = lax.broadcasted_iota(jnp.int32, (HEAD_DIM, HEAD_DIM), 0)
    n = lax.broadcasted_iota(jnp.int32, (HEAD_DIM, HEAD_DIM), 1)
    quarter = AXIS_DIM // 2
    half, axis, c = n // AXIS_DIM, (n // quarter) % 2, n % quarter
    return jnp.where(src == axis * AXIS_DIM + half * quarter + c, 1.0, 0.0).astype(BF16)


def _qkv_kernel(*refs, rope, cached, keep_f32):
    it = iter(refs)
    x_ref, mod_ref, g_ref, w_ref, qg_ref, kg_ref = (next(it) for _ in range(6))
    rope_refs = [next(it) for _ in range(2)] if rope else None
    cache_refs = [next(it) for _ in range(2)] if cached else None
    q_ref, kh_ref, vt_ref = (next(it) for _ in range(3))
    cache_out_refs = [next(it) for _ in range(2)] if cached else None
    f32_refs = [next(it) for _ in range(2)] if keep_f32 else None

    if cached:
        @pl.when(pl.program_id(1) == 0)
        def _():
            for kv in range(N_KV_HEADS):
                k = cache_refs[0][:, kv, :].astype(BF16)
                if rope:
                    k = _dot(k, _rope_lane_matrix()).astype(BF16)
                cache_out_refs[0][kv] = k
                cache_out_refs[1][kv] = cache_refs[1][:, kv, :].T.astype(BF16)

    mod = mod_ref[...]
    shift, scale = mod[0:1], mod[1:2]
    g = g_ref[...]
    gains = [qg_ref[...] * (SM_SCALE * LOG2E)] * N_HEADS + [kg_ref[...]] * N_KV_HEADS
    pair_mean = _pair_mean_matrix()
    v0 = (N_HEADS + N_KV_HEADS) * HEAD_DIM
    sub_rows = min(QKV_SUB_ROWS, x_ref.shape[0])
    n_sub = x_ref.shape[0] // sub_rows

    def rows_of(sub):
        return slice(sub * sub_rows, (sub + 1) * sub_rows)

    def projection(sub):
        h = _modulate(x_ref[rows_of(sub), :], g, shift, scale).astype(BF16)
        return _dot(h, w_ref[...])

    def finish(sub, p):
        rows = rows_of(sub)
        if rope:
            tables = [r[rows, :] for r in rope_refs]
        for pair in range((N_HEADS + N_KV_HEADS) // 2):
            blk = p[:, pair * 2 * HEAD_DIM:(pair + 1) * 2 * HEAD_DIM]
            ms = _dot((blk * blk).astype(BF16), pair_mean)
            normed = blk * lax.rsqrt(ms + EPS)
            for half in range(2):
                hd = 2 * pair + half
                a = normed[:, half * HEAD_DIM:(half + 1) * HEAD_DIM] * gains[hd]
                if rope:
                    a = _rope(a, *tables)
                if hd < N_HEADS:
                    q_ref[hd, rows, :] = a.astype(BF16)
                else:
                    kv = hd - N_HEADS
                    v = p[:, v0 + kv * HEAD_DIM:v0 + (kv + 1) * HEAD_DIM]
                    kh_ref[kv, rows, :] = a.astype(BF16)
                    vt_ref[kv, :, rows] = v.T.astype(BF16)
                    if keep_f32:
                        f32_refs[0][rows, kv, :] = a
                        f32_refs[1][rows, kv, :] = v

    p_next = projection(0)
    for sub in range(n_sub):
        p_cur = p_next
        if sub + 1 < n_sub:
            p_next = projection(sub + 1)
        finish(sub, p_cur)


def _qkv_proj(x, mod, layer, row0, row_stride, g, w_qkv, q_gain, k_gain, rope_tables, cache,
              cache_layer, keep_f32, tile):
    n_seq, seq_len, _ = x.shape
    rope = rope_tables is not None
    cached = cache is not None
    in_specs = [
        pl.BlockSpec((None, tile, D_MODEL), lambda s, t: (s, t, 0)),
        _mod_spec(layer, row0, row_stride),
        _const_spec((1, D_MODEL)),
        _const_spec((D_MODEL, QKV_WIDTH)),
        _const_spec((1, HEAD_DIM)),
        _const_spec((1, HEAD_DIM)),
    ]
    args = [x, mod, g, w_qkv, q_gain, k_gain]
    if rope:
        in_specs += [pl.BlockSpec((tile, HEAD_DIM), lambda s, t: (t, 0))] * 2
        args += list(rope_tables)
    out_shape = [jax.ShapeDtypeStruct((n_seq, N_HEADS, seq_len, HEAD_DIM), BF16),
                 jax.ShapeDtypeStruct((n_seq, N_KV_HEADS, seq_len, HEAD_DIM), BF16),
                 jax.ShapeDtypeStruct((n_seq, N_KV_HEADS, HEAD_DIM, seq_len), BF16)]
    out_specs = [pl.BlockSpec((None, N_HEADS, tile, HEAD_DIM), lambda s, t: (s, 0, t, 0)),
                 pl.BlockSpec((None, N_KV_HEADS, tile, HEAD_DIM), lambda s, t: (s, 0, t, 0)),
                 pl.BlockSpec((None, N_KV_HEADS, HEAD_DIM, tile), lambda s, t: (s, 0, 0, t))]
    if cached:
        past = cache[0].shape[2]
        in_specs += [pl.BlockSpec((None, None, past, N_KV_HEADS, HEAD_DIM),
                                  lambda s, t: (s, cache_layer, 0, 0, 0))] * 2
        args += list(cache)
        out_shape += [jax.ShapeDtypeStruct((n_seq, N_KV_HEADS, past, HEAD_DIM), BF16),
                      jax.ShapeDtypeStruct((n_seq, N_KV_HEADS, HEAD_DIM, past), BF16)]
        out_specs += [pl.BlockSpec((None, N_KV_HEADS, past, HEAD_DIM), lambda s, t: (s, 0, 0, 0)),
                      pl.BlockSpec((None, N_KV_HEADS, HEAD_DIM, past), lambda s, t: (s, 0, 0, 0))]
    if keep_f32:
        out_shape += [jax.ShapeDtypeStruct((n_seq, 1, seq_len, N_KV_HEADS, HEAD_DIM), F32)] * 2
        out_specs += [pl.BlockSpec((None, None, tile, N_KV_HEADS, HEAD_DIM),
                                   lambda s, t: (s, 0, t, 0, 0))] * 2
    return pl.pallas_call(
        functools.partial(_qkv_kernel, rope=rope, cached=cached, keep_f32=keep_f32),
        out_shape=tuple(out_shape),
        grid=(n_seq, seq_len // tile),
        in_specs=in_specs,
        out_specs=tuple(out_specs),
        compiler_params=_params(("arbitrary", "arbitrary")),
        name="qkv_proj",
    )(*args)


def _attn_kernel(*refs, n_segments, key_chunk):
    q_ref = refs[0]
    segments = [(refs[1 + 2 * i], refs[2 + 2 * i]) for i in range(n_segments)]
    x_ref, mod_ref, w_o_ref, qg_ref, o_ref, heads_scr, shift_scr, safe_scr = refs[1 + 2 * n_segments:]
    group = N_HEADS // N_KV_HEADS
    chunks = [(seg, c0, min(key_chunk, k_ref.shape[1] - c0))
              for seg, (k_ref, _) in enumerate(segments)
              for c0 in range(0, k_ref.shape[1], key_chunk)]

    @pl.when(pl.program_id(1) == 0)
    def _():
        qg = qg_ref[...]
        q_sq = HEAD_DIM * (SM_SCALE * LOG2E) ** 2 * jnp.max(qg * qg)
        for kv in range(N_KV_HEADS):
            k_sq = None
            for k_ref, _ in segments:
                k = k_ref[kv].astype(F32)
                seg_sq = jnp.max(jnp.sum(k * k, axis=1, keepdims=True))
                k_sq = seg_sq if k_sq is None else jnp.maximum(k_sq, seg_sq)
            bound_sq = q_sq * k_sq * BOUND_SLACK
            shift_scr[kv] = jnp.full(shift_scr.shape[1:], jnp.sqrt(bound_sq), F32)
            safe_scr[kv] = (bound_sq <= MAX_SAFE_SHIFT ** 2).astype(jnp.int32)

    def bounded():
        items = [(hd, ci) for hd in range(N_HEADS) for ci in range(len(chunks))]
        scores = {}

        def issue_scores(j):
            hd, ci = items[j]
            seg, c0, rows = chunks[ci]
            keys = segments[seg][0][hd // group, c0:c0 + rows, :]
            scores[j] = _dot_nt(keys, q_ref[hd])

        for j in range(min(ATTN_LOOKAHEAD, len(items))):
            issue_scores(j)
        for i, (hd, ci) in enumerate(items):
            if i + ATTN_LOOKAHEAD < len(items):
                issue_scores(i + ATTN_LOOKAHEAD)
            seg, c0, rows = chunks[ci]
            p = jnp.exp2(scores.pop(i) - shift_scr[hd // group][0:1, 0:1])
            ps = jnp.sum(p, axis=0, keepdims=True)
            pv = _dot(segments[seg][1][hd // group, :, c0:c0 + rows], p.astype(BF16))
            denom, acc = (ps, pv) if ci == 0 else (denom + ps, acc + pv)
            if ci == len(chunks) - 1:
                heads_scr[hd] = (acc * (1.0 / denom)).T.astype(BF16)

    def exact():
        def one_head(hd, carry):
            kv = hd // group
            q = q_ref[hd]
            scores = [_dot_nt(k_ref[kv], q) for k_ref, _ in segments]
            m = None
            for s in scores:
                sm = jnp.max(s, axis=0, keepdims=True)
                m = sm if m is None else jnp.maximum(m, sm)
            denom = o_t = None
            for s, (_, vt_ref) in zip(scores, segments):
                p = jnp.exp2(s - m)
                ps = jnp.sum(p, axis=0, keepdims=True)
                pv = _dot(vt_ref[kv], p.astype(BF16))
                denom, o_t = (ps, pv) if denom is None else (denom + ps, o_t + pv)
            heads_scr[hd] = (o_t * (1.0 / denom)).T.astype(BF16)
            return carry

        lax.fori_loop(0, N_HEADS, one_head, 0)

    safe = safe_scr[0]
    for kv in range(1, N_KV_HEADS):
        safe = jnp.minimum(safe, safe_scr[kv])
    pl.when(safe == 1)(bounded)
    pl.when(safe != 1)(exact)

    attn = jnp.concatenate([heads_scr[hd] for hd in range(N_HEADS)], axis=-1)
    o_ref[...] = x_ref[...] + mod_ref[2:3] * _dot(attn, w_o_ref[...])


def _attn_sublayer(q, key_segments, x, mod, layer, row0, row_stride, w_o, q_gain, tile):
    n_seq, seq_len, _ = x.shape
    in_specs = [pl.BlockSpec((None, N_HEADS, tile, HEAD_DIM), lambda s, t: (s, 0, t, 0))]
    args = [q]
    for k, vt in key_segments:
        n_keys = k.shape[2]
        in_specs += [pl.BlockSpec((None, N_KV_HEADS, n_keys, HEAD_DIM), lambda s, t: (s, 0, 0, 0)),
                     pl.BlockSpec((None, N_KV_HEADS, HEAD_DIM, n_keys), lambda s, t: (s, 0, 0, 0))]
        args += [k, vt]
    in_specs += [
        pl.BlockSpec((None, tile, D_MODEL), lambda s, t: (s, t, 0)),
        _mod_spec(layer, row0, row_stride),
        _const_spec((D_MODEL, D_MODEL)),
        _const_spec((1, HEAD_DIM)),
    ]
    args += [x, mod, w_o, q_gain]
    return pl.pallas_call(
        functools.partial(_attn_kernel, n_segments=len(key_segments), key_chunk=ATTN_KEY_CHUNK),
        out_shape=jax.ShapeDtypeStruct(x.shape, F32),
        grid=(n_seq, seq_len // tile),
        in_specs=in_specs,
        out_specs=pl.BlockSpec((None, tile, D_MODEL), lambda s, t: (s, t, 0)),
        scratch_shapes=[pltpu.VMEM((N_HEADS, tile, HEAD_DIM), BF16),
                        pltpu.VMEM((N_KV_HEADS, 8, HEAD_DIM), F32),
                        pltpu.SMEM((N_KV_HEADS,), jnp.int32)],
        compiler_params=_params(("arbitrary", "arbitrary")),
        name="attention",
    )(*args)


def _rope_tables(seq_len):
    rows = seq_len // GRID_W
    r = np.repeat(np.arange(rows), GRID_W).astype(np.float32)
    col = np.tile(np.arange(GRID_W), rows).astype(np.float32)
    expo = -np.arange(0, AXIS_DIM, 2, dtype=np.float32) / np.float32(AXIS_DIM)
    inv = np.power(np.float32(ROPE_THETA), expo).astype(np.float32)
    ar = r[:, None] * inv
    ac = col[:, None] * inv
    ang = np.concatenate([ar, ar, ac, ac], axis=-1)
    cos, sin = np.cos(ang), np.sin(ang)
    first = (np.arange(HEAD_DIM) % AXIS_DIM) < AXIS_DIM // 2
    return (jnp.asarray(_rope_lane_order(cos), F32),
            jnp.asarray(_rope_lane_order(np.where(first, -sin, sin)), F32))


def _rope_weights(w_qkv, q_gain, k_gain):
    qk = (N_HEADS + N_KV_HEADS) * HEAD_DIM
    lead = w_qkv.shape[:-1]
    w_qk = _rope_lane_order(w_qkv[..., :qk].reshape(*lead, N_HEADS + N_KV_HEADS, HEAD_DIM))
    w = jnp.concatenate([w_qk.reshape(*lead, qk), w_qkv[..., qk:]], axis=-1)
    return w, _rope_lane_order(q_gain), _rope_lane_order(k_gain)


def _run_layer(x, l, mod, grp, wts, ffn_w, ffn_f32, ride):
    row0, row_stride = grp["row0"], grp["row_stride"]
    i = l // 2
    g1 = wts["norm_g"][l, 0].reshape(1, D_MODEL)
    g2 = wts["norm_g"][l, 1].reshape(1, D_MODEL)
    new_kv = None
    if l % 2 == 0:
        tile = min(EVEN_TILE, x.shape[1])
        casts = []
        if l == 0 and 0 not in ffn_w:
            steps = x.shape[0] * (x.shape[1] // tile)
            if ride and all(_can_ride(w, steps) for w in ffn_f32):
                casts = [(w, 0) for w in ffn_f32]
            else:
                ffn_w[0] = tuple(w[0].astype(BF16) for w in ffn_f32)
        x, cast_out = _even_sublayer(x, mod, l, row0, row_stride, g1, wts["w_in_even"][i],
                                     wts["pool_w"][i], wts["pool_scale"][i].reshape(1, POOL_WIDTH),
                                     wts["conv_w"][i], wts["conv_b"][i].reshape(1, CONV_WIDTH),
                                     wts["w_out_even"][i], tile, casts)
        if casts:
            ffn_w[0] = tuple(cast_out)
    else:
        cache, rope_tables = grp["cache"], grp["rope_tables"]
        sfx = "" if rope_tables is None else "_rope"
        res = _qkv_proj(x, mod, l, row0, row_stride, g1, wts["w_qkv" + sfx][i],
                        wts["q_gain" + sfx][i].reshape(1, HEAD_DIM),
                        wts["k_gain" + sfx][i].reshape(1, HEAD_DIM), rope_tables, cache, i,
                        cache is None, min(QKV_TILE, x.shape[1]))
        q, segments = res[0], [res[1:3]]
        if cache is None:
            new_kv = res[3:]
        else:
            segments.append(res[3:5])
        x = _attn_sublayer(q, segments, x, mod, l, row0, row_stride, wts["w_o"][i],
                           wts["q_gain"][i].reshape(1, HEAD_DIM), grp["attn_tile"])
    flat = x.reshape(1, -1, D_MODEL) if row_stride == 0 else x
    tile = min(FFN_TILE, flat.shape[1])
    casts = []
    if l + 1 < DEPTH and l + 1 not in ffn_w:
        steps = flat.shape[0] * (flat.shape[1] // tile)
        if ride and all(_can_ride(w, steps) for w in ffn_f32):
            casts = [(w, l + 1) for w in ffn_f32]
        else:
            ffn_w[l + 1] = tuple(w[l + 1].astype(BF16) for w in ffn_f32)
    flat, cast_out = _ffn_sublayer(flat, mod, l, row0, row_stride, g2, *ffn_w[l],
                                   wts["final_g"].reshape(1, D_MODEL), l == DEPTH - 1, tile, casts)
    if casts:
        ffn_w[l + 1] = tuple(cast_out)
    return flat.reshape(x.shape), new_kv


def kernel(x_prompt, x_sample, cache_k, cache_v, c, c_ctx, w_ada, b_ada, norm_g, w_in_even, pool_w,
           pool_scale, conv_w, conv_b, w_out_even, w_qkv, q_gain, k_gain, w_o, w_ffn_in, w_ffn_out,
           final_g):
    n_ctx, ctx_len, _ = x_prompt.shape
    n_lat, lat_len, _ = x_sample.shape
    assert 1 + n_lat <= COND_ROWS

    cond = jnp.concatenate(
        [c_ctx[None, :], c, jnp.zeros((COND_ROWS - 1 - n_lat, D_MODEL), F32)], axis=0)
    mod = _ada(cond, w_ada, b_ada).reshape(DEPTH, COND_ROWS, 6, D_MODEL)

    wts = dict(
        norm_g=norm_g, final_g=final_g, pool_scale=pool_scale, conv_w=conv_w, conv_b=conv_b,
        q_gain=q_gain, k_gain=k_gain,
        w_in_even=w_in_even.astype(BF16), pool_w=pool_w.astype(BF16),
        w_out_even=w_out_even.astype(BF16), w_qkv=w_qkv.astype(BF16), w_o=w_o.astype(BF16),
    )
    w_rope, qg_rope, kg_rope = _rope_weights(wts["w_qkv"], q_gain, k_gain)
    wts.update(w_qkv_rope=w_rope, q_gain_rope=qg_rope, k_gain_rope=kg_rope)

    ctx_grp = dict(row0=0, row_stride=0, cache=None, rope_tables=None, attn_tile=ctx_len)
    lat_grp = dict(row0=1, row_stride=1, cache=(cache_k, cache_v), rope_tables=_rope_tables(lat_len),
                   attn_tile=ATTN_TILE)
    ffn_w, ffn_f32 = {}, (w_ffn_in, w_ffn_out)
    y_prompt, y_sample, ctx_kv = x_prompt, x_sample, []
    for l in range(DEPTH):
        y_sample, _ = _run_layer(y_sample, l, mod, lat_grp, wts, ffn_w, ffn_f32, ride=True)
        y_prompt, kv = _run_layer(y_prompt, l, mod, ctx_grp, wts, ffn_w, ffn_f32, ride=False)
        if kv is not None:
            ctx_kv.append(kv)
    new_cache_k = jnp.concatenate([k for k, _ in ctx_kv], axis=1)
    new_cache_v = jnp.concatenate([v for _, v in ctx_kv], axis=1)
    return y_prompt, y_sample, new_cache_k, new_cache_v
```

```python
import functools
import math

import jax
import jax.numpy as jnp
import numpy as np
from jax import lax
from jax.experimental import pallas as pl
from jax.experimental.pallas import tpu as pltpu

D_MODEL = 1024
DEPTH = 2
GRID_W = 64
EPS = 1e-6
POOL_WIDTH = D_MODEL // 2
POOL_GROUPS = 4
POOL_GROUP_DIM = POOL_WIDTH // POOL_GROUPS
POOL_WINDOWS = (2, 4, 8, 16)
CONV_WIDTH = D_MODEL // 2
EVEN_IN = POOL_WIDTH + 3 * CONV_WIDTH
HEAD_DIM = 128
N_HEADS = D_MODEL // HEAD_DIM
N_KV_HEADS = N_HEADS // 4
KV_WIDTH = N_KV_HEADS * HEAD_DIM
QKV_WIDTH = (N_HEADS + 2 * N_KV_HEADS) * HEAD_DIM
ROPE_THETA = 10000.0
AXIS_DIM = HEAD_DIM // 2
D_FF = ((8 * D_MODEL // 3 + 255) // 256) * 256
SM_SCALE = HEAD_DIM ** -0.5
LOG2E = math.log2(math.e)

COND_ROWS = 8
HALO = 16
ADA_BLOCK_N = 1536
FF_CHUNKS = ((0, 1024), (1024, 1024), (2048, 768))
CTX_PACK_ROWS = 1024
QKV_TILE = 1024
QKV_SUB_ROWS = 128
EVEN_TILE = 1024
EVEN_SUB_ROWS = 512
FFN_TILE = 1024
FFN_SUB_ROWS = 256
FFN_LOOKAHEAD = 1
ATTN_TILE = 256
ATTN_KEY_CHUNK = 512
ATTN_LOOKAHEAD = 2
BOUND_SLACK = 1.1
MAX_SAFE_SHIFT = 60.0
VMEM_LIMIT_BYTES = 56 * 1024 * 1024

BF16 = jnp.bfloat16
F32 = jnp.float32


def _dot(a, b):
    return jnp.dot(a, b, preferred_element_type=F32)


def _dot_nt(a, b):
    return lax.dot_general(a, b, (((1,), (1,)), ((), ())), preferred_element_type=F32)


def _silu(a):
    return a * (1.0 / (1.0 + jnp.exp(-a)))


def _rms(x):
    return x * lax.rsqrt(jnp.mean(x * x, axis=-1, keepdims=True) + EPS)


def _modulate(x, g, shift, scale):
    return (_rms(x) * g) * (1.0 + scale) + shift


def _params(sem):
    return pltpu.CompilerParams(dimension_semantics=sem, vmem_limit_bytes=VMEM_LIMIT_BYTES)


def _const_spec(shape):
    n = len(shape)
    return pl.BlockSpec(shape, lambda *_: (0,) * n, pipeline_mode=pl.Buffered(1))


def _riding_casts(casts, n_seq, n_t):
    steps = n_seq * n_t
    in_specs, out_specs, out_shapes = [], [], []
    for w, layer in casts:
        rows = w.shape[1] // steps
        in_specs.append(pl.BlockSpec((None, rows, w.shape[2]),
                                     lambda s, t, layer=layer: (layer, s * n_t + t, 0)))
        out_specs.append(pl.BlockSpec((rows, w.shape[2]), lambda s, t: (s * n_t + t, 0)))
        out_shapes.append(jax.ShapeDtypeStruct(w.shape[1:], BF16))
    return in_specs, out_specs, out_shapes


def _can_ride(w, steps):
    return w.shape[1] % steps == 0 and (w.shape[1] // steps) % 16 == 0


def _with_casts(body, n_in, n_out, n_cast):
    def run(*refs):
        ins, cast_ins = refs[:n_in], refs[n_in:n_in + n_cast]
        outs = refs[n_in + n_cast:n_in + n_cast + n_out]
        cast_outs = refs[n_in + n_cast + n_out:n_in + 2 * n_cast + n_out]
        for src, dst in zip(cast_ins, cast_outs):
            dst[...] = src[...].astype(dst.dtype)
        body(*ins, *outs, *refs[n_in + 2 * n_cast + n_out:])
    return run


def _ada_kernel(cond_ref, w_ref, b_ref, o_ref):
    s = _silu(cond_ref[...]).astype(BF16)
    o_ref[...] = _dot(s, w_ref[...].astype(BF16)) + b_ref[...]


def _ada(cond, w_ada, b_ada):
    n_out = w_ada.shape[-1]
    return pl.pallas_call(
        _ada_kernel,
        out_shape=jax.ShapeDtypeStruct((DEPTH, COND_ROWS, n_out), F32),
        grid=(DEPTH, n_out // ADA_BLOCK_N),
        in_specs=[
            pl.BlockSpec((COND_ROWS, D_MODEL), lambda l, j: (0, 0)),
            pl.BlockSpec((None, D_MODEL, ADA_BLOCK_N), lambda l, j: (l, 0, j)),
            pl.BlockSpec((None, 1, ADA_BLOCK_N), lambda l, j: (l, 0, j)),
        ],
        out_specs=pl.BlockSpec((None, COND_ROWS, ADA_BLOCK_N), lambda l, j: (l, 0, j)),
        compiler_params=_params(("arbitrary", "arbitrary")),
        name="ada",
    )(cond, w_ada, b_ada.reshape(DEPTH, 1, n_out))


def _mod_spec(layer, row0, row_stride):
    return pl.BlockSpec((None, None, 6, D_MODEL),
                        lambda s, t: (layer, row0 + row_stride * s, 0, 0))


def _even_kernel(x_ref, xp_ref, xn_ref, mod_ref, g_ref, w_in_ref, pool_w_ref, pool_scale_ref,
                 conv_w_ref, conv_b_ref, w_out_ref, o_ref, *, tile, seq_len, packed):
    t = pl.program_id(1)
    n_t = pl.num_programs(1)
    sub = seq_len if packed else min(EVEN_SUB_ROWS, tile)
    n_sub = tile // sub
    win = sub + 2 * HALO
    mod = mod_ref[...]
    shift, scale, gate = mod[0:1], mod[1:2], mod[2:3]
    g = g_ref[...]
    cw = conv_w_ref[...]

    def h_rows(lo, hi):
        pieces = []
        if lo < HALO:
            pieces.append(xp_ref[...])
        pieces.append(x_ref[max(lo, HALO) - HALO:min(hi, HALO + tile) - HALO, :])
        if hi > HALO + tile:
            pieces.append(xn_ref[...])
        return jnp.concatenate([_modulate(r, g, shift, scale).astype(BF16) for r in pieces], axis=0)

    def project(s):
        if packed:
            rows = x_ref[s * sub:(s + 1) * sub, :]
            return _dot(_modulate(rows, g, shift, scale).astype(BF16), w_in_ref[...])
        lo = 0 if s == 0 else s * sub + 2 * HALO
        return _dot(h_rows(lo, (s + 1) * sub + 2 * HALO), w_in_ref[...])

    def edge_masked(a, s):
        if packed:
            return a
        parts = [a[:HALO], a[HALO:HALO + sub], a[HALO + sub:]]
        if s == 0:
            parts[0] = jnp.where(t > 0, parts[0], 0.0)
        if s == n_sub - 1:
            parts[2] = jnp.where(t < n_t - 1, parts[2], 0.0)
        return jnp.concatenate(parts, axis=0)

    def shifted(a, k):
        return pltpu.roll(a, k % win, 0)

    def mixers(s, p):
        first = 0 if packed else t * tile + s * sub
        pos = first + lax.broadcasted_iota(jnp.int32, (sub, 1), 0)
        u = edge_masked(p[:, :POOL_WIDTH], s)
        ya = []
        for i, w in enumerate(POOL_WINDOWS):
            ug = u[:, i * POOL_GROUP_DIM:(i + 1) * POOL_GROUP_DIM]
            fwd, n = ug, 1
            while n < w // 2:
                fwd = fwd + shifted(fwd, -n)
                n *= 2
            wsum = shifted(fwd, w // 2) + fwd
            cnt = (jnp.clip(pos + w // 2, 0, seq_len)
                   - jnp.clip(pos - w // 2, 0, seq_len)).astype(F32)
            pooled = wsum[HALO:HALO + sub] / cnt - ug[HALO:HALO + sub]
            ya.append(_dot(pooled.astype(BF16), pool_w_ref[i]))
        ya = jnp.concatenate(ya, axis=-1) * pool_scale_ref[...]
        c0 = POOL_WIDTH
        bg = p[HALO:HALO + sub, c0:c0 + CONV_WIDTH]
        z = edge_masked(p[:, c0 + CONV_WIDTH:c0 + 2 * CONV_WIDTH] * p[:, c0 + 2 * CONV_WIDTH:], s)
        conv = (shifted(z, 1) * cw[0:1] + z * cw[1:2] + shifted(z, -1) * cw[2:3])[HALO:HALO + sub]
        yb = bg * (conv + conv_b_ref[...])
        y = _dot(jnp.concatenate([ya, yb], axis=-1).astype(BF16), w_out_ref[...])
        rows = slice(s * sub, (s + 1) * sub)
        o_ref[rows, :] = x_ref[rows, :] + gate * y

    block = project(0)
    window = None
    margin = jnp.zeros((HALO, EVEN_IN), F32)
    for s in range(n_sub):
        if packed:
            window = jnp.concatenate([margin, block, margin], axis=0)
        else:
            window = block if s == 0 else jnp.concatenate([window[sub:], block], axis=0)
        if s + 1 < n_sub:
            block = project(s + 1)
        mixers(s, window)


def _even_sublayer(x, mod, layer, row0, row_stride, g, w_in, pool_w, pool_scale, conv_w, conv_b,
                   w_out, tile, inner=None, casts=()):
    n_seq, n_rows, _ = x.shape
    packed = inner is not None and inner < n_rows
    assert not packed or tile % inner == 0
    n_t = n_rows // tile
    hb = tile // HALO
    n_hb = n_rows // HALO
    cast_in, cast_out, cast_shapes = _riding_casts(casts, n_seq, n_t)
    kern = _with_casts(functools.partial(_even_kernel, tile=tile, packed=packed,
                                         seq_len=inner if packed else n_rows), 11, 1, len(casts))
    res = pl.pallas_call(
        kern,
        out_shape=(jax.ShapeDtypeStruct(x.shape, F32), *cast_shapes),
        grid=(n_seq, n_t),
        in_specs=[
            pl.BlockSpec((None, tile, D_MODEL), lambda s, t: (s, t, 0)),
            pl.BlockSpec((None, HALO, D_MODEL), lambda s, t: (s, jnp.maximum(t * hb - 1, 0), 0)),
            pl.BlockSpec((None, HALO, D_MODEL),
                         lambda s, t: (s, jnp.minimum((t + 1) * hb, n_hb - 1), 0)),
            _mod_spec(layer, row0, row_stride),
            _const_spec((1, D_MODEL)),
            _const_spec((D_MODEL, EVEN_IN)),
            _const_spec((POOL_GROUPS, POOL_GROUP_DIM, POOL_GROUP_DIM)),
            _const_spec((1, POOL_WIDTH)),
            _const_spec((3, CONV_WIDTH)),
            _const_spec((1, CONV_WIDTH)),
            _const_spec((POOL_WIDTH + CONV_WIDTH, D_MODEL)),
            *cast_in,
        ],
        out_specs=(pl.BlockSpec((None, tile, D_MODEL), lambda s, t: (s, t, 0)), *cast_out),
        compiler_params=_params(("arbitrary", "arbitrary")),
        name="even_mixer",
    )(x, x, x, mod, g, w_in, pool_w, pool_scale, conv_w, conv_b, w_out, *[w for w, _ in casts])
    return res[0], list(res[1:])


def _ffn_kernel(x_ref, mod_ref, g_ref, w_in_ref, w_out_ref, fg_ref, o_ref, *, final_norm):
    mod = mod_ref[...]
    shift, scale, gate = mod[3:4], mod[4:5], mod[5:6]
    g = g_ref[...]
    n_sub = x_ref.shape[0] // FFN_SUB_ROWS
    items = [(sub, c) for sub in range(n_sub) for c in range(len(FF_CHUNKS))]
    hs, ups = {}, {}

    def issue_up(j):
        sub, c = items[j]
        if c == 0:
            rows = x_ref[sub * FFN_SUB_ROWS:(sub + 1) * FFN_SUB_ROWS, :]
            hs[sub] = _modulate(rows, g, shift, scale).astype(BF16)
        c0, cw = FF_CHUNKS[c]
        ups[j] = (_dot(hs[sub], w_in_ref[:, c0:c0 + cw]),
                  _dot(hs[sub], w_in_ref[:, D_FF + c0:D_FF + c0 + cw]))

    for j in range(min(FFN_LOOKAHEAD, len(items))):
        issue_up(j)
    for i, (sub, c) in enumerate(items):
        if i + FFN_LOOKAHEAD < len(items):
            issue_up(i + FFN_LOOKAHEAD)
        a, b = ups.pop(i)
        c0, cw = FF_CHUNKS[c]
        part = _dot((_silu(a) * b).astype(BF16), w_out_ref[c0:c0 + cw, :])
        acc = part if c == 0 else acc + part
        if c == len(FF_CHUNKS) - 1:
            rows = slice(sub * FFN_SUB_ROWS, (sub + 1) * FFN_SUB_ROWS)
            out = x_ref[rows, :] + gate * acc
            if final_norm:
                out = _rms(out) * fg_ref[...]
            o_ref[rows, :] = out


def _ffn_sublayer(x, mod, layer, row0, row_stride, g, w_in, w_out, final_g, final_norm, tile,
                  casts=()):
    n_seq, seq_len, _ = x.shape
    cast_in, cast_out, cast_shapes = _riding_casts(casts, n_seq, seq_len // tile)
    kern = _with_casts(functools.partial(_ffn_kernel, final_norm=final_norm), 6, 1, len(casts))
    res = pl.pallas_call(
        kern,
        out_shape=(jax.ShapeDtypeStruct(x.shape, F32), *cast_shapes),
        grid=(n_seq, seq_len // tile),
        in_specs=[
            pl.BlockSpec((None, tile, D_MODEL), lambda s, t: (s, t, 0)),
            _mod_spec(layer, row0, row_stride),
            _const_spec((1, D_MODEL)),
            _const_spec((D_MODEL, 2 * D_FF)),
            _const_spec((D_FF, D_MODEL)),
            _const_spec((1, D_MODEL)),
            *cast_in,
        ],
        out_specs=(pl.BlockSpec((None, tile, D_MODEL), lambda s, t: (s, t, 0)), *cast_out),
        compiler_params=_params(("arbitrary", "arbitrary")),
        name="ffn",
    )(x, mod, g, w_in, w_out, final_g, *[w for w, _ in casts])
    return res[0], list(res[1:])


def _rope_lane_order(a):
    lead = a.shape[:-1]
    return a.reshape(*lead, 2, 2, AXIS_DIM // 2).swapaxes(-3, -2).reshape(*lead, HEAD_DIM)


def _rope(a, cos, sin_signed):
    return a * cos + pltpu.roll(a, HEAD_DIM // 2, 1) * sin_signed


def _pair_mean_matrix():
    r = lax.broadcasted_iota(jnp.int32, (2 * HEAD_DIM, 2 * HEAD_DIM), 0) // HEAD_DIM
    c = lax.broadcasted_iota(jnp.int32, (2 * HEAD_DIM, 2 * HEAD_DIM), 1) // HEAD_DIM
    return jnp.where(r == c, 1.0 / HEAD_DIM, 0.0).astype(BF16)


def _rope_lane_matrix():
    src = lax.broadcasted_iota(jnp.int32, (HEAD_DIM, HEAD_DIM), 0)
    n = lax.broadcasted_iota(jnp.int32, (HEAD_DIM, HEAD_DIM), 1)
    quarter = AXIS_DIM // 2
    half, axis, c = n // AXIS_DIM, (n // quarter) % 2, n % quarter
    return jnp.where(src == axis * AXIS_DIM + half * quarter + c, 1.0, 0.0).astype(BF16)


def _qkv_kernel(*refs, rope, cached, keep_f32):
    it = iter(refs)
    x_ref, mod_ref, g_ref, w_ref, qg_ref, kg_ref = (next(it) for _ in range(6))
    rope_refs = [next(it) for _ in range(2)] if rope else None
    cache_refs = [next(it) for _ in range(2)] if cached else None
    q_ref, kh_ref, vt_ref = (next(it) for _ in range(3))
    cache_out_refs = [next(it) for _ in range(2)] if cached else None
    f32_refs = [next(it) for _ in range(2)] if keep_f32 else None

    if cached:
        @pl.when(pl.program_id(1) == 0)
        def _():
            for kv in range(N_KV_HEADS):
                k = cache_refs[0][:, kv, :].astype(BF16)
                if rope:
                    k = _dot(k, _rope_lane_matrix()).astype(BF16)
                cache_out_refs[0][kv] = k
                cache_out_refs[1][kv] = cache_refs[1][:, kv, :].T.astype(BF16)

    mod = mod_ref[...]
    shift, scale = mod[0:1], mod[1:2]
    g = g_ref[...]
    gains = [qg_ref[...] * (SM_SCALE * LOG2E)] * N_HEADS + [kg_ref[...]] * N_KV_HEADS
    pair_mean = _pair_mean_matrix()
    v0 = (N_HEADS + N_KV_HEADS) * HEAD_DIM
    sub_rows = min(QKV_SUB_ROWS, x_ref.shape[0])
    n_sub = x_ref.shape[0] // sub_rows

    def rows_of(sub):
        return slice(sub * sub_rows, (sub + 1) * sub_rows)

    def projection(sub):
        h = _modulate(x_ref[rows_of(sub), :], g, shift, scale).astype(BF16)
        return _dot(h, w_ref[...])

    def finish(sub, p):
        rows = rows_of(sub)
        if rope:
            tables = [r[rows, :] for r in rope_refs]
        for pair in range((N_HEADS + N_KV_HEADS) // 2):
            blk = p[:, pair * 2 * HEAD_DIM:(pair + 1) * 2 * HEAD_DIM]
            ms = _dot((blk * blk).astype(BF16), pair_mean)
            normed = blk * lax.rsqrt(ms + EPS)
            for half in range(2):
                hd = 2 * pair + half
                a = normed[:, half * HEAD_DIM:(half + 1) * HEAD_DIM] * gains[hd]
                if rope:
                    a = _rope(a, *tables)
                if hd < N_HEADS:
                    q_ref[hd, rows, :] = a.astype(BF16)
                else:
                    kv = hd - N_HEADS
                    v = p[:, v0 + kv * HEAD_DIM:v0 + (kv + 1) * HEAD_DIM]
                    kh_ref[kv, rows, :] = a.astype(BF16)
                    vt_ref[kv, :, rows] = v.T.astype(BF16)
                    if keep_f32:
                        f32_refs[0][rows, kv, :] = a
                        f32_refs[1][rows, kv, :] = v

    p_next = projection(0)
    for sub in range(n_sub):
        p_cur = p_next
        if sub + 1 < n_sub:
            p_next = projection(sub + 1)
        finish(sub, p_cur)


def _qkv_proj(x, mod, layer, row0, row_stride, g, w_qkv, q_gain, k_gain, rope_tables, cache,
              cache_layer, keep_f32, tile):
    n_seq, seq_len, _ = x.shape
    rope = rope_tables is not None
    cached = cache is not None
    in_specs = [
        pl.BlockSpec((None, tile, D_MODEL), lambda s, t: (s, t, 0)),
        _mod_spec(layer, row0, row_stride),
        _const_spec((1, D_MODEL)),
        _const_spec((D_MODEL, QKV_WIDTH)),
        _const_spec((1, HEAD_DIM)),
        _const_spec((1, HEAD_DIM)),
    ]
    args = [x, mod, g, w_qkv, q_gain, k_gain]
    if rope:
        in_specs += [pl.BlockSpec((tile, HEAD_DIM), lambda s, t: (t, 0))] * 2
        args += list(rope_tables)
    out_shape = [jax.ShapeDtypeStruct((n_seq, N_HEADS, seq_len, HEAD_DIM), BF16),
                 jax.ShapeDtypeStruct((n_seq, N_KV_HEADS, seq_len, HEAD_DIM), BF16),
                 jax.ShapeDtypeStruct((n_seq, N_KV_HEADS, HEAD_DIM, seq_len), BF16)]
    out_specs = [pl.BlockSpec((None, N_HEADS, tile, HEAD_DIM), lambda s, t: (s, 0, t, 0)),
                 pl.BlockSpec((None, N_KV_HEADS, tile, HEAD_DIM), lambda s, t: (s, 0, t, 0)),
                 pl.BlockSpec((None, N_KV_HEADS, HEAD_DIM, tile), lambda s, t: (s, 0, 0, t))]
    if cached:
        past = cache[0].shape[2]
        in_specs += [pl.BlockSpec((None, None, past, N_KV_HEADS, HEAD_DIM),
                                  lambda s, t: (s, cache_layer, 0, 0, 0))] * 2
        args += list(cache)
        out_shape += [jax.ShapeDtypeStruct((n_seq, N_KV_HEADS, past, HEAD_DIM), BF16),
                      jax.ShapeDtypeStruct((n_seq, N_KV_HEADS, HEAD_DIM, past), BF16)]
        out_specs += [pl.BlockSpec((None, N_KV_HEADS, past, HEAD_DIM), lambda s, t: (s, 0, 0, 0)),
                      pl.BlockSpec((None, N_KV_HEADS, HEAD_DIM, past), lambda s, t: (s, 0, 0, 0))]
    if keep_f32:
        out_shape += [jax.ShapeDtypeStruct((n_seq, 1, seq_len, N_KV_HEADS, HEAD_DIM), F32)] * 2
        out_specs += [pl.BlockSpec((None, None, tile, N_KV_HEADS, HEAD_DIM),
                                   lambda s, t: (s, 0, t, 0, 0))] * 2
    return pl.pallas_call(
        functools.partial(_qkv_kernel, rope=rope, cached=cached, keep_f32=keep_f32),
        out_shape=tuple(out_shape),
        grid=(n_seq, seq_len // tile),
        in_specs=in_specs,
        out_specs=tuple(out_specs),
        compiler_params=_params(("arbitrary", "arbitrary")),
        name="qkv_proj",
    )(*args)


def _attn_kernel(*refs, n_segments, key_chunk, inner):
    q_ref = refs[0]
    segments = [(refs[1 + 2 * i], refs[2 + 2 * i]) for i in range(n_segments)]
    x_ref, mod_ref, w_o_ref, qg_ref, o_ref, heads_scr, shift_scr, safe_scr = refs[1 + 2 * n_segments:]
    group = N_HEADS // N_KV_HEADS
    tile = q_ref.shape[1]
    q_rows = tile if inner is None else inner
    n_parts = tile // q_rows

    def chunks_of(part):
        if inner is None:
            return [(seg, c0, min(key_chunk, k_ref.shape[1] - c0))
                    for seg, (k_ref, _) in enumerate(segments)
                    for c0 in range(0, k_ref.shape[1], key_chunk)]
        return [(0, part * inner + c0, min(key_chunk, inner - c0))
                for c0 in range(0, inner, key_chunk)]

    @pl.when(pl.program_id(1) == 0)
    def _():
        qg = qg_ref[...]
        q_sq = HEAD_DIM * (SM_SCALE * LOG2E) ** 2 * jnp.max(qg * qg)
        for kv in range(N_KV_HEADS):
            k_sq = None
            for k_ref, _ in segments:
                k = k_ref[kv].astype(F32)
                seg_sq = jnp.max(jnp.sum(k * k, axis=1, keepdims=True))
                k_sq = seg_sq if k_sq is None else jnp.maximum(k_sq, seg_sq)
            bound_sq = q_sq * k_sq * BOUND_SLACK
            shift_scr[kv] = jnp.full(shift_scr.shape[1:], jnp.sqrt(bound_sq), F32)
            safe_scr[kv] = (bound_sq <= MAX_SAFE_SHIFT ** 2).astype(jnp.int32)

    def bounded():
        items = [(part, hd, ci, chunk, ci == len(chunks_of(part)) - 1)
                 for part in range(n_parts) for hd in range(N_HEADS)
                 for ci, chunk in enumerate(chunks_of(part))]
        scores = {}

        def issue_scores(j):
            part, hd, _, (seg, c0, rows), _ = items[j]
            keys = segments[seg][0][hd // group, c0:c0 + rows, :]
            q = q_ref[hd, part * q_rows:(part + 1) * q_rows, :]
            scores[j] = _dot_nt(keys, q)

        for j in range(min(ATTN_LOOKAHEAD, len(items))):
            issue_scores(j)
        for i, (part, hd, ci, (seg, c0, rows), last) in enumerate(items):
            if i + ATTN_LOOKAHEAD < len(items):
                issue_scores(i + ATTN_LOOKAHEAD)
            p = jnp.exp2(scores.pop(i) - shift_scr[hd // group][0:1, 0:1])
            ps = jnp.sum(p, axis=0, keepdims=True)
            pv = _dot(segments[seg][1][hd // group, :, c0:c0 + rows], p.astype(BF16))
            denom, acc = (ps, pv) if ci == 0 else (denom + ps, acc + pv)
            if last:
                heads_scr[hd, part * q_rows:(part + 1) * q_rows, :] = (
                    (acc * (1.0 / denom)).T.astype(BF16))

    def exact():
        def one_head(hd, carry):
            kv = hd // group
            for part in range(n_parts):
                rows_q = slice(part * q_rows, (part + 1) * q_rows)
                q = q_ref[hd, rows_q, :]
                if inner is None:
                    key_sets = [(k_ref[kv], vt_ref[kv]) for k_ref, vt_ref in segments]
                else:
                    keys = slice(part * inner, (part + 1) * inner)
                    key_sets = [(segments[0][0][kv, keys, :], segments[0][1][kv, :, keys])]
                scores = [_dot_nt(k, q) for k, _ in key_sets]
                m = None
                for s in scores:
                    sm = jnp.max(s, axis=0, keepdims=True)
                    m = sm if m is None else jnp.maximum(m, sm)
                denom = o_t = None
                for s, (_, vt) in zip(scores, key_sets):
                    p = jnp.exp2(s - m)
                    ps = jnp.sum(p, axis=0, keepdims=True)
                    pv = _dot(vt, p.astype(BF16))
                    denom, o_t = (ps, pv) if denom is None else (denom + ps, o_t + pv)
                heads_scr[hd, rows_q, :] = (o_t * (1.0 / denom)).T.astype(BF16)
            return carry

        lax.fori_loop(0, N_HEADS, one_head, 0)

    safe = safe_scr[0]
    for kv in range(1, N_KV_HEADS):
        safe = jnp.minimum(safe, safe_scr[kv])
    pl.when(safe == 1)(bounded)
    pl.when(safe != 1)(exact)

    attn = jnp.concatenate([heads_scr[hd] for hd in range(N_HEADS)], axis=-1)
    o_ref[...] = x_ref[...] + mod_ref[2:3] * _dot(attn, w_o_ref[...])


def _attn_sublayer(q, key_segments, x, mod, layer, row0, row_stride, w_o, q_gain, tile, inner=None):
    n_seq, seq_len, _ = x.shape
    if inner is not None and inner == seq_len:
        inner = None
    assert inner is None or (len(key_segments) == 1 and tile % inner == 0)
    in_specs = [pl.BlockSpec((None, N_HEADS, tile, HEAD_DIM), lambda s, t: (s, 0, t, 0))]
    args = [q]
    for k, vt in key_segments:
        n_keys = k.shape[2]
        in_specs += [pl.BlockSpec((None, N_KV_HEADS, n_keys, HEAD_DIM), lambda s, t: (s, 0, 0, 0)),
                     pl.BlockSpec((None, N_KV_HEADS, HEAD_DIM, n_keys), lambda s, t: (s, 0, 0, 0))]
        args += [k, vt]
    in_specs += [
        pl.BlockSpec((None, tile, D_MODEL), lambda s, t: (s, t, 0)),
        _mod_spec(layer, row0, row_stride),
        _const_spec((D_MODEL, D_MODEL)),
        _const_spec((1, HEAD_DIM)),
    ]
    args += [x, mod, w_o, q_gain]
    return pl.pallas_call(
        functools.partial(_attn_kernel, n_segments=len(key_segments), key_chunk=ATTN_KEY_CHUNK,
                          inner=inner),
        out_shape=jax.ShapeDtypeStruct(x.shape, F32),
        grid=(n_seq, seq_len // tile),
        in_specs=in_specs,
        out_specs=pl.BlockSpec((None, tile, D_MODEL), lambda s, t: (s, t, 0)),
        scratch_shapes=[pltpu.VMEM((N_HEADS, tile, HEAD_DIM), BF16),
                        pltpu.VMEM((N_KV_HEADS, 8, HEAD_DIM), F32),
                        pltpu.SMEM((N_KV_HEADS,), jnp.int32)],
        compiler_params=_params(("arbitrary", "arbitrary")),
        name="attention",
    )(*args)


def _rope_tables(seq_len):
    rows = seq_len // GRID_W
    r = np.repeat(np.arange(rows), GRID_W).astype(np.float32)
    col = np.tile(np.arange(GRID_W), rows).astype(np.float32)
    expo = -np.arange(0, AXIS_DIM, 2, dtype=np.float32) / np.float32(AXIS_DIM)
    inv = np.power(np.float32(ROPE_THETA), expo).astype(np.float32)
    ar = r[:, None] * inv
    ac = col[:, None] * inv
    ang = np.concatenate([ar, ar, ac, ac], axis=-1)
    cos, sin = np.cos(ang), np.sin(ang)
    first = (np.arange(HEAD_DIM) % AXIS_DIM) < AXIS_DIM // 2
    return (jnp.asarray(_rope_lane_order(cos), F32),
            jnp.asarray(_rope_lane_order(np.where(first, -sin, sin)), F32))


def _rope_weights(w_qkv, q_gain, k_gain):
    qk = (N_HEADS + N_KV_HEADS) * HEAD_DIM
    lead = w_qkv.shape[:-1]
    w_qk = _rope_lane_order(w_qkv[..., :qk].reshape(*lead, N_HEADS + N_KV_HEADS, HEAD_DIM))
    w = jnp.concatenate([w_qk.reshape(*lead, qk), w_qkv[..., qk:]], axis=-1)
    return w, _rope_lane_order(q_gain), _rope_lane_order(k_gain)


def _run_layer(x, l, mod, grp, wts, ffn_w, ffn_f32, ride):
    row0, row_stride = grp["row0"], grp["row_stride"]
    i = l // 2
    g1 = wts["norm_g"][l, 0].reshape(1, D_MODEL)
    g2 = wts["norm_g"][l, 1].reshape(1, D_MODEL)
    new_kv = None
    if l % 2 == 0:
        tile = min(EVEN_TILE, x.shape[1])
        casts = []
        if l == 0 and 0 not in ffn_w:
            steps = x.shape[0] * (x.shape[1] // tile)
            if ride and all(_can_ride(w, steps) for w in ffn_f32):
                casts = [(w, 0) for w in ffn_f32]
            else:
                ffn_w[0] = tuple(w[0].astype(BF16) for w in ffn_f32)
        x, cast_out = _even_sublayer(x, mod, l, row0, row_stride, g1, wts["w_in_even"][i],
                                     wts["pool_w"][i], wts["pool_scale"][i].reshape(1, POOL_WIDTH),
                                     wts["conv_w"][i], wts["conv_b"][i].reshape(1, CONV_WIDTH),
                                     wts["w_out_even"][i], tile, grp["inner"], casts)
        if casts:
            ffn_w[0] = tuple(cast_out)
    else:
        cache, rope_tables = grp["cache"], grp["rope_tables"]
        sfx = "" if rope_tables is None else "_rope"
        res = _qkv_proj(x, mod, l, row0, row_stride, g1, wts["w_qkv" + sfx][i],
                        wts["q_gain" + sfx][i].reshape(1, HEAD_DIM),
                        wts["k_gain" + sfx][i].reshape(1, HEAD_DIM), rope_tables, cache, i,
                        cache is None, min(QKV_TILE, x.shape[1]))
        q, segments = res[0], [res[1:3]]
        if cache is None:
            new_kv = res[3:]
        else:
            segments.append(res[3:5])
        x = _attn_sublayer(q, segments, x, mod, l, row0, row_stride, wts["w_o"][i],
                           wts["q_gain"][i].reshape(1, HEAD_DIM), grp["attn_tile"], grp["inner"])
    flat = x.reshape(1, -1, D_MODEL) if row_stride == 0 else x
    tile = min(FFN_TILE, flat.shape[1])
    casts = []
    if l + 1 < DEPTH and l + 1 not in ffn_w:
        steps = flat.shape[0] * (flat.shape[1] // tile)
        if ride and all(_can_ride(w, steps) for w in ffn_f32):
            casts = [(w, l + 1) for w in ffn_f32]
        else:
            ffn_w[l + 1] = tuple(w[l + 1].astype(BF16) for w in ffn_f32)
    flat, cast_out = _ffn_sublayer(flat, mod, l, row0, row_stride, g2, *ffn_w[l],
                                   wts["final_g"].reshape(1, D_MODEL), l == DEPTH - 1, tile, casts)
    if casts:
        ffn_w[l + 1] = tuple(cast_out)
    return flat.reshape(x.shape), new_kv


def kernel(x_prompt, x_sample, cache_k, cache_v, c, c_ctx, w_ada, b_ada, norm_g, w_in_even, pool_w,
           pool_scale, conv_w, conv_b, w_out_even, w_qkv, q_gain, k_gain, w_o, w_ffn_in, w_ffn_out,
           final_g):
    n_ctx, ctx_len, _ = x_prompt.shape
    n_lat, lat_len, _ = x_sample.shape
    assert 1 + n_lat <= COND_ROWS

    cond = jnp.concatenate(
        [c_ctx[None, :], c, jnp.zeros((COND_ROWS - 1 - n_lat, D_MODEL), F32)], axis=0)
    mod = _ada(cond, w_ada, b_ada).reshape(DEPTH, COND_ROWS, 6, D_MODEL)

    wts = dict(
        norm_g=norm_g, final_g=final_g, pool_scale=pool_scale, conv_w=conv_w, conv_b=conv_b,
        q_gain=q_gain, k_gain=k_gain,
        w_in_even=w_in_even.astype(BF16), pool_w=pool_w.astype(BF16),
        w_out_even=w_out_even.astype(BF16), w_qkv=w_qkv.astype(BF16), w_o=w_o.astype(BF16),
    )
    w_rope, qg_rope, kg_rope = _rope_weights(wts["w_qkv"], q_gain, k_gain)
    wts.update(w_qkv_rope=w_rope, q_gain_rope=qg_rope, k_gain_rope=kg_rope)

    pack = math.gcd(n_ctx, max(1, CTX_PACK_ROWS // ctx_len))
    ctx_grp = dict(row0=0, row_stride=0, cache=None, rope_tables=None, attn_tile=pack * ctx_len,
                   inner=ctx_len)
    lat_grp = dict(row0=1, row_stride=1, cache=(cache_k, cache_v), rope_tables=_rope_tables(lat_len),
                   attn_tile=ATTN_TILE, inner=None)
    ffn_w, ffn_f32 = {}, (w_ffn_in, w_ffn_out)
    y_prompt = x_prompt.reshape(n_ctx // pack, pack * ctx_len, D_MODEL)
    y_sample, ctx_kv = x_sample, []
    for l in range(DEPTH):
        y_sample, _ = _run_layer(y_sample, l, mod, lat_grp, wts, ffn_w, ffn_f32, ride=True)
        y_prompt, kv = _run_layer(y_prompt, l, mod, ctx_grp, wts, ffn_w, ffn_f32, ride=False)
        if kv is not None:
            ctx_kv.append(kv)
    kv_shape = (n_ctx, 1, ctx_len, N_KV_HEADS, HEAD_DIM)
    new_cache_k = jnp.concatenate([k.reshape(kv_shape) for k, _ in ctx_kv], axis=1)
    new_cache_v = jnp.concatenate([v.reshape(kv_shape) for _, v in ctx_kv], axis=1)
    return y_prompt.reshape(x_prompt.shape), y_sample, new_cache_k, new_cache_v
```

```python
import functools
import math

import jax
import jax.numpy as jnp
import numpy as np
from jax import lax
from jax.experimental import pallas as pl
from jax.experimental.pallas import tpu as pltpu

D_MODEL = 1024
DEPTH = 2
GRID_W = 64
EPS = 1e-6
POOL_WIDTH = D_MODEL // 2
POOL_GROUPS = 4
POOL_GROUP_DIM = POOL_WIDTH // POOL_GROUPS
POOL_WINDOWS = (2, 4, 8, 16)
CONV_WIDTH = D_MODEL // 2
EVEN_IN = POOL_WIDTH + 3 * CONV_WIDTH
HEAD_DIM = 128
N_HEADS = D_MODEL // HEAD_DIM
N_KV_HEADS = N_HEADS // 4
KV_WIDTH = N_KV_HEADS * HEAD_DIM
QKV_WIDTH = (N_HEADS + 2 * N_KV_HEADS) * HEAD_DIM
ROPE_THETA = 10000.0
AXIS_DIM = HEAD_DIM // 2
D_FF = ((8 * D_MODEL // 3 + 255) // 256) * 256
SM_SCALE = HEAD_DIM ** -0.5
LOG2E = math.log2(math.e)

COND_ROWS = 8
HALO = 16
ADA_BLOCK_N = 1536
FF_CHUNKS = ((0, 1024), (1024, 1024), (2048, 768))
CTX_PACK_ROWS = 1024
QKV_TILE = 1024
QKV_SUB_ROWS = 128
EVEN_TILE = 1024
EVEN_SUB_ROWS = 512
FFN_TILE = 1024
FFN_SUB_ROWS = 256
FFN_LOOKAHEAD = 1
ATTN_TILE = 256
ATTN_KEY_CHUNK = 512
ATTN_LOOKAHEAD = 2
BOUND_SLACK = 1.1
MAX_SAFE_SHIFT = 60.0
VMEM_LIMIT_BYTES = 56 * 1024 * 1024

BF16 = jnp.bfloat16
F32 = jnp.float32


def _dot(a, b):
    return jnp.dot(a, b, preferred_element_type=F32)


def _dot_nt(a, b):
    return lax.dot_general(a, b, (((1,), (1,)), ((), ())), preferred_element_type=F32)


def _silu(a):
    return a * (1.0 / (1.0 + jnp.exp(-a)))


def _rms(x):
    return x * lax.rsqrt(jnp.mean(x * x, axis=-1, keepdims=True) + EPS)


def _modulate(x, g, shift, scale):
    return (_rms(x) * g) * (1.0 + scale) + shift


def _params(sem):
    return pltpu.CompilerParams(dimension_semantics=sem, vmem_limit_bytes=VMEM_LIMIT_BYTES)


def _const_spec(shape):
    n = len(shape)
    return pl.BlockSpec(shape, lambda *_: (0,) * n, pipeline_mode=pl.Buffered(1))


def _riding_casts(casts, n_seq, n_t):
    steps = n_seq * n_t
    in_specs, out_specs, out_shapes = [], [], []
    for w, layer, _ in casts:
        rows = w.shape[1] // steps
        in_specs.append(pl.BlockSpec((None, rows, w.shape[2]),
                                     lambda s, t, layer=layer: (layer, s * n_t + t, 0)))
        out_specs.append(pl.BlockSpec((rows, w.shape[2]), lambda s, t: (s * n_t + t, 0)))
        out_shapes.append(jax.ShapeDtypeStruct(w.shape[1:], BF16))
    return in_specs, out_specs, out_shapes


def _can_ride(w, steps):
    return w.shape[1] % steps == 0 and (w.shape[1] // steps) % 16 == 0


def _rope_lane_order_cols(w):
    quarter_w = AXIS_DIM // 2
    lane = lax.broadcasted_iota(jnp.int32, w.shape, 1)
    quarter = (lane % HEAD_DIM) // quarter_w
    qk = lane < (N_HEADS + N_KV_HEADS) * HEAD_DIM
    from_right = pltpu.roll(w, w.shape[1] - quarter_w, 1)
    from_left = pltpu.roll(w, quarter_w, 1)
    swapped = jnp.where(quarter == 1, from_right, jnp.where(quarter == 2, from_left, w))
    return jnp.where(qk, swapped, w)


def _with_casts(body, n_in, n_out, kinds):
    n_cast = len(kinds)

    def run(*refs):
        ins, cast_ins = refs[:n_in], refs[n_in:n_in + n_cast]
        outs = refs[n_in + n_cast:n_in + n_cast + n_out]
        cast_outs = refs[n_in + n_cast + n_out:n_in + 2 * n_cast + n_out]
        for src, dst, kind in zip(cast_ins, cast_outs, kinds):
            w = src[...]
            dst[...] = (_rope_lane_order_cols(w) if kind == "rope" else w).astype(dst.dtype)
        body(*ins, *outs, *refs[n_in + 2 * n_cast + n_out:])
    return run


def _ada_kernel(cond_ref, w_ref, b_ref, o_ref):
    s = _silu(cond_ref[...]).astype(BF16)
    o_ref[...] = _dot(s, w_ref[...].astype(BF16)) + b_ref[...]


def _ada(cond, w_ada, b_ada):
    n_out = w_ada.shape[-1]
    return pl.pallas_call(
        _ada_kernel,
        out_shape=jax.ShapeDtypeStruct((DEPTH, COND_ROWS, n_out), F32),
        grid=(DEPTH, n_out // ADA_BLOCK_N),
        in_specs=[
            pl.BlockSpec((COND_ROWS, D_MODEL), lambda l, j: (0, 0)),
            pl.BlockSpec((None, D_MODEL, ADA_BLOCK_N), lambda l, j: (l, 0, j)),
            pl.BlockSpec((None, 1, ADA_BLOCK_N), lambda l, j: (l, 0, j)),
        ],
        out_specs=pl.BlockSpec((None, COND_ROWS, ADA_BLOCK_N), lambda l, j: (l, 0, j)),
        compiler_params=_params(("arbitrary", "arbitrary")),
        name="ada",
    )(cond, w_ada, b_ada.reshape(DEPTH, 1, n_out))


def _mod_spec(layer, row0, row_stride):
    return pl.BlockSpec((None, None, 6, D_MODEL),
                        lambda s, t: (layer, row0 + row_stride * s, 0, 0))


def _even_kernel(x_ref, xp_ref, xn_ref, mod_ref, g_ref, w_in_ref, pool_w_ref, pool_scale_ref,
                 conv_w_ref, conv_b_ref, w_out_ref, o_ref, *, tile, seq_len, packed):
    t = pl.program_id(1)
    n_t = pl.num_programs(1)
    sub = seq_len if packed else min(EVEN_SUB_ROWS, tile)
    n_sub = tile // sub
    win = sub + 2 * HALO
    mod = mod_ref[...]
    shift, scale, gate = mod[0:1], mod[1:2], mod[2:3]
    g = g_ref[...]
    cw = conv_w_ref[...]

    def h_rows(lo, hi):
        pieces = []
        if lo < HALO:
            pieces.append(xp_ref[...])
        pieces.append(x_ref[max(lo, HALO) - HALO:min(hi, HALO + tile) - HALO, :])
        if hi > HALO + tile:
            pieces.append(xn_ref[...])
        return jnp.concatenate([_modulate(r, g, shift, scale).astype(BF16) for r in pieces], axis=0)

    def project(s):
        if packed:
            rows = x_ref[s * sub:(s + 1) * sub, :]
            return _dot(_modulate(rows, g, shift, scale).astype(BF16), w_in_ref[...])
        lo = 0 if s == 0 else s * sub + 2 * HALO
        return _dot(h_rows(lo, (s + 1) * sub + 2 * HALO), w_in_ref[...])

    def edge_masked(a, s):
        if packed:
            return a
        parts = [a[:HALO], a[HALO:HALO + sub], a[HALO + sub:]]
        if s == 0:
            parts[0] = jnp.where(t > 0, parts[0], 0.0)
        if s == n_sub - 1:
            parts[2] = jnp.where(t < n_t - 1, parts[2], 0.0)
        return jnp.concatenate(parts, axis=0)

    def shifted(a, k):
        return pltpu.roll(a, k % win, 0)

    def mixers(s, p):
        first = 0 if packed else t * tile + s * sub
        pos = first + lax.broadcasted_iota(jnp.int32, (sub, 1), 0)
        u = edge_masked(p[:, :POOL_WIDTH], s)
        ya = []
        for i, w in enumerate(POOL_WINDOWS):
            ug = u[:, i * POOL_GROUP_DIM:(i + 1) * POOL_GROUP_DIM]
            fwd, n = ug, 1
            while n < w // 2:
                fwd = fwd + shifted(fwd, -n)
                n *= 2
            wsum = shifted(fwd, w // 2) + fwd
            cnt = (jnp.clip(pos + w // 2, 0, seq_len)
                   - jnp.clip(pos - w // 2, 0, seq_len)).astype(F32)
            pooled = wsum[HALO:HALO + sub] / cnt - ug[HALO:HALO + sub]
            ya.append(_dot(pooled.astype(BF16), pool_w_ref[i]))
        ya = jnp.concatenate(ya, axis=-1) * pool_scale_ref[...]
        c0 = POOL_WIDTH
        bg = p[HALO:HALO + sub, c0:c0 + CONV_WIDTH]
        z = edge_masked(p[:, c0 + CONV_WIDTH:c0 + 2 * CONV_WIDTH] * p[:, c0 + 2 * CONV_WIDTH:], s)
        conv = (shifted(z, 1) * cw[0:1] + z * cw[1:2] + shifted(z, -1) * cw[2:3])[HALO:HALO + sub]
        yb = bg * (conv + conv_b_ref[...])
        y = _dot(jnp.concatenate([ya, yb], axis=-1).astype(BF16), w_out_ref[...])
        rows = slice(s * sub, (s + 1) * sub)
        o_ref[rows, :] = x_ref[rows, :] + gate * y

    block = project(0)
    window = None
    margin = jnp.zeros((HALO, EVEN_IN), F32)
    for s in range(n_sub):
        if packed:
            window = jnp.concatenate([margin, block, margin], axis=0)
        else:
            window = block if s == 0 else jnp.concatenate([window[sub:], block], axis=0)
        if s + 1 < n_sub:
            block = project(s + 1)
        mixers(s, window)


def _even_sublayer(x, mod, layer, row0, row_stride, g, w_in, pool_w, pool_scale, conv_w, conv_b,
                   w_out, tile, inner=None, casts=()):
    n_seq, n_rows, _ = x.shape
    packed = inner is not None and inner < n_rows
    assert not packed or tile % inner == 0
    n_t = n_rows // tile
    hb = tile // HALO
    n_hb = n_rows // HALO
    cast_in, cast_out, cast_shapes = _riding_casts(casts, n_seq, n_t)
    kern = _with_casts(functools.partial(_even_kernel, tile=tile, packed=packed,
                                         seq_len=inner if packed else n_rows), 11, 1,
                       [kind for _, _, kind in casts])
    res = pl.pallas_call(
        kern,
        out_shape=(jax.ShapeDtypeStruct(x.shape, F32), *cast_shapes),
        grid=(n_seq, n_t),
        in_specs=[
            pl.BlockSpec((None, tile, D_MODEL), lambda s, t: (s, t, 0)),
            pl.BlockSpec((None, HALO, D_MODEL), lambda s, t: (s, jnp.maximum(t * hb - 1, 0), 0)),
            pl.BlockSpec((None, HALO, D_MODEL),
                         lambda s, t: (s, jnp.minimum((t + 1) * hb, n_hb - 1), 0)),
            _mod_spec(layer, row0, row_stride),
            _const_spec((1, D_MODEL)),
            _const_spec((D_MODEL, EVEN_IN)),
            _const_spec((POOL_GROUPS, POOL_GROUP_DIM, POOL_GROUP_DIM)),
            _const_spec((1, POOL_WIDTH)),
            _const_spec((3, CONV_WIDTH)),
            _const_spec((1, CONV_WIDTH)),
            _const_spec((POOL_WIDTH + CONV_WIDTH, D_MODEL)),
            *cast_in,
        ],
        out_specs=(pl.BlockSpec((None, tile, D_MODEL), lambda s, t: (s, t, 0)), *cast_out),
        compiler_params=_params(("arbitrary", "arbitrary")),
        name="even_mixer",
    )(x, x, x, mod, g, w_in, pool_w, pool_scale, conv_w, conv_b, w_out, *[c[0] for c in casts])
    return res[0], list(res[1:])


def _ffn_kernel(x_ref, mod_ref, g_ref, w_in_ref, w_out_ref, fg_ref, o_ref, *, final_norm):
    mod = mod_ref[...]
    shift, scale, gate = mod[3:4], mod[4:5], mod[5:6]
    g = g_ref[...]
    n_sub = x_ref.shape[0] // FFN_SUB_ROWS
    items = [(sub, c) for sub in range(n_sub) for c in range(len(FF_CHUNKS))]
    hs, ups = {}, {}

    def issue_up(j):
        sub, c = items[j]
        if c == 0:
            rows = x_ref[sub * FFN_SUB_ROWS:(sub + 1) * FFN_SUB_ROWS, :]
            hs[sub] = _modulate(rows, g, shift, scale).astype(BF16)
        c0, cw = FF_CHUNKS[c]
        ups[j] = (_dot(hs[sub], w_in_ref[:, c0:c0 + cw]),
                  _dot(hs[sub], w_in_ref[:, D_FF + c0:D_FF + c0 + cw]))

    for j in range(min(FFN_LOOKAHEAD, len(items))):
        issue_up(j)
    for i, (sub, c) in enumerate(items):
        if i + FFN_LOOKAHEAD < len(items):
            issue_up(i + FFN_LOOKAHEAD)
        a, b = ups.pop(i)
        c0, cw = FF_CHUNKS[c]
        part = _dot((_silu(a) * b).astype(BF16), w_out_ref[c0:c0 + cw, :])
        acc = part if c == 0 else acc + part
        if c == len(FF_CHUNKS) - 1:
            rows = slice(sub * FFN_SUB_ROWS, (sub + 1) * FFN_SUB_ROWS)
            out = x_ref[rows, :] + gate * acc
            if final_norm:
                out = _rms(out) * fg_ref[...]
            o_ref[rows, :] = out


def _ffn_sublayer(x, mod, layer, row0, row_stride, g, w_in, w_out, final_g, final_norm, tile,
                  casts=()):
    n_seq, seq_len, _ = x.shape
    cast_in, cast_out, cast_shapes = _riding_casts(casts, n_seq, seq_len // tile)
    kern = _with_casts(functools.partial(_ffn_kernel, final_norm=final_norm), 6, 1,
                       [kind for _, _, kind in casts])
    res = pl.pallas_call(
        kern,
        out_shape=(jax.ShapeDtypeStruct(x.shape, F32), *cast_shapes),
        grid=(n_seq, seq_len // tile),
        in_specs=[
            pl.BlockSpec((None, tile, D_MODEL), lambda s, t: (s, t, 0)),
            _mod_spec(layer, row0, row_stride),
            _const_spec((1, D_MODEL)),
            _const_spec((D_MODEL, 2 * D_FF)),
            _const_spec((D_FF, D_MODEL)),
            _const_spec((1, D_MODEL)),
            *cast_in,
        ],
        out_specs=(pl.BlockSpec((None, tile, D_MODEL), lambda s, t: (s, t, 0)), *cast_out),
        compiler_params=_params(("arbitrary", "arbitrary")),
        name="ffn",
    )(x, mod, g, w_in, w_out, final_g, *[c[0] for c in casts])
    return res[0], list(res[1:])


def _rope_lane_order(a):
    lead = a.shape[:-1]
    return a.reshape(*lead, 2, 2, AXIS_DIM // 2).swapaxes(-3, -2).reshape(*lead, HEAD_DIM)


def _rope(a, cos, sin_signed):
    return a * cos + pltpu.roll(a, HEAD_DIM // 2, 1) * sin_signed


def _pair_mean_matrix():
    r = lax.broadcasted_iota(jnp.int32, (2 * HEAD_DIM, 2 * HEAD_DIM), 0) // HEAD_DIM
    c = lax.broadcasted_iota(jnp.int32, (2 * HEAD_DIM, 2 * HEAD_DIM), 1) // HEAD_DIM
    return jnp.where(r == c, 1.0 / HEAD_DIM, 0.0).astype(BF16)


def _rope_lane_matrix():
    src = lax.broadcasted_iota(jnp.int32, (HEAD_DIM, HEAD_DIM), 0)
    n = lax.broadcasted_iota(jnp.int32, (HEAD_DIM, HEAD_DIM), 1)
    quarter = AXIS_DIM // 2
    half, axis, c = n // AXIS_DIM, (n // quarter) % 2, n % quarter
    return jnp.where(src == axis * AXIS_DIM + half * quarter + c, 1.0, 0.0).astype(BF16)


def _qkv_kernel(*refs, rope, cached, keep_f32):
    it = iter(refs)
    x_ref, mod_ref, g_ref, w_ref, qg_ref, kg_ref = (next(it) for _ in range(6))
    rope_refs = [next(it) for _ in range(2)] if rope else None
    cache_refs = [next(it) for _ in range(2)] if cached else None
    q_ref, kh_ref, vt_ref = (next(it) for _ in range(3))
    cache_out_refs = [next(it) for _ in range(2)] if cached else None
    f32_refs = [next(it) for _ in range(2)] if keep_f32 else None

    if cached:
        @pl.when(pl.program_id(1) == 0)
        def _():
            for kv in range(N_KV_HEADS):
                k = cache_refs[0][:, kv, :].astype(BF16)
                if rope:
                    k = _dot(k, _rope_lane_matrix()).astype(BF16)
                cache_out_refs[0][kv] = k
                cache_out_refs[1][kv] = cache_refs[1][:, kv, :].T.astype(BF16)

    mod = mod_ref[...]
    shift, scale = mod[0:1], mod[1:2]
    g = g_ref[...]
    gains = [qg_ref[...] * (SM_SCALE * LOG2E)] * N_HEADS + [kg_ref[...]] * N_KV_HEADS
    pair_mean = _pair_mean_matrix()
    v0 = (N_HEADS + N_KV_HEADS) * HEAD_DIM
    sub_rows = min(QKV_SUB_ROWS, x_ref.shape[0])
    n_sub = x_ref.shape[0] // sub_rows

    def rows_of(sub):
        return slice(sub * sub_rows, (sub + 1) * sub_rows)

    def projection(sub):
        h = _modulate(x_ref[rows_of(sub), :], g, shift, scale).astype(BF16)
        return _dot(h, w_ref[...])

    def finish(sub, p):
        rows = rows_of(sub)
        if rope:
            tables = [r[rows, :] for r in rope_refs]
        for pair in range((N_HEADS + N_KV_HEADS) // 2):
            blk = p[:, pair * 2 * HEAD_DIM:(pair + 1) * 2 * HEAD_DIM]
            ms = _dot((blk * blk).astype(BF16), pair_mean)
            normed = blk * lax.rsqrt(ms + EPS)
            for half in range(2):
                hd = 2 * pair + half
                a = normed[:, half * HEAD_DIM:(half + 1) * HEAD_DIM] * gains[hd]
                if rope:
                    a = _rope(a, *tables)
                if hd < N_HEADS:
                    q_ref[hd, rows, :] = a.astype(BF16)
                else:
                    kv = hd - N_HEADS
                    v = p[:, v0 + kv * HEAD_DIM:v0 + (kv + 1) * HEAD_DIM]
                    kh_ref[kv, rows, :] = a.astype(BF16)
                    vt_ref[kv, :, rows] = v.T.astype(BF16)
                    if keep_f32:
                        f32_refs[0][rows, kv, :] = a
                        f32_refs[1][rows, kv, :] = v

    p_next = projection(0)
    for sub in range(n_sub):
        p_cur = p_next
        if sub + 1 < n_sub:
            p_next = projection(sub + 1)
        finish(sub, p_cur)


def _qkv_proj(x, mod, layer, row0, row_stride, g, w_qkv, q_gain, k_gain, rope_tables, cache,
              cache_layer, keep_f32, tile):
    n_seq, seq_len, _ = x.shape
    rope = rope_tables is not None
    cached = cache is not None
    in_specs = [
        pl.BlockSpec((None, tile, D_MODEL), lambda s, t: (s, t, 0)),
        _mod_spec(layer, row0, row_stride),
        _const_spec((1, D_MODEL)),
        _const_spec((D_MODEL, QKV_WIDTH)),
        _const_spec((1, HEAD_DIM)),
        _const_spec((1, HEAD_DIM)),
    ]
    args = [x, mod, g, w_qkv, q_gain, k_gain]
    if rope:
        in_specs += [pl.BlockSpec((tile, HEAD_DIM), lambda s, t: (t, 0))] * 2
        args += list(rope_tables)
    out_shape = [jax.ShapeDtypeStruct((n_seq, N_HEADS, seq_len, HEAD_DIM), BF16),
                 jax.ShapeDtypeStruct((n_seq, N_KV_HEADS, seq_len, HEAD_DIM), BF16),
                 jax.ShapeDtypeStruct((n_seq, N_KV_HEADS, HEAD_DIM, seq_len), BF16)]
    out_specs = [pl.BlockSpec((None, N_HEADS, tile, HEAD_DIM), lambda s, t: (s, 0, t, 0)),
                 pl.BlockSpec((None, N_KV_HEADS, tile, HEAD_DIM), lambda s, t: (s, 0, t, 0)),
                 pl.BlockSpec((None, N_KV_HEADS, HEAD_DIM, tile), lambda s, t: (s, 0, 0, t))]
    if cached:
        past = cache[0].shape[2]
        in_specs += [pl.BlockSpec((None, None, past, N_KV_HEADS, HEAD_DIM),
                                  lambda s, t: (s, cache_layer, 0, 0, 0))] * 2
        args += list(cache)
        out_shape += [jax.ShapeDtypeStruct((n_seq, N_KV_HEADS, past, HEAD_DIM), BF16),
                      jax.ShapeDtypeStruct((n_seq, N_KV_HEADS, HEAD_DIM, past), BF16)]
        out_specs += [pl.BlockSpec((None, N_KV_HEADS, past, HEAD_DIM), lambda s, t: (s, 0, 0, 0)),
                      pl.BlockSpec((None, N_KV_HEADS, HEAD_DIM, past), lambda s, t: (s, 0, 0, 0))]
    if keep_f32:
        out_shape += [jax.ShapeDtypeStruct((n_seq, 1, seq_len, N_KV_HEADS, HEAD_DIM), F32)] * 2
        out_specs += [pl.BlockSpec((None, None, tile, N_KV_HEADS, HEAD_DIM),
                                   lambda s, t: (s, 0, t, 0, 0))] * 2
    return pl.pallas_call(
        functools.partial(_qkv_kernel, rope=rope, cached=cached, keep_f32=keep_f32),
        out_shape=tuple(out_shape),
        grid=(n_seq, seq_len // tile),
        in_specs=in_specs,
        out_specs=tuple(out_specs),
        compiler_params=_params(("arbitrary", "arbitrary")),
        name="qkv_proj",
    )(*args)


def _attn_kernel(*refs, n_segments, key_chunk, inner):
    q_ref = refs[0]
    segments = [(refs[1 + 2 * i], refs[2 + 2 * i]) for i in range(n_segments)]
    x_ref, mod_ref, w_o_ref, qg_ref, o_ref, heads_scr, shift_scr, safe_scr = refs[1 + 2 * n_segments:]
    group = N_HEADS // N_KV_HEADS
    tile = q_ref.shape[1]
    q_rows = tile if inner is None else inner
    n_parts = tile // q_rows

    def chunks_of(part):
        if inner is None:
            return [(seg, c0, min(key_chunk, k_ref.shape[1] - c0))
                    for seg, (k_ref, _) in enumerate(segments)
                    for c0 in range(0, k_ref.shape[1], key_chunk)]
        return [(0, part * inner + c0, min(key_chunk, inner - c0))
                for c0 in range(0, inner, key_chunk)]

    @pl.when(pl.program_id(1) == 0)
    def _():
        qg = qg_ref[...]
        q_sq = HEAD_DIM * (SM_SCALE * LOG2E) ** 2 * jnp.max(qg * qg)
        for kv in range(N_KV_HEADS):
            k_sq = None
            for k_ref, _ in segments:
                k = k_ref[kv].astype(F32)
                seg_sq = jnp.max(jnp.sum(k * k, axis=1, keepdims=True))
                k_sq = seg_sq if k_sq is None else jnp.maximum(k_sq, seg_sq)
            bound_sq = q_sq * k_sq * BOUND_SLACK
            shift_scr[kv] = jnp.full(shift_scr.shape[1:], jnp.sqrt(bound_sq), F32)
            safe_scr[kv] = (bound_sq <= MAX_SAFE_SHIFT ** 2).astype(jnp.int32)

    def bounded():
        items = [(part, hd, ci, chunk, ci == len(chunks_of(part)) - 1)
                 for part in range(n_parts) for hd in range(N_HEADS)
                 for ci, chunk in enumerate(chunks_of(part))]
        scores = {}

        def issue_scores(j):
            part, hd, _, (seg, c0, rows), _ = items[j]
            keys = segments[seg][0][hd // group, c0:c0 + rows, :]
            q = q_ref[hd, part * q_rows:(part + 1) * q_rows, :]
            scores[j] = _dot_nt(keys, q)

        for j in range(min(ATTN_LOOKAHEAD, len(items))):
            issue_scores(j)
        for i, (part, hd, ci, (seg, c0, rows), last) in enumerate(items):
            if i + ATTN_LOOKAHEAD < len(items):
                issue_scores(i + ATTN_LOOKAHEAD)
            p = jnp.exp2(scores.pop(i) - shift_scr[hd // group][0:1, 0:1])
            ps = jnp.sum(p, axis=0, keepdims=True)
            pv = _dot(segments[seg][1][hd // group, :, c0:c0 + rows], p.astype(BF16))
            denom, acc = (ps, pv) if ci == 0 else (denom + ps, acc + pv)
            if last:
                heads_scr[hd, part * q_rows:(part + 1) * q_rows, :] = (
                    (acc * (1.0 / denom)).T.astype(BF16))

    def exact():
        def one_head(hd, carry):
            kv = hd // group
            for part in range(n_parts):
                rows_q = slice(part * q_rows, (part + 1) * q_rows)
                q = q_ref[hd, rows_q, :]
                if inner is None:
                    key_sets = [(k_ref[kv], vt_ref[kv]) for k_ref, vt_ref in segments]
                else:
                    keys = slice(part * inner, (part + 1) * inner)
                    key_sets = [(segments[0][0][kv, keys, :], segments[0][1][kv, :, keys])]
                scores = [_dot_nt(k, q) for k, _ in key_sets]
                m = None
                for s in scores:
                    sm = jnp.max(s, axis=0, keepdims=True)
                    m = sm if m is None else jnp.maximum(m, sm)
                denom = o_t = None
                for s, (_, vt) in zip(scores, key_sets):
                    p = jnp.exp2(s - m)
                    ps = jnp.sum(p, axis=0, keepdims=True)
                    pv = _dot(vt, p.astype(BF16))
                    denom, o_t = (ps, pv) if denom is None else (denom + ps, o_t + pv)
                heads_scr[hd, rows_q, :] = (o_t * (1.0 / denom)).T.astype(BF16)
            return carry

        lax.fori_loop(0, N_HEADS, one_head, 0)

    safe = safe_scr[0]
    for kv in range(1, N_KV_HEADS):
        safe = jnp.minimum(safe, safe_scr[kv])
    pl.when(safe == 1)(bounded)
    pl.when(safe != 1)(exact)

    attn = jnp.concatenate([heads_scr[hd] for hd in range(N_HEADS)], axis=-1)
    o_ref[...] = x_ref[...] + mod_ref[2:3] * _dot(attn, w_o_ref[...])


def _attn_sublayer(q, key_segments, x, mod, layer, row0, row_stride, w_o, q_gain, tile, inner=None):
    n_seq, seq_len, _ = x.shape
    if inner is not None and inner == seq_len:
        inner = None
    assert inner is None or (len(key_segments) == 1 and tile % inner == 0)
    in_specs = [pl.BlockSpec((None, N_HEADS, tile, HEAD_DIM), lambda s, t: (s, 0, t, 0))]
    args = [q]
    for k, vt in key_segments:
        n_keys = k.shape[2]
        in_specs += [pl.BlockSpec((None, N_KV_HEADS, n_keys, HEAD_DIM), lambda s, t: (s, 0, 0, 0)),
                     pl.BlockSpec((None, N_KV_HEADS, HEAD_DIM, n_keys), lambda s, t: (s, 0, 0, 0))]
        args += [k, vt]
    in_specs += [
        pl.BlockSpec((None, tile, D_MODEL), lambda s, t: (s, t, 0)),
        _mod_spec(layer, row0, row_stride),
        _const_spec((D_MODEL, D_MODEL)),
        _const_spec((1, HEAD_DIM)),
    ]
    args += [x, mod, w_o, q_gain]
    return pl.pallas_call(
        functools.partial(_attn_kernel, n_segments=len(key_segments), key_chunk=ATTN_KEY_CHUNK,
                          inner=inner),
        out_shape=jax.ShapeDtypeStruct(x.shape, F32),
        grid=(n_seq, seq_len // tile),
        in_specs=in_specs,
        out_specs=pl.BlockSpec((None, tile, D_MODEL), lambda s, t: (s, t, 0)),
        scratch_shapes=[pltpu.VMEM((N_HEADS, tile, HEAD_DIM), BF16),
                        pltpu.VMEM((N_KV_HEADS, 8, HEAD_DIM), F32),
                        pltpu.SMEM((N_KV_HEADS,), jnp.int32)],
        compiler_params=_params(("arbitrary", "arbitrary")),
        name="attention",
    )(*args)


def _rope_tables(seq_len):
    rows = seq_len // GRID_W
    r = np.repeat(np.arange(rows), GRID_W).astype(np.float32)
    col = np.tile(np.arange(GRID_W), rows).astype(np.float32)
    expo = -np.arange(0, AXIS_DIM, 2, dtype=np.float32) / np.float32(AXIS_DIM)
    inv = np.power(np.float32(ROPE_THETA), expo).astype(np.float32)
    ar = r[:, None] * inv
    ac = col[:, None] * inv
    ang = np.concatenate([ar, ar, ac, ac], axis=-1)
    cos, sin = np.cos(ang), np.sin(ang)
    first = (np.arange(HEAD_DIM) % AXIS_DIM) < AXIS_DIM // 2
    return (jnp.asarray(_rope_lane_order(cos), F32),
            jnp.asarray(_rope_lane_order(np.where(first, -sin, sin)), F32))


def _rope_weights(w_qkv):
    qk = (N_HEADS + N_KV_HEADS) * HEAD_DIM
    lead = w_qkv.shape[:-1]
    w_qk = _rope_lane_order(w_qkv[..., :qk].reshape(*lead, N_HEADS + N_KV_HEADS, HEAD_DIM))
    return jnp.concatenate([w_qk.reshape(*lead, qk), w_qkv[..., qk:]], axis=-1)


def _run_layer(x, l, mod, grp, wts, ffn_w, ffn_f32, odd_w, odd_f32, ride):
    row0, row_stride = grp["row0"], grp["row_stride"]
    i = l // 2
    g1 = wts["norm_g"][l, 0].reshape(1, D_MODEL)
    g2 = wts["norm_g"][l, 1].reshape(1, D_MODEL)
    new_kv = None
    if l % 2 == 0:
        tile = min(EVEN_TILE, x.shape[1])
        casts = []
        if l == 0 and 0 not in ffn_w:
            steps = x.shape[0] * (x.shape[1] // tile)
            if ride and all(_can_ride(w, steps) for w in ffn_f32 + odd_f32):
                casts = [(w, 0, "plain") for w in ffn_f32]
                for i_odd in range(DEPTH // 2):
                    casts += [(odd_f32[0], i_odd, "plain"), (odd_f32[0], i_odd, "rope"),
                              (odd_f32[1], i_odd, "plain")]
            else:
                ffn_w[0] = tuple(w[0].astype(BF16) for w in ffn_f32)
                for i_odd in range(DEPTH // 2):
                    w_qkv_bf16 = odd_f32[0][i_odd].astype(BF16)
                    odd_w[i_odd] = (w_qkv_bf16, _rope_weights(w_qkv_bf16),
                                    odd_f32[1][i_odd].astype(BF16))
        x, cast_out = _even_sublayer(x, mod, l, row0, row_stride, g1, wts["w_in_even"][i],
                                     wts["pool_w"][i], wts["pool_scale"][i].reshape(1, POOL_WIDTH),
                                     wts["conv_w"][i], wts["conv_b"][i].reshape(1, CONV_WIDTH),
                                     wts["w_out_even"][i], tile, grp["inner"], casts)
        if casts:
            ffn_w[0] = tuple(cast_out[:2])
            for i_odd in range(DEPTH // 2):
                odd_w[i_odd] = tuple(cast_out[2 + 3 * i_odd:5 + 3 * i_odd])
    else:
        cache, rope_tables = grp["cache"], grp["rope_tables"]
        sfx = "" if rope_tables is None else "_rope"
        w_qkv, w_qkv_rope, w_o = odd_w[i]
        res = _qkv_proj(x, mod, l, row0, row_stride, g1, w_qkv if rope_tables is None else w_qkv_rope,
                        wts["q_gain" + sfx][i].reshape(1, HEAD_DIM),
                        wts["k_gain" + sfx][i].reshape(1, HEAD_DIM), rope_tables, cache, i,
                        cache is None, min(QKV_TILE, x.shape[1]))
        q, segments = res[0], [res[1:3]]
        if cache is None:
            new_kv = res[3:]
        else:
            segments.append(res[3:5])
        x = _attn_sublayer(q, segments, x, mod, l, row0, row_stride, w_o,
                           wts["q_gain"][i].reshape(1, HEAD_DIM), grp["attn_tile"], grp["inner"])
    flat = x.reshape(1, -1, D_MODEL) if row_stride == 0 else x
    tile = min(FFN_TILE, flat.shape[1])
    casts = []
    if l + 1 < DEPTH and l + 1 not in ffn_w:
        steps = flat.shape[0] * (flat.shape[1] // tile)
        if ride and all(_can_ride(w, steps) for w in ffn_f32):
            casts = [(w, l + 1, "plain") for w in ffn_f32]
        else:
            ffn_w[l + 1] = tuple(w[l + 1].astype(BF16) for w in ffn_f32)
    flat, cast_out = _ffn_sublayer(flat, mod, l, row0, row_stride, g2, *ffn_w[l],
                                   wts["final_g"].reshape(1, D_MODEL), l == DEPTH - 1, tile, casts)
    if casts:
        ffn_w[l + 1] = tuple(cast_out)
    return flat.reshape(x.shape), new_kv


def kernel(x_prompt, x_sample, cache_k, cache_v, c, c_ctx, w_ada, b_ada, norm_g, w_in_even, pool_w,
           pool_scale, conv_w, conv_b, w_out_even, w_qkv, q_gain, k_gain, w_o, w_ffn_in, w_ffn_out,
           final_g):
    n_ctx, ctx_len, _ = x_prompt.shape
    n_lat, lat_len, _ = x_sample.shape
    assert 1 + n_lat <= COND_ROWS

    cond = jnp.concatenate(
        [c_ctx[None, :], c, jnp.zeros((COND_ROWS - 1 - n_lat, D_MODEL), F32)], axis=0)
    mod = _ada(cond, w_ada, b_ada).reshape(DEPTH, COND_ROWS, 6, D_MODEL)

    wts = dict(
        norm_g=norm_g, final_g=final_g, pool_scale=pool_scale, conv_w=conv_w, conv_b=conv_b,
        q_gain=q_gain, k_gain=k_gain,
        w_in_even=w_in_even.astype(BF16), pool_w=pool_w.astype(BF16),
        w_out_even=w_out_even.astype(BF16),
        q_gain_rope=_rope_lane_order(q_gain), k_gain_rope=_rope_lane_order(k_gain),
    )

    pack = math.gcd(n_ctx, max(1, CTX_PACK_ROWS // ctx_len))
    ctx_grp = dict(row0=0, row_stride=0, cache=None, rope_tables=None, attn_tile=pack * ctx_len,
                   inner=ctx_len)
    lat_grp = dict(row0=1, row_stride=1, cache=(cache_k, cache_v), rope_tables=_rope_tables(lat_len),
                   attn_tile=ATTN_TILE, inner=None)
    ffn_w, ffn_f32 = {}, (w_ffn_in, w_ffn_out)
    odd_w, odd_f32 = {}, (w_qkv, w_o)
    y_prompt = x_prompt.reshape(n_ctx // pack, pack * ctx_len, D_MODEL)
    y_sample, ctx_kv = x_sample, []
    for l in range(DEPTH):
        y_sample, _ = _run_layer(y_sample, l, mod, lat_grp, wts, ffn_w, ffn_f32, odd_w, odd_f32, True)
        y_prompt, kv = _run_layer(y_prompt, l, mod, ctx_grp, wts, ffn_w, ffn_f32, odd_w, odd_f32,
                                  False)
        if kv is not None:
            ctx_kv.append(kv)
    kv_shape = (n_ctx, 1, ctx_len, N_KV_HEADS, HEAD_DIM)
    new_cache_k = jnp.concatenate([k.reshape(kv_shape) for k, _ in ctx_kv], axis=1)
    new_cache_v = jnp.concatenate([v.reshape(kv_shape) for _, v in ctx_kv], axis=1)
    return y_prompt.reshape(x_prompt.shape), y_sample, new_cache_k, new_cache_v
```

```python
import functools
import math

import jax
import jax.numpy as jnp
import numpy as np
from jax import lax
from jax.experimental import pallas as pl
from jax.experimental.pallas import tpu as pltpu

D_MODEL = 1024
DEPTH = 2
GRID_W = 64
EPS = 1e-6
POOL_WIDTH = D_MODEL // 2
POOL_GROUPS = 4
POOL_GROUP_DIM = POOL_WIDTH // POOL_GROUPS
POOL_WINDOWS = (2, 4, 8, 16)
CONV_WIDTH = D_MODEL // 2
EVEN_IN = POOL_WIDTH + 3 * CONV_WIDTH
HEAD_DIM = 128
N_HEADS = D_MODEL // HEAD_DIM
N_KV_HEADS = N_HEADS // 4
QKV_WIDTH = (N_HEADS + 2 * N_KV_HEADS) * HEAD_DIM
ROPE_THETA = 10000.0
AXIS_DIM = HEAD_DIM // 2
D_FF = ((8 * D_MODEL // 3 + 255) // 256) * 256
SM_SCALE = HEAD_DIM ** -0.5
LOG2E = math.log2(math.e)

COND_ROWS = 8
HALO = 16
ADA_BLOCK_N = 3072
FF_CHUNKS = ((0, 1024), (1024, 1024), (2048, 768))
CTX_PACK_ROWS = 1024
QKV_TILE = 1024
QKV_SUB_ROWS = 128
EVEN_TILE = 1024
EVEN_SUB_ROWS = 512
FFN_TILE = 1024
FFN_SUB_ROWS = 256
FFN_LOOKAHEAD = 1
ATTN_TILE = 256
ATTN_KEY_CHUNK = 512
ATTN_LOOKAHEAD = 2
BOUND_SLACK = 1.1
MAX_SAFE_SHIFT = 60.0
V7X_VMEM_BYTES = 64 * 1024 * 1024
VMEM_LIMIT_BYTES = V7X_VMEM_BYTES - 8 * 1024 * 1024

BF16 = jnp.bfloat16
F32 = jnp.float32


def _dot(a, b):
    return jnp.dot(a, b, preferred_element_type=F32)


def _dot_nt(a, b):
    return lax.dot_general(a, b, (((1,), (1,)), ((), ())), preferred_element_type=F32)


def _silu(a):
    return a * (1.0 / (1.0 + jnp.exp(-a)))


def _rms(x):
    return x * lax.rsqrt(jnp.mean(x * x, axis=-1, keepdims=True) + EPS)


def _modulate(x, g, shift, scale):
    return (_rms(x) * g) * (1.0 + scale) + shift


def _params(sem):
    return pltpu.CompilerParams(dimension_semantics=sem, vmem_limit_bytes=VMEM_LIMIT_BYTES)


def _const_spec(shape):
    n = len(shape)
    return pl.BlockSpec(shape, lambda *_: (0,) * n, pipeline_mode=pl.Buffered(1))


def _riding_casts(casts, n_seq, n_t):
    steps = n_seq * n_t
    in_specs, out_specs, out_shapes = [], [], []
    for w, layer, _ in casts:
        rows = w.shape[1] // steps
        in_specs.append(pl.BlockSpec((None, rows, w.shape[2]),
                                     lambda s, t, layer=layer: (layer, s * n_t + t, 0)))
        out_specs.append(pl.BlockSpec((rows, w.shape[2]), lambda s, t: (s * n_t + t, 0)))
        out_shapes.append(jax.ShapeDtypeStruct(w.shape[1:], BF16))
    return in_specs, out_specs, out_shapes


def _can_ride(w, steps):
    return w.shape[1] % steps == 0 and (w.shape[1] // steps) % 16 == 0


def _rope_lane_order_cols(w):
    quarter_w = AXIS_DIM // 2
    lane = lax.broadcasted_iota(jnp.int32, w.shape, 1)
    quarter = (lane % HEAD_DIM) // quarter_w
    qk = lane < (N_HEADS + N_KV_HEADS) * HEAD_DIM
    from_right = pltpu.roll(w, w.shape[1] - quarter_w, 1)
    from_left = pltpu.roll(w, quarter_w, 1)
    swapped = jnp.where(quarter == 1, from_right, jnp.where(quarter == 2, from_left, w))
    return jnp.where(qk, swapped, w)


def _with_casts(body, n_in, n_out, kinds):
    n_cast = len(kinds)

    def run(*refs):
        ins, cast_ins = refs[:n_in], refs[n_in:n_in + n_cast]
        outs = refs[n_in + n_cast:n_in + n_cast + n_out]
        cast_outs = refs[n_in + n_cast + n_out:n_in + 2 * n_cast + n_out]
        for src, dst, kind in zip(cast_ins, cast_outs, kinds):
            w = src[...]
            dst[...] = (_rope_lane_order_cols(w) if kind == "rope" else w).astype(dst.dtype)
        body(*ins, *outs, *refs[n_in + 2 * n_cast + n_out:])
    return run


def _ada_kernel(cond_ref, w_ref, b_ref, o_ref):
    s = _silu(cond_ref[...]).astype(BF16)
    o_ref[...] = _dot(s, w_ref[...].astype(BF16)) + b_ref[...]


def _ada(cond, w_ada, b_ada):
    n_out = w_ada.shape[-1]
    return pl.pallas_call(
        _ada_kernel,
        out_shape=jax.ShapeDtypeStruct((DEPTH, COND_ROWS, n_out), F32),
        grid=(DEPTH, n_out // ADA_BLOCK_N),
        in_specs=[
            pl.BlockSpec((COND_ROWS, D_MODEL), lambda l, j: (0, 0)),
            pl.BlockSpec((None, D_MODEL, ADA_BLOCK_N), lambda l, j: (l, 0, j)),
            pl.BlockSpec((None, 1, ADA_BLOCK_N), lambda l, j: (l, 0, j)),
        ],
        out_specs=pl.BlockSpec((None, COND_ROWS, ADA_BLOCK_N), lambda l, j: (l, 0, j)),
        compiler_params=_params(("arbitrary", "arbitrary")),
        name="ada",
    )(cond, w_ada, b_ada.reshape(DEPTH, 1, n_out))


def _mod_spec(layer, row0, row_stride):
    return pl.BlockSpec((None, None, 6, D_MODEL),
                        lambda s, t: (layer, row0 + row_stride * s, 0, 0))


def _even_kernel(x_ref, xp_ref, xn_ref, mod_ref, g_ref, w_in_ref, pool_w_ref, pool_scale_ref,
                 conv_w_ref, conv_b_ref, w_out_ref, o_ref, *, tile, seq_len, packed):
    t = pl.program_id(1)
    n_t = pl.num_programs(1)
    sub = seq_len if packed else min(EVEN_SUB_ROWS, tile)
    n_sub = tile // sub
    win = sub + 2 * HALO
    mod = mod_ref[...]
    shift, scale, gate = mod[0:1], mod[1:2], mod[2:3]
    g = g_ref[...]
    cw = conv_w_ref[...]

    def h_rows(lo, hi):
        pieces = []
        if lo < HALO:
            pieces.append(xp_ref[...])
        pieces.append(x_ref[max(lo, HALO) - HALO:min(hi, HALO + tile) - HALO, :])
        if hi > HALO + tile:
            pieces.append(xn_ref[...])
        return jnp.concatenate([_modulate(r, g, shift, scale).astype(BF16) for r in pieces], axis=0)

    def project(s):
        if packed:
            rows = x_ref[s * sub:(s + 1) * sub, :]
            return _dot(_modulate(rows, g, shift, scale).astype(BF16), w_in_ref[...])
        lo = 0 if s == 0 else s * sub + 2 * HALO
        return _dot(h_rows(lo, (s + 1) * sub + 2 * HALO), w_in_ref[...])

    def edge_masked(a, s):
        if packed:
            return a
        parts = [a[:HALO], a[HALO:HALO + sub], a[HALO + sub:]]
        if s == 0:
            parts[0] = jnp.where(t > 0, parts[0], 0.0)
        if s == n_sub - 1:
            parts[2] = jnp.where(t < n_t - 1, parts[2], 0.0)
        return jnp.concatenate(parts, axis=0)

    def shifted(a, k):
        return pltpu.roll(a, k % win, 0)

    def mixers(s, p):
        first = 0 if packed else t * tile + s * sub
        pos = first + lax.broadcasted_iota(jnp.int32, (sub, 1), 0)
        u = edge_masked(p[:, :POOL_WIDTH], s)
        ya = []
        for i, w in enumerate(POOL_WINDOWS):
            ug = u[:, i * POOL_GROUP_DIM:(i + 1) * POOL_GROUP_DIM]
            fwd, n = ug, 1
            while n < w // 2:
                fwd = fwd + shifted(fwd, -n)
                n *= 2
            wsum = shifted(fwd, w // 2) + fwd
            cnt = (jnp.clip(pos + w // 2, 0, seq_len)
                   - jnp.clip(pos - w // 2, 0, seq_len)).astype(F32)
            pooled = wsum[HALO:HALO + sub] / cnt - ug[HALO:HALO + sub]
            ya.append(_dot(pooled.astype(BF16), pool_w_ref[i]))
        ya = jnp.concatenate(ya, axis=-1) * pool_scale_ref[...]
        c0 = POOL_WIDTH
        bg = p[HALO:HALO + sub, c0:c0 + CONV_WIDTH]
        z = edge_masked(p[:, c0 + CONV_WIDTH:c0 + 2 * CONV_WIDTH] * p[:, c0 + 2 * CONV_WIDTH:], s)
        conv = (shifted(z, 1) * cw[0:1] + z * cw[1:2] + shifted(z, -1) * cw[2:3])[HALO:HALO + sub]
        yb = bg * (conv + conv_b_ref[...])
        y = _dot(jnp.concatenate([ya, yb], axis=-1).astype(BF16), w_out_ref[...])
        rows = slice(s * sub, (s + 1) * sub)
        o_ref[rows, :] = x_ref[rows, :] + gate * y

    block = project(0)
    window = None
    margin = jnp.zeros((HALO, EVEN_IN), F32)
    for s in range(n_sub):
        if packed:
            window = jnp.concatenate([margin, block, margin], axis=0)
        else:
            window = block if s == 0 else jnp.concatenate([window[sub:], block], axis=0)
        if s + 1 < n_sub:
            block = project(s + 1)
        mixers(s, window)


def _even_sublayer(x, mod, layer, row0, row_stride, g, w_in, pool_w, pool_scale, conv_w, conv_b,
                   w_out, tile, inner=None, casts=()):
    n_seq, n_rows, _ = x.shape
    packed = inner is not None and inner < n_rows
    assert not packed or tile % inner == 0
    n_t = n_rows // tile
    hb = tile // HALO
    n_hb = n_rows // HALO
    cast_in, cast_out, cast_shapes = _riding_casts(casts, n_seq, n_t)
    kern = _with_casts(functools.partial(_even_kernel, tile=tile, packed=packed,
                                         seq_len=inner if packed else n_rows), 11, 1,
                       [kind for _, _, kind in casts])
    res = pl.pallas_call(
        kern,
        out_shape=(jax.ShapeDtypeStruct(x.shape, F32), *cast_shapes),
        grid=(n_seq, n_t),
        in_specs=[
            pl.BlockSpec((None, tile, D_MODEL), lambda s, t: (s, t, 0)),
            pl.BlockSpec((None, HALO, D_MODEL), lambda s, t: (s, jnp.maximum(t * hb - 1, 0), 0)),
            pl.BlockSpec((None, HALO, D_MODEL),
                         lambda s, t: (s, jnp.minimum((t + 1) * hb, n_hb - 1), 0)),
            _mod_spec(layer, row0, row_stride),
            _const_spec((1, D_MODEL)),
            _const_spec((D_MODEL, EVEN_IN)),
            _const_spec((POOL_GROUPS, POOL_GROUP_DIM, POOL_GROUP_DIM)),
            _const_spec((1, POOL_WIDTH)),
            _const_spec((3, CONV_WIDTH)),
            _const_spec((1, CONV_WIDTH)),
            _const_spec((POOL_WIDTH + CONV_WIDTH, D_MODEL)),
            *cast_in,
        ],
        out_specs=(pl.BlockSpec((None, tile, D_MODEL), lambda s, t: (s, t, 0)), *cast_out),
        compiler_params=_params(("arbitrary", "arbitrary")),
        name="even_mixer",
    )(x, x, x, mod, g, w_in, pool_w, pool_scale, conv_w, conv_b, w_out, *[c[0] for c in casts])
    return res[0], list(res[1:])


def _ffn_kernel(x_ref, mod_ref, g_ref, w_in_ref, w_out_ref, fg_ref, o_ref, *, final_norm):
    mod = mod_ref[...]
    shift, scale, gate = mod[3:4], mod[4:5], mod[5:6]
    g = g_ref[...]
    n_sub = x_ref.shape[0] // FFN_SUB_ROWS
    items = [(sub, c) for sub in range(n_sub) for c in range(len(FF_CHUNKS))]
    hs, ups = {}, {}

    def issue_up(j):
        sub, c = items[j]
        if c == 0:
            rows = x_ref[sub * FFN_SUB_ROWS:(sub + 1) * FFN_SUB_ROWS, :]
            hs[sub] = _modulate(rows, g, shift, scale).astype(BF16)
        c0, cw = FF_CHUNKS[c]
        ups[j] = (_dot(hs[sub], w_in_ref[:, c0:c0 + cw]),
                  _dot(hs[sub], w_in_ref[:, D_FF + c0:D_FF + c0 + cw]))

    for j in range(min(FFN_LOOKAHEAD, len(items))):
        issue_up(j)
    for i, (sub, c) in enumerate(items):
        if i + FFN_LOOKAHEAD < len(items):
            issue_up(i + FFN_LOOKAHEAD)
        a, b = ups.pop(i)
        c0, cw = FF_CHUNKS[c]
        part = _dot((_silu(a) * b).astype(BF16), w_out_ref[c0:c0 + cw, :])
        acc = part if c == 0 else acc + part
        if c == len(FF_CHUNKS) - 1:
            rows = slice(sub * FFN_SUB_ROWS, (sub + 1) * FFN_SUB_ROWS)
            out = x_ref[rows, :] + gate * acc
            if final_norm:
                out = _rms(out) * fg_ref[...]
            o_ref[rows, :] = out


def _ffn_sublayer(x, mod, layer, row0, row_stride, g, w_in, w_out, final_g, final_norm, tile,
                  casts=()):
    n_seq, seq_len, _ = x.shape
    cast_in, cast_out, cast_shapes = _riding_casts(casts, n_seq, seq_len // tile)
    kern = _with_casts(functools.partial(_ffn_kernel, final_norm=final_norm), 6, 1,
                       [kind for _, _, kind in casts])
    res = pl.pallas_call(
        kern,
        out_shape=(jax.ShapeDtypeStruct(x.shape, F32), *cast_shapes),
        grid=(n_seq, seq_len // tile),
        in_specs=[
            pl.BlockSpec((None, tile, D_MODEL), lambda s, t: (s, t, 0)),
            _mod_spec(layer, row0, row_stride),
            _const_spec((1, D_MODEL)),
            _const_spec((D_MODEL, 2 * D_FF)),
            _const_spec((D_FF, D_MODEL)),
            _const_spec((1, D_MODEL)),
            *cast_in,
        ],
        out_specs=(pl.BlockSpec((None, tile, D_MODEL), lambda s, t: (s, t, 0)), *cast_out),
        compiler_params=_params(("arbitrary", "arbitrary")),
        name="ffn",
    )(x, mod, g, w_in, w_out, final_g, *[c[0] for c in casts])
    return res[0], list(res[1:])


def _rope_lane_order(a):
    lead = a.shape[:-1]
    return a.reshape(*lead, 2, 2, AXIS_DIM // 2).swapaxes(-3, -2).reshape(*lead, HEAD_DIM)


def _rope(a, cos, sin_signed):
    return a * cos + pltpu.roll(a, HEAD_DIM // 2, 1) * sin_signed


def _pair_mean_matrix():
    r = lax.broadcasted_iota(jnp.int32, (2 * HEAD_DIM, 2 * HEAD_DIM), 0) // HEAD_DIM
    c = lax.broadcasted_iota(jnp.int32, (2 * HEAD_DIM, 2 * HEAD_DIM), 1) // HEAD_DIM
    return jnp.where(r == c, 1.0 / HEAD_DIM, 0.0).astype(BF16)


def _rope_lane_matrix():
    src = lax.broadcasted_iota(jnp.int32, (HEAD_DIM, HEAD_DIM), 0)
    n = lax.broadcasted_iota(jnp.int32, (HEAD_DIM, HEAD_DIM), 1)
    quarter = AXIS_DIM // 2
    half, axis, c = n // AXIS_DIM, (n // quarter) % 2, n % quarter
    return jnp.where(src == axis * AXIS_DIM + half * quarter + c, 1.0, 0.0).astype(BF16)


def _qkv_kernel(*refs, rope, cached, keep_f32):
    it = iter(refs)
    x_ref, mod_ref, g_ref, w_ref, qg_ref, kg_ref = (next(it) for _ in range(6))
    rope_refs = [next(it) for _ in range(2)] if rope else None
    cache_refs = [next(it) for _ in range(2)] if cached else None
    q_ref, kh_ref, vt_ref = (next(it) for _ in range(3))
    cache_out_refs = [next(it) for _ in range(2)] if cached else None
    f32_refs = [next(it) for _ in range(2)] if keep_f32 else None

    if cached:
        @pl.when(pl.program_id(1) == 0)
        def _():
            for kv in range(N_KV_HEADS):
                k = cache_refs[0][:, kv, :].astype(BF16)
                if rope:
                    k = _dot(k, _rope_lane_matrix()).astype(BF16)
                cache_out_refs[0][kv] = k
                cache_out_refs[1][kv] = cache_refs[1][:, kv, :].T.astype(BF16)

    mod = mod_ref[...]
    shift, scale = mod[0:1], mod[1:2]
    g = g_ref[...]
    gains = [qg_ref[...] * (SM_SCALE * LOG2E)] * N_HEADS + [kg_ref[...]] * N_KV_HEADS
    pair_mean = _pair_mean_matrix()
    v0 = (N_HEADS + N_KV_HEADS) * HEAD_DIM
    sub_rows = min(QKV_SUB_ROWS, x_ref.shape[0])
    n_sub = x_ref.shape[0] // sub_rows

    def rows_of(sub):
        return slice(sub * sub_rows, (sub + 1) * sub_rows)

    def projection(sub):
        h = _modulate(x_ref[rows_of(sub), :], g, shift, scale).astype(BF16)
        return _dot(h, w_ref[...])

    def finish(sub, p):
        rows = rows_of(sub)
        if rope:
            tables = [r[rows, :] for r in rope_refs]
        for pair in range((N_HEADS + N_KV_HEADS) // 2):
            blk = p[:, pair * 2 * HEAD_DIM:(pair + 1) * 2 * HEAD_DIM]
            ms = _dot((blk * blk).astype(BF16), pair_mean)
            normed = blk * lax.rsqrt(ms + EPS)
            for half in range(2):
                hd = 2 * pair + half
                a = normed[:, half * HEAD_DIM:(half + 1) * HEAD_DIM] * gains[hd]
                if rope:
                    a = _rope(a, *tables)
                if hd < N_HEADS:
                    q_ref[hd, rows, :] = a.astype(BF16)
                else:
                    kv = hd - N_HEADS
                    v = p[:, v0 + kv * HEAD_DIM:v0 + (kv + 1) * HEAD_DIM]
                    kh_ref[kv, rows, :] = a.astype(BF16)
                    vt_ref[kv, :, rows] = v.T.astype(BF16)
                    if keep_f32:
                        f32_refs[0][rows, kv, :] = a
                        f32_refs[1][rows, kv, :] = v

    p_next = projection(0)
    for sub in range(n_sub):
        p_cur = p_next
        if sub + 1 < n_sub:
            p_next = projection(sub + 1)
        finish(sub, p_cur)


def _qkv_proj(x, mod, layer, row0, row_stride, g, w_qkv, q_gain, k_gain, rope_tables, cache,
              cache_layer, keep_f32, tile):
    n_seq, seq_len, _ = x.shape
    rope = rope_tables is not None
    cached = cache is not None
    in_specs = [
        pl.BlockSpec((None, tile, D_MODEL), lambda s, t: (s, t, 0)),
        _mod_spec(layer, row0, row_stride),
        _const_spec((1, D_MODEL)),
        _const_spec((D_MODEL, QKV_WIDTH)),
        _const_spec((1, HEAD_DIM)),
        _const_spec((1, HEAD_DIM)),
    ]
    args = [x, mod, g, w_qkv, q_gain, k_gain]
    if rope:
        in_specs += [pl.BlockSpec((tile, HEAD_DIM), lambda s, t: (t, 0))] * 2
        args += list(rope_tables)
    out_shape = [jax.ShapeDtypeStruct((n_seq, N_HEADS, seq_len, HEAD_DIM), BF16),
                 jax.ShapeDtypeStruct((n_seq, N_KV_HEADS, seq_len, HEAD_DIM), BF16),
                 jax.ShapeDtypeStruct((n_seq, N_KV_HEADS, HEAD_DIM, seq_len), BF16)]
    out_specs = [pl.BlockSpec((None, N_HEADS, tile, HEAD_DIM), lambda s, t: (s, 0, t, 0)),
                 pl.BlockSpec((None, N_KV_HEADS, tile, HEAD_DIM), lambda s, t: (s, 0, t, 0)),
                 pl.BlockSpec((None, N_KV_HEADS, HEAD_DIM, tile), lambda s, t: (s, 0, 0, t))]
    if cached:
        past = cache[0].shape[2]
        in_specs += [pl.BlockSpec((None, None, past, N_KV_HEADS, HEAD_DIM),
                                  lambda s, t: (s, cache_layer, 0, 0, 0))] * 2
        args += list(cache)
        out_shape += [jax.ShapeDtypeStruct((n_seq, N_KV_HEADS, past, HEAD_DIM), BF16),
                      jax.ShapeDtypeStruct((n_seq, N_KV_HEADS, HEAD_DIM, past), BF16)]
        out_specs += [pl.BlockSpec((None, N_KV_HEADS, past, HEAD_DIM), lambda s, t: (s, 0, 0, 0)),
                      pl.BlockSpec((None, N_KV_HEADS, HEAD_DIM, past), lambda s, t: (s, 0, 0, 0))]
    if keep_f32:
        out_shape += [jax.ShapeDtypeStruct((n_seq, 1, seq_len, N_KV_HEADS, HEAD_DIM), F32)] * 2
        out_specs += [pl.BlockSpec((None, None, tile, N_KV_HEADS, HEAD_DIM),
                                   lambda s, t: (s, 0, t, 0, 0))] * 2
    return pl.pallas_call(
        functools.partial(_qkv_kernel, rope=rope, cached=cached, keep_f32=keep_f32),
        out_shape=tuple(out_shape),
        grid=(n_seq, seq_len // tile),
        in_specs=in_specs,
        out_specs=tuple(out_specs),
        compiler_params=_params(("arbitrary", "arbitrary")),
        name="qkv_proj",
    )(*args)


def _attn_kernel(*refs, n_segments, key_chunk, inner):
    q_ref = refs[0]
    segments = [(refs[1 + 2 * i], refs[2 + 2 * i]) for i in range(n_segments)]
    x_ref, mod_ref, w_o_ref, qg_ref, o_ref, heads_scr, shift_scr, safe_scr = refs[1 + 2 * n_segments:]
    group = N_HEADS // N_KV_HEADS
    tile = q_ref.shape[1]
    q_rows = tile if inner is None else inner
    n_parts = tile // q_rows

    def chunks_of(part):
        if inner is None:
            return [(seg, c0, min(key_chunk, k_ref.shape[1] - c0))
                    for seg, (k_ref, _) in enumerate(segments)
                    for c0 in range(0, k_ref.shape[1], key_chunk)]
        return [(0, part * inner + c0, min(key_chunk, inner - c0))
                for c0 in range(0, inner, key_chunk)]

    @pl.when(pl.program_id(1) == 0)
    def _():
        qg = qg_ref[...]
        q_sq = HEAD_DIM * (SM_SCALE * LOG2E) ** 2 * jnp.max(qg * qg)
        for kv in range(N_KV_HEADS):
            k_sq = None
            for k_ref, _ in segments:
                k = k_ref[kv].astype(F32)
                seg_sq = jnp.max(jnp.sum(k * k, axis=1, keepdims=True))
                k_sq = seg_sq if k_sq is None else jnp.maximum(k_sq, seg_sq)
            bound_sq = q_sq * k_sq * BOUND_SLACK
            shift_scr[kv] = jnp.full(shift_scr.shape[1:], jnp.sqrt(bound_sq), F32)
            safe_scr[kv] = (bound_sq <= MAX_SAFE_SHIFT ** 2).astype(jnp.int32)

    def bounded():
        items = [(part, hd, ci, chunk, ci == len(chunks_of(part)) - 1)
                 for part in range(n_parts) for hd in range(N_HEADS)
                 for ci, chunk in enumerate(chunks_of(part))]
        scores = {}

        def issue_scores(j):
            part, hd, _, (seg, c0, rows), _ = items[j]
            keys = segments[seg][0][hd // group, c0:c0 + rows, :]
            q = q_ref[hd, part * q_rows:(part + 1) * q_rows, :]
            scores[j] = _dot_nt(keys, q)

        for j in range(min(ATTN_LOOKAHEAD, len(items))):
            issue_scores(j)
        for i, (part, hd, ci, (seg, c0, rows), last) in enumerate(items):
            if i + ATTN_LOOKAHEAD < len(items):
                issue_scores(i + ATTN_LOOKAHEAD)
            p = jnp.exp2(scores.pop(i) - shift_scr[hd // group][0:1, 0:1])
            ps = jnp.sum(p, axis=0, keepdims=True)
            pv = _dot(segments[seg][1][hd // group, :, c0:c0 + rows], p.astype(BF16))
            denom, acc = (ps, pv) if ci == 0 else (denom + ps, acc + pv)
            if last:
                heads_scr[hd, part * q_rows:(part + 1) * q_rows, :] = (
                    (acc * (1.0 / denom)).T.astype(BF16))

    def exact():
        def one_head(hd, carry):
            kv = hd // group
            for part in range(n_parts):
                rows_q = slice(part * q_rows, (part + 1) * q_rows)
                q = q_ref[hd, rows_q, :]
                if inner is None:
                    key_sets = [(k_ref[kv], vt_ref[kv]) for k_ref, vt_ref in segments]
                else:
                    keys = slice(part * inner, (part + 1) * inner)
                    key_sets = [(segments[0][0][kv, keys, :], segments[0][1][kv, :, keys])]
                scores = [_dot_nt(k, q) for k, _ in key_sets]
                m = None
                for s in scores:
                    sm = jnp.max(s, axis=0, keepdims=True)
                    m = sm if m is None else jnp.maximum(m, sm)
                denom = o_t = None
                for s, (_, vt) in zip(scores, key_sets):
                    p = jnp.exp2(s - m)
                    ps = jnp.sum(p, axis=0, keepdims=True)
                    pv = _dot(vt, p.astype(BF16))
                    denom, o_t = (ps, pv) if denom is None else (denom + ps, o_t + pv)
                heads_scr[hd, rows_q, :] = (o_t * (1.0 / denom)).T.astype(BF16)
            return carry

        lax.fori_loop(0, N_HEADS, one_head, 0)

    safe = safe_scr[0]
    for kv in range(1, N_KV_HEADS):
        safe = jnp.minimum(safe, safe_scr[kv])
    pl.when(safe == 1)(bounded)
    pl.when(safe != 1)(exact)

    attn = jnp.concatenate([heads_scr[hd] for hd in range(N_HEADS)], axis=-1)
    o_ref[...] = x_ref[...] + mod_ref[2:3] * _dot(attn, w_o_ref[...])


def _attn_sublayer(q, key_segments, x, mod, layer, row0, row_stride, w_o, q_gain, tile, inner=None):
    n_seq, seq_len, _ = x.shape
    if inner is not None and inner == seq_len:
        inner = None
    assert inner is None or (len(key_segments) == 1 and tile % inner == 0)
    in_specs = [pl.BlockSpec((None, N_HEADS, tile, HEAD_DIM), lambda s, t: (s, 0, t, 0))]
    args = [q]
    for k, vt in key_segments:
        n_keys = k.shape[2]
        in_specs += [pl.BlockSpec((None, N_KV_HEADS, n_keys, HEAD_DIM), lambda s, t: (s, 0, 0, 0)),
                     pl.BlockSpec((None, N_KV_HEADS, HEAD_DIM, n_keys), lambda s, t: (s, 0, 0, 0))]
        args += [k, vt]
    in_specs += [
        pl.BlockSpec((None, tile, D_MODEL), lambda s, t: (s, t, 0)),
        _mod_spec(layer, row0, row_stride),
        _const_spec((D_MODEL, D_MODEL)),
        _const_spec((1, HEAD_DIM)),
    ]
    args += [x, mod, w_o, q_gain]
    return pl.pallas_call(
        functools.partial(_attn_kernel, n_segments=len(key_segments), key_chunk=ATTN_KEY_CHUNK,
                          inner=inner),
        out_shape=jax.ShapeDtypeStruct(x.shape, F32),
        grid=(n_seq, seq_len // tile),
        in_specs=in_specs,
        out_specs=pl.BlockSpec((None, tile, D_MODEL), lambda s, t: (s, t, 0)),
        scratch_shapes=[pltpu.VMEM((N_HEADS, tile, HEAD_DIM), BF16),
                        pltpu.VMEM((N_KV_HEADS, 8, HEAD_DIM), F32),
                        pltpu.SMEM((N_KV_HEADS,), jnp.int32)],
        compiler_params=_params(("arbitrary", "arbitrary")),
        name="attention",
    )(*args)


def _rope_tables(seq_len):
    rows = seq_len // GRID_W
    r = np.repeat(np.arange(rows), GRID_W).astype(np.float32)
    col = np.tile(np.arange(GRID_W), rows).astype(np.float32)
    expo = -np.arange(0, AXIS_DIM, 2, dtype=np.float32) / np.float32(AXIS_DIM)
    inv = np.power(np.float32(ROPE_THETA), expo).astype(np.float32)
    ar = r[:, None] * inv
    ac = col[:, None] * inv
    ang = np.concatenate([ar, ar, ac, ac], axis=-1)
    cos, sin = np.cos(ang), np.sin(ang)
    first = (np.arange(HEAD_DIM) % AXIS_DIM) < AXIS_DIM // 2
    return (jnp.asarray(_rope_lane_order(cos), F32),
            jnp.asarray(_rope_lane_order(np.where(first, -sin, sin)), F32))


def _rope_weights(w_qkv):
    qk = (N_HEADS + N_KV_HEADS) * HEAD_DIM
    lead = w_qkv.shape[:-1]
    w_qk = _rope_lane_order(w_qkv[..., :qk].reshape(*lead, N_HEADS + N_KV_HEADS, HEAD_DIM))
    return jnp.concatenate([w_qk.reshape(*lead, qk), w_qkv[..., qk:]], axis=-1)


def _run_layer(x, l, mod, grp, wts, ffn_w, ffn_f32, odd_w, odd_f32, ride):
    row0, row_stride = grp["row0"], grp["row_stride"]
    i = l // 2
    g1 = wts["norm_g"][l, 0].reshape(1, D_MODEL)
    g2 = wts["norm_g"][l, 1].reshape(1, D_MODEL)
    new_kv = None
    if l % 2 == 0:
        tile = min(EVEN_TILE, x.shape[1])
        casts = []
        if l == 0 and 0 not in ffn_w:
            steps = x.shape[0] * (x.shape[1] // tile)
            if ride and all(_can_ride(w, steps) for w in ffn_f32 + odd_f32):
                casts = [(w, 0, "plain") for w in ffn_f32]
                for i_odd in range(DEPTH // 2):
                    casts += [(odd_f32[0], i_odd, "plain"), (odd_f32[0], i_odd, "rope"),
                              (odd_f32[1], i_odd, "plain")]
            else:
                ffn_w[0] = tuple(w[0].astype(BF16) for w in ffn_f32)
                for i_odd in range(DEPTH // 2):
                    w_qkv_bf16 = odd_f32[0][i_odd].astype(BF16)
                    odd_w[i_odd] = (w_qkv_bf16, _rope_weights(w_qkv_bf16),
                                    odd_f32[1][i_odd].astype(BF16))
        x, cast_out = _even_sublayer(x, mod, l, row0, row_stride, g1, wts["w_in_even"][i],
                                     wts["pool_w"][i], wts["pool_scale"][i].reshape(1, POOL_WIDTH),
                                     wts["conv_w"][i], wts["conv_b"][i].reshape(1, CONV_WIDTH),
                                     wts["w_out_even"][i], tile, grp["inner"], casts)
        if casts:
            ffn_w[0] = tuple(cast_out[:2])
            for i_odd in range(DEPTH // 2):
                odd_w[i_odd] = tuple(cast_out[2 + 3 * i_odd:5 + 3 * i_odd])
    else:
        cache, rope_tables = grp["cache"], grp["rope_tables"]
        sfx = "" if rope_tables is None else "_rope"
        w_qkv, w_qkv_rope, w_o = odd_w[i]
        res = _qkv_proj(x, mod, l, row0, row_stride, g1, w_qkv if rope_tables is None else w_qkv_rope,
                        wts["q_gain" + sfx][i].reshape(1, HEAD_DIM),
                        wts["k_gain" + sfx][i].reshape(1, HEAD_DIM), rope_tables, cache, i,
                        cache is None, min(QKV_TILE, x.shape[1]))
        q, segments = res[0], [res[1:3]]
        if cache is None:
            new_kv = res[3:]
        else:
            segments.append(res[3:5])
        x = _attn_sublayer(q, segments, x, mod, l, row0, row_stride, w_o,
                           wts["q_gain"][i].reshape(1, HEAD_DIM), grp["attn_tile"], grp["inner"])
    flat = x.reshape(1, -1, D_MODEL) if row_stride == 0 else x
    tile = min(FFN_TILE, flat.shape[1])
    casts = []
    if l + 1 < DEPTH and l + 1 not in ffn_w:
        steps = flat.shape[0] * (flat.shape[1] // tile)
        if ride and all(_can_ride(w, steps) for w in ffn_f32):
            casts = [(w, l + 1, "plain") for w in ffn_f32]
        else:
            ffn_w[l + 1] = tuple(w[l + 1].astype(BF16) for w in ffn_f32)
    flat, cast_out = _ffn_sublayer(flat, mod, l, row0, row_stride, g2, *ffn_w[l],
                                   wts["final_g"].reshape(1, D_MODEL), l == DEPTH - 1, tile, casts)
    if casts:
        ffn_w[l + 1] = tuple(cast_out)
    return flat.reshape(x.shape), new_kv


def kernel(x_prompt, x_sample, cache_k, cache_v, c, c_ctx, w_ada, b_ada, norm_g, w_in_even, pool_w,
           pool_scale, conv_w, conv_b, w_out_even, w_qkv, q_gain, k_gain, w_o, w_ffn_in, w_ffn_out,
           final_g):
    n_ctx, ctx_len, _ = x_prompt.shape
    n_lat, lat_len, _ = x_sample.shape
    assert 1 + n_lat <= COND_ROWS

    cond = jnp.concatenate(
        [c_ctx[None, :], c, jnp.zeros((COND_ROWS - 1 - n_lat, D_MODEL), F32)], axis=0)
    mod = _ada(cond, w_ada, b_ada).reshape(DEPTH, COND_ROWS, 6, D_MODEL)

    wts = dict(
        norm_g=norm_g, final_g=final_g, pool_scale=pool_scale, conv_w=conv_w, conv_b=conv_b,
        q_gain=q_gain, k_gain=k_gain,
        w_in_even=w_in_even.astype(BF16), pool_w=pool_w.astype(BF16),
        w_out_even=w_out_even.astype(BF16),
        q_gain_rope=_rope_lane_order(q_gain), k_gain_rope=_rope_lane_order(k_gain),
    )

    pack = math.gcd(n_ctx, max(1, CTX_PACK_ROWS // ctx_len))
    ctx_grp = dict(row0=0, row_stride=0, cache=None, rope_tables=None, attn_tile=pack * ctx_len,
                   inner=ctx_len)
    lat_grp = dict(row0=1, row_stride=1, cache=(cache_k, cache_v), rope_tables=_rope_tables(lat_len),
                   attn_tile=ATTN_TILE, inner=None)
    ffn_w, ffn_f32 = {}, (w_ffn_in, w_ffn_out)
    odd_w, odd_f32 = {}, (w_qkv, w_o)
    y_prompt = x_prompt.reshape(n_ctx // pack, pack * ctx_len, D_MODEL)
    y_sample, ctx_kv = x_sample, []
    for l in range(DEPTH):
        y_sample, _ = _run_layer(y_sample, l, mod, lat_grp, wts, ffn_w, ffn_f32, odd_w, odd_f32, True)
        y_prompt, kv = _run_layer(y_prompt, l, mod, ctx_grp, wts, ffn_w, ffn_f32, odd_w, odd_f32,
                                  False)
        if kv is not None:
            ctx_kv.append(kv)
    kv_shape = (n_ctx, 1, ctx_len, N_KV_HEADS, HEAD_DIM)
    new_cache_k = jnp.concatenate([k.reshape(kv_shape) for k, _ in ctx_kv], axis=1)
    new_cache_v = jnp.concatenate([v.reshape(kv_shape) for _, v in ctx_kv], axis=1)
    return y_prompt.reshape(x_prompt.shape), y_sample, new_cache_k, new_cache_v
```

```python
import functools
import math

import jax
import jax.numpy as jnp
import numpy as np
from jax import lax
from jax.experimental import pallas as pl
from jax.experimental.pallas import tpu as pltpu

D_MODEL = 1024
DEPTH = 2
GRID_W = 64
EPS = 1e-6
POOL_WIDTH = D_MODEL // 2
POOL_GROUPS = 4
POOL_GROUP_DIM = POOL_WIDTH // POOL_GROUPS
POOL_WINDOWS = (2, 4, 8, 16)
CONV_WIDTH = D_MODEL // 2
EVEN_IN = POOL_WIDTH + 3 * CONV_WIDTH
HEAD_DIM = 128
N_HEADS = D_MODEL // HEAD_DIM
N_KV_HEADS = N_HEADS // 4
QKV_WIDTH = (N_HEADS + 2 * N_KV_HEADS) * HEAD_DIM
ROPE_THETA = 10000.0
AXIS_DIM = HEAD_DIM // 2
D_FF = ((8 * D_MODEL // 3 + 255) // 256) * 256
SM_SCALE = HEAD_DIM ** -0.5
LOG2E = math.log2(math.e)

COND_ROWS = 8
HALO = 16
ADA_BLOCK_N = 3072
FF_CHUNKS = ((0, 1024), (1024, 1024), (2048, 768))
CTX_PACK_ROWS = 1024
QKV_TILE = 1024
QKV_SUB_ROWS = 128
EVEN_TILE = 1024
EVEN_SUB_ROWS = 512
FFN_TILE = 1024
FFN_SUB_ROWS = 256
FFN_LOOKAHEAD = 1
ATTN_TILE = 512
ATTN_Q_ROWS = 256
ATTN_KEY_CHUNK = 512
ATTN_LOOKAHEAD = 2
BOUND_SLACK = 1.1
MAX_SAFE_SHIFT = 60.0
V7X_VMEM_BYTES = 64 * 1024 * 1024
VMEM_LIMIT_BYTES = V7X_VMEM_BYTES - 8 * 1024 * 1024

BF16 = jnp.bfloat16
F32 = jnp.float32


def _dot(a, b):
    return jnp.dot(a, b, preferred_element_type=F32)


def _dot_nt(a, b):
    return lax.dot_general(a, b, (((1,), (1,)), ((), ())), preferred_element_type=F32)


def _silu(a):
    return a * (1.0 / (1.0 + jnp.exp(-a)))


def _rms(x):
    return x * lax.rsqrt(jnp.mean(x * x, axis=-1, keepdims=True) + EPS)


def _modulate(x, g, shift, scale):
    return (_rms(x) * g) * (1.0 + scale) + shift


def _params(sem):
    return pltpu.CompilerParams(dimension_semantics=sem, vmem_limit_bytes=VMEM_LIMIT_BYTES)


def _const_spec(shape):
    n = len(shape)
    return pl.BlockSpec(shape, lambda *_: (0,) * n, pipeline_mode=pl.Buffered(1))


def _riding_casts(casts, n_seq, n_t):
    steps = n_seq * n_t
    in_specs, out_specs, out_shapes = [], [], []
    for w, layer, _ in casts:
        rows = w.shape[1] // steps
        in_specs.append(pl.BlockSpec((None, rows, w.shape[2]),
                                     lambda s, t, layer=layer: (layer, s * n_t + t, 0)))
        out_specs.append(pl.BlockSpec((rows, w.shape[2]), lambda s, t: (s * n_t + t, 0)))
        out_shapes.append(jax.ShapeDtypeStruct(w.shape[1:], BF16))
    return in_specs, out_specs, out_shapes


def _can_ride(w, steps):
    return w.shape[1] % steps == 0 and (w.shape[1] // steps) % 16 == 0


def _rope_lane_order_cols(w):
    quarter_w = AXIS_DIM // 2
    lane = lax.broadcasted_iota(jnp.int32, w.shape, 1)
    quarter = (lane % HEAD_DIM) // quarter_w
    qk = lane < (N_HEADS + N_KV_HEADS) * HEAD_DIM
    from_right = pltpu.roll(w, w.shape[1] - quarter_w, 1)
    from_left = pltpu.roll(w, quarter_w, 1)
    swapped = jnp.where(quarter == 1, from_right, jnp.where(quarter == 2, from_left, w))
    return jnp.where(qk, swapped, w)


def _with_casts(body, n_in, n_out, kinds):
    n_cast = len(kinds)

    def run(*refs):
        ins, cast_ins = refs[:n_in], refs[n_in:n_in + n_cast]
        outs = refs[n_in + n_cast:n_in + n_cast + n_out]
        cast_outs = refs[n_in + n_cast + n_out:n_in + 2 * n_cast + n_out]
        for src, dst, kind in zip(cast_ins, cast_outs, kinds):
            w = src[...]
            dst[...] = (_rope_lane_order_cols(w) if kind == "rope" else w).astype(dst.dtype)
        body(*ins, *outs, *refs[n_in + 2 * n_cast + n_out:])
    return run


def _ada_kernel(cond_ref, w_ref, b_ref, o_ref):
    s = _silu(cond_ref[...]).astype(BF16)
    o_ref[...] = _dot(s, w_ref[...].astype(BF16)) + b_ref[...]


def _ada(cond, w_ada, b_ada):
    n_out = w_ada.shape[-1]
    return pl.pallas_call(
        _ada_kernel,
        out_shape=jax.ShapeDtypeStruct((DEPTH, COND_ROWS, n_out), F32),
        grid=(DEPTH, n_out // ADA_BLOCK_N),
        in_specs=[
            pl.BlockSpec((COND_ROWS, D_MODEL), lambda l, j: (0, 0)),
            pl.BlockSpec((None, D_MODEL, ADA_BLOCK_N), lambda l, j: (l, 0, j)),
            pl.BlockSpec((None, 1, ADA_BLOCK_N), lambda l, j: (l, 0, j)),
        ],
        out_specs=pl.BlockSpec((None, COND_ROWS, ADA_BLOCK_N), lambda l, j: (l, 0, j)),
        compiler_params=_params(("arbitrary", "arbitrary")),
        name="ada",
    )(cond, w_ada, b_ada.reshape(DEPTH, 1, n_out))


def _mod_spec(layer, row0, row_stride):
    return pl.BlockSpec((None, None, 6, D_MODEL),
                        lambda s, t: (layer, row0 + row_stride * s, 0, 0))


def _even_kernel(x_ref, xp_ref, xn_ref, mod_ref, g_ref, w_in_ref, pool_w_ref, pool_scale_ref,
                 conv_w_ref, conv_b_ref, w_out_ref, o_ref, *, tile, seq_len, packed):
    t = pl.program_id(1)
    n_t = pl.num_programs(1)
    sub = seq_len if packed else min(EVEN_SUB_ROWS, tile)
    n_sub = tile // sub
    win = sub + 2 * HALO
    mod = mod_ref[...]
    shift, scale, gate = mod[0:1], mod[1:2], mod[2:3]
    g = g_ref[...]
    cw = conv_w_ref[...]

    def h_rows(lo, hi):
        pieces = []
        if lo < HALO:
            pieces.append(xp_ref[...])
        pieces.append(x_ref[max(lo, HALO) - HALO:min(hi, HALO + tile) - HALO, :])
        if hi > HALO + tile:
            pieces.append(xn_ref[...])
        return jnp.concatenate([_modulate(r, g, shift, scale).astype(BF16) for r in pieces], axis=0)

    def project(s):
        if packed:
            rows = x_ref[s * sub:(s + 1) * sub, :]
            return _dot(_modulate(rows, g, shift, scale).astype(BF16), w_in_ref[...])
        lo = 0 if s == 0 else s * sub + 2 * HALO
        return _dot(h_rows(lo, (s + 1) * sub + 2 * HALO), w_in_ref[...])

    def edge_masked(a, s):
        if packed:
            return a
        parts = [a[:HALO], a[HALO:HALO + sub], a[HALO + sub:]]
        if s == 0:
            parts[0] = jnp.where(t > 0, parts[0], 0.0)
        if s == n_sub - 1:
            parts[2] = jnp.where(t < n_t - 1, parts[2], 0.0)
        return jnp.concatenate(parts, axis=0)

    def shifted(a, k):
        return pltpu.roll(a, k % win, 0)

    def mixers(s, p):
        first = 0 if packed else t * tile + s * sub
        pos = first + lax.broadcasted_iota(jnp.int32, (sub, 1), 0)
        u = edge_masked(p[:, :POOL_WIDTH], s)
        ya = []
        for i, w in enumerate(POOL_WINDOWS):
            ug = u[:, i * POOL_GROUP_DIM:(i + 1) * POOL_GROUP_DIM]
            fwd, n = ug, 1
            while n < w // 2:
                fwd = fwd + shifted(fwd, -n)
                n *= 2
            wsum = shifted(fwd, w // 2) + fwd
            cnt = (jnp.clip(pos + w // 2, 0, seq_len)
                   - jnp.clip(pos - w // 2, 0, seq_len)).astype(F32)
            pooled = wsum[HALO:HALO + sub] / cnt - ug[HALO:HALO + sub]
            ya.append(_dot(pooled.astype(BF16), pool_w_ref[i]))
        ya = jnp.concatenate(ya, axis=-1) * pool_scale_ref[...]
        c0 = POOL_WIDTH
        bg = p[HALO:HALO + sub, c0:c0 + CONV_WIDTH]
        z = edge_masked(p[:, c0 + CONV_WIDTH:c0 + 2 * CONV_WIDTH] * p[:, c0 + 2 * CONV_WIDTH:], s)
        conv = (shifted(z, 1) * cw[0:1] + z * cw[1:2] + shifted(z, -1) * cw[2:3])[HALO:HALO + sub]
        yb = bg * (conv + conv_b_ref[...])
        y = _dot(jnp.concatenate([ya, yb], axis=-1).astype(BF16), w_out_ref[...])
        rows = slice(s * sub, (s + 1) * sub)
        o_ref[rows, :] = x_ref[rows, :] + gate * y

    block = project(0)
    window = None
    margin = jnp.zeros((HALO, EVEN_IN), F32)
    for s in range(n_sub):
        if packed:
            window = jnp.concatenate([margin, block, margin], axis=0)
        else:
            window = block if s == 0 else jnp.concatenate([window[sub:], block], axis=0)
        if s + 1 < n_sub:
            block = project(s + 1)
        mixers(s, window)


def _even_sublayer(x, mod, layer, row0, row_stride, g, w_in, pool_w, pool_scale, conv_w, conv_b,
                   w_out, tile, inner=None, casts=()):
    n_seq, n_rows, _ = x.shape
    packed = inner is not None and inner < n_rows
    assert not packed or tile % inner == 0
    n_t = n_rows // tile
    hb = tile // HALO
    n_hb = n_rows // HALO
    cast_in, cast_out, cast_shapes = _riding_casts(casts, n_seq, n_t)
    kern = _with_casts(functools.partial(_even_kernel, tile=tile, packed=packed,
                                         seq_len=inner if packed else n_rows), 11, 1,
                       [kind for _, _, kind in casts])
    res = pl.pallas_call(
        kern,
        out_shape=(jax.ShapeDtypeStruct(x.shape, F32), *cast_shapes),
        grid=(n_seq, n_t),
        in_specs=[
            pl.BlockSpec((None, tile, D_MODEL), lambda s, t: (s, t, 0)),
            pl.BlockSpec((None, HALO, D_MODEL), lambda s, t: (s, jnp.maximum(t * hb - 1, 0), 0)),
            pl.BlockSpec((None, HALO, D_MODEL),
                         lambda s, t: (s, jnp.minimum((t + 1) * hb, n_hb - 1), 0)),
            _mod_spec(layer, row0, row_stride),
            _const_spec((1, D_MODEL)),
            _const_spec((D_MODEL, EVEN_IN)),
            _const_spec((POOL_GROUPS, POOL_GROUP_DIM, POOL_GROUP_DIM)),
            _const_spec((1, POOL_WIDTH)),
            _const_spec((3, CONV_WIDTH)),
            _const_spec((1, CONV_WIDTH)),
            _const_spec((POOL_WIDTH + CONV_WIDTH, D_MODEL)),
            *cast_in,
        ],
        out_specs=(pl.BlockSpec((None, tile, D_MODEL), lambda s, t: (s, t, 0)), *cast_out),
        compiler_params=_params(("arbitrary", "arbitrary")),
        name="even_mixer",
    )(x, x, x, mod, g, w_in, pool_w, pool_scale, conv_w, conv_b, w_out, *[c[0] for c in casts])
    return res[0], list(res[1:])


def _ffn_kernel(x_ref, mod_ref, g_ref, w_in_ref, w_out_ref, fg_ref, o_ref, *, final_norm):
    mod = mod_ref[...]
    shift, scale, gate = mod[3:4], mod[4:5], mod[5:6]
    g = g_ref[...]
    n_sub = x_ref.shape[0] // FFN_SUB_ROWS
    items = [(sub, c) for sub in range(n_sub) for c in range(len(FF_CHUNKS))]
    hs, ups = {}, {}

    def issue_up(j):
        sub, c = items[j]
        if c == 0:
            rows = x_ref[sub * FFN_SUB_ROWS:(sub + 1) * FFN_SUB_ROWS, :]
            hs[sub] = _modulate(rows, g, shift, scale).astype(BF16)
        c0, cw = FF_CHUNKS[c]
        ups[j] = (_dot(hs[sub], w_in_ref[:, c0:c0 + cw]),
                  _dot(hs[sub], w_in_ref[:, D_FF + c0:D_FF + c0 + cw]))

    for j in range(min(FFN_LOOKAHEAD, len(items))):
        issue_up(j)
    for i, (sub, c) in enumerate(items):
        if i + FFN_LOOKAHEAD < len(items):
            issue_up(i + FFN_LOOKAHEAD)
        a, b = ups.pop(i)
        c0, cw = FF_CHUNKS[c]
        part = _dot((_silu(a) * b).astype(BF16), w_out_ref[c0:c0 + cw, :])
        acc = part if c == 0 else acc + part
        if c == len(FF_CHUNKS) - 1:
            rows = slice(sub * FFN_SUB_ROWS, (sub + 1) * FFN_SUB_ROWS)
            out = x_ref[rows, :] + gate * acc
            if final_norm:
                out = _rms(out) * fg_ref[...]
            o_ref[rows, :] = out


def _ffn_sublayer(x, mod, layer, row0, row_stride, g, w_in, w_out, final_g, final_norm, tile,
                  casts=()):
    n_seq, seq_len, _ = x.shape
    cast_in, cast_out, cast_shapes = _riding_casts(casts, n_seq, seq_len // tile)
    kern = _with_casts(functools.partial(_ffn_kernel, final_norm=final_norm), 6, 1,
                       [kind for _, _, kind in casts])
    res = pl.pallas_call(
        kern,
        out_shape=(jax.ShapeDtypeStruct(x.shape, F32), *cast_shapes),
        grid=(n_seq, seq_len // tile),
        in_specs=[
            pl.BlockSpec((None, tile, D_MODEL), lambda s, t: (s, t, 0)),
            _mod_spec(layer, row0, row_stride),
            _const_spec((1, D_MODEL)),
            _const_spec((D_MODEL, 2 * D_FF)),
            _const_spec((D_FF, D_MODEL)),
            _const_spec((1, D_MODEL)),
            *cast_in,
        ],
        out_specs=(pl.BlockSpec((None, tile, D_MODEL), lambda s, t: (s, t, 0)), *cast_out),
        compiler_params=_params(("arbitrary", "arbitrary")),
        name="ffn",
    )(x, mod, g, w_in, w_out, final_g, *[c[0] for c in casts])
    return res[0], list(res[1:])


def _rope_lane_order(a):
    lead = a.shape[:-1]
    return a.reshape(*lead, 2, 2, AXIS_DIM // 2).swapaxes(-3, -2).reshape(*lead, HEAD_DIM)


def _rope(a, cos, sin_signed):
    return a * cos + pltpu.roll(a, HEAD_DIM // 2, 1) * sin_signed


def _pair_mean_matrix():
    r = lax.broadcasted_iota(jnp.int32, (2 * HEAD_DIM, 2 * HEAD_DIM), 0) // HEAD_DIM
    c = lax.broadcasted_iota(jnp.int32, (2 * HEAD_DIM, 2 * HEAD_DIM), 1) // HEAD_DIM
    return jnp.where(r == c, 1.0 / HEAD_DIM, 0.0).astype(BF16)


def _rope_lane_matrix():
    src = lax.broadcasted_iota(jnp.int32, (HEAD_DIM, HEAD_DIM), 0)
    n = lax.broadcasted_iota(jnp.int32, (HEAD_DIM, HEAD_DIM), 1)
    quarter = AXIS_DIM // 2
    half, axis, c = n // AXIS_DIM, (n // quarter) % 2, n % quarter
    return jnp.where(src == axis * AXIS_DIM + half * quarter + c, 1.0, 0.0).astype(BF16)


def _qkv_kernel(*refs, rope, cached, keep_f32):
    it = iter(refs)
    x_ref, mod_ref, g_ref, w_ref, qg_ref, kg_ref = (next(it) for _ in range(6))
    rope_refs = [next(it) for _ in range(2)] if rope else None
    cache_refs = [next(it) for _ in range(2)] if cached else None
    q_ref, kh_ref, vt_ref = (next(it) for _ in range(3))
    cache_out_refs = [next(it) for _ in range(2)] if cached else None
    f32_refs = [next(it) for _ in range(2)] if keep_f32 else None

    if cached:
        @pl.when(pl.program_id(1) == 0)
        def _():
            for kv in range(N_KV_HEADS):
                k = cache_refs[0][:, kv, :].astype(BF16)
                if rope:
                    k = _dot(k, _rope_lane_matrix()).astype(BF16)
                cache_out_refs[0][kv] = k
                cache_out_refs[1][kv] = cache_refs[1][:, kv, :].T.astype(BF16)

    mod = mod_ref[...]
    shift, scale = mod[0:1], mod[1:2]
    g = g_ref[...]
    gains = [qg_ref[...] * (SM_SCALE * LOG2E)] * N_HEADS + [kg_ref[...]] * N_KV_HEADS
    pair_mean = _pair_mean_matrix()
    v0 = (N_HEADS + N_KV_HEADS) * HEAD_DIM
    sub_rows = min(QKV_SUB_ROWS, x_ref.shape[0])
    n_sub = x_ref.shape[0] // sub_rows

    def rows_of(sub):
        return slice(sub * sub_rows, (sub + 1) * sub_rows)

    def projection(sub):
        h = _modulate(x_ref[rows_of(sub), :], g, shift, scale).astype(BF16)
        return _dot(h, w_ref[...])

    def finish(sub, p):
        rows = rows_of(sub)
        if rope:
            tables = [r[rows, :] for r in rope_refs]
        for pair in range((N_HEADS + N_KV_HEADS) // 2):
            blk = p[:, pair * 2 * HEAD_DIM:(pair + 1) * 2 * HEAD_DIM]
            ms = _dot((blk * blk).astype(BF16), pair_mean)
            normed = blk * lax.rsqrt(ms + EPS)
            for half in range(2):
                hd = 2 * pair + half
                a = normed[:, half * HEAD_DIM:(half + 1) * HEAD_DIM] * gains[hd]
                if rope:
                    a = _rope(a, *tables)
                if hd < N_HEADS:
                    q_ref[hd, rows, :] = a.astype(BF16)
                else:
                    kv = hd - N_HEADS
                    v = p[:, v0 + kv * HEAD_DIM:v0 + (kv + 1) * HEAD_DIM]
                    kh_ref[kv, rows, :] = a.astype(BF16)
                    vt_ref[kv, :, rows] = v.T.astype(BF16)
                    if keep_f32:
                        f32_refs[0][rows, kv, :] = a
                        f32_refs[1][rows, kv, :] = v

    p_next = projection(0)
    for sub in range(n_sub):
        p_cur = p_next
        if sub + 1 < n_sub:
            p_next = projection(sub + 1)
        finish(sub, p_cur)


def _qkv_proj(x, mod, layer, row0, row_stride, g, w_qkv, q_gain, k_gain, rope_tables, cache,
              cache_layer, keep_f32, tile):
    n_seq, seq_len, _ = x.shape
    rope = rope_tables is not None
    cached = cache is not None
    in_specs = [
        pl.BlockSpec((None, tile, D_MODEL), lambda s, t: (s, t, 0)),
        _mod_spec(layer, row0, row_stride),
        _const_spec((1, D_MODEL)),
        _const_spec((D_MODEL, QKV_WIDTH)),
        _const_spec((1, HEAD_DIM)),
        _const_spec((1, HEAD_DIM)),
    ]
    args = [x, mod, g, w_qkv, q_gain, k_gain]
    if rope:
        in_specs += [pl.BlockSpec((tile, HEAD_DIM), lambda s, t: (t, 0))] * 2
        args += list(rope_tables)
    out_shape = [jax.ShapeDtypeStruct((n_seq, N_HEADS, seq_len, HEAD_DIM), BF16),
                 jax.ShapeDtypeStruct((n_seq, N_KV_HEADS, seq_len, HEAD_DIM), BF16),
                 jax.ShapeDtypeStruct((n_seq, N_KV_HEADS, HEAD_DIM, seq_len), BF16)]
    out_specs = [pl.BlockSpec((None, N_HEADS, tile, HEAD_DIM), lambda s, t: (s, 0, t, 0)),
                 pl.BlockSpec((None, N_KV_HEADS, tile, HEAD_DIM), lambda s, t: (s, 0, t, 0)),
                 pl.BlockSpec((None, N_KV_HEADS, HEAD_DIM, tile), lambda s, t: (s, 0, 0, t))]
    if cached:
        past = cache[0].shape[2]
        in_specs += [pl.BlockSpec((None, None, past, N_KV_HEADS, HEAD_DIM),
                                  lambda s, t: (s, cache_layer, 0, 0, 0))] * 2
        args += list(cache)
        out_shape += [jax.ShapeDtypeStruct((n_seq, N_KV_HEADS, past, HEAD_DIM), BF16),
                      jax.ShapeDtypeStruct((n_seq, N_KV_HEADS, HEAD_DIM, past), BF16)]
        out_specs += [pl.BlockSpec((None, N_KV_HEADS, past, HEAD_DIM), lambda s, t: (s, 0, 0, 0)),
                      pl.BlockSpec((None, N_KV_HEADS, HEAD_DIM, past), lambda s, t: (s, 0, 0, 0))]
    if keep_f32:
        out_shape += [jax.ShapeDtypeStruct((n_seq, 1, seq_len, N_KV_HEADS, HEAD_DIM), F32)] * 2
        out_specs += [pl.BlockSpec((None, None, tile, N_KV_HEADS, HEAD_DIM),
                                   lambda s, t: (s, 0, t, 0, 0))] * 2
    return pl.pallas_call(
        functools.partial(_qkv_kernel, rope=rope, cached=cached, keep_f32=keep_f32),
        out_shape=tuple(out_shape),
        grid=(n_seq, seq_len // tile),
        in_specs=in_specs,
        out_specs=tuple(out_specs),
        compiler_params=_params(("arbitrary", "arbitrary")),
        name="qkv_proj",
    )(*args)


def _attn_kernel(*refs, n_segments, key_chunk, inner):
    q_ref = refs[0]
    segments = [(refs[1 + 2 * i], refs[2 + 2 * i]) for i in range(n_segments)]
    x_ref, mod_ref, w_o_ref, qg_ref, o_ref, heads_scr, shift_scr, safe_scr = refs[1 + 2 * n_segments:]
    group = N_HEADS // N_KV_HEADS
    tile = q_ref.shape[1]
    q_rows = min(ATTN_Q_ROWS, tile) if inner is None else inner
    n_parts = tile // q_rows

    def chunks_of(part):
        if inner is None:
            return [(seg, c0, min(key_chunk, k_ref.shape[1] - c0))
                    for seg, (k_ref, _) in enumerate(segments)
                    for c0 in range(0, k_ref.shape[1], key_chunk)]
        return [(0, part * inner + c0, min(key_chunk, inner - c0))
                for c0 in range(0, inner, key_chunk)]

    @pl.when(pl.program_id(1) == 0)
    def _():
        qg = qg_ref[...]
        q_sq = HEAD_DIM * (SM_SCALE * LOG2E) ** 2 * jnp.max(qg * qg)
        for kv in range(N_KV_HEADS):
            k_sq = None
            for k_ref, _ in segments:
                k = k_ref[kv].astype(F32)
                seg_sq = jnp.max(jnp.sum(k * k, axis=1, keepdims=True))
                k_sq = seg_sq if k_sq is None else jnp.maximum(k_sq, seg_sq)
            bound_sq = q_sq * k_sq * BOUND_SLACK
            shift_scr[kv] = jnp.full(shift_scr.shape[1:], jnp.sqrt(bound_sq), F32)
            safe_scr[kv] = (bound_sq <= MAX_SAFE_SHIFT ** 2).astype(jnp.int32)

    def bounded():
        items = [(part, hd, ci, chunk, ci == len(chunks_of(part)) - 1)
                 for part in range(n_parts) for hd in range(N_HEADS)
                 for ci, chunk in enumerate(chunks_of(part))]
        scores = {}

        def issue_scores(j):
            part, hd, _, (seg, c0, rows), _ = items[j]
            keys = segments[seg][0][hd // group, c0:c0 + rows, :]
            q = q_ref[hd, part * q_rows:(part + 1) * q_rows, :]
            scores[j] = _dot_nt(keys, q)

        for j in range(min(ATTN_LOOKAHEAD, len(items))):
            issue_scores(j)
        for i, (part, hd, ci, (seg, c0, rows), last) in enumerate(items):
            if i + ATTN_LOOKAHEAD < len(items):
                issue_scores(i + ATTN_LOOKAHEAD)
            p = jnp.exp2(scores.pop(i) - shift_scr[hd // group][0:1, 0:1])
            ps = jnp.sum(p, axis=0, keepdims=True)
            pv = _dot(segments[seg][1][hd // group, :, c0:c0 + rows], p.astype(BF16))
            denom, acc = (ps, pv) if ci == 0 else (denom + ps, acc + pv)
            if last:
                heads_scr[hd, part * q_rows:(part + 1) * q_rows, :] = (
                    (acc * (1.0 / denom)).T.astype(BF16))

    def exact():
        def one_head(hd, carry):
            kv = hd // group
            for part in range(n_parts):
                rows_q = slice(part * q_rows, (part + 1) * q_rows)
                q = q_ref[hd, rows_q, :]
                if inner is None:
                    key_sets = [(k_ref[kv], vt_ref[kv]) for k_ref, vt_ref in segments]
                else:
                    keys = slice(part * inner, (part + 1) * inner)
                    key_sets = [(segments[0][0][kv, keys, :], segments[0][1][kv, :, keys])]
                scores = [_dot_nt(k, q) for k, _ in key_sets]
                m = None
                for s in scores:
                    sm = jnp.max(s, axis=0, keepdims=True)
                    m = sm if m is None else jnp.maximum(m, sm)
                denom = o_t = None
                for s, (_, vt) in zip(scores, key_sets):
                    p = jnp.exp2(s - m)
                    ps = jnp.sum(p, axis=0, keepdims=True)
                    pv = _dot(vt, p.astype(BF16))
                    denom, o_t = (ps, pv) if denom is None else (denom + ps, o_t + pv)
                heads_scr[hd, rows_q, :] = (o_t * (1.0 / denom)).T.astype(BF16)
            return carry

        lax.fori_loop(0, N_HEADS, one_head, 0)

    safe = safe_scr[0]
    for kv in range(1, N_KV_HEADS):
        safe = jnp.minimum(safe, safe_scr[kv])
    pl.when(safe == 1)(bounded)
    pl.when(safe != 1)(exact)

    attn = jnp.concatenate([heads_scr[hd] for hd in range(N_HEADS)], axis=-1)
    o_ref[...] = x_ref[...] + mod_ref[2:3] * _dot(attn, w_o_ref[...])


def _attn_sublayer(q, key_segments, x, mod, layer, row0, row_stride, w_o, q_gain, tile, inner=None):
    n_seq, seq_len, _ = x.shape
    if inner is not None and inner == seq_len:
        inner = None
    assert inner is None or (len(key_segments) == 1 and tile % inner == 0)
    in_specs = [pl.BlockSpec((None, N_HEADS, tile, HEAD_DIM), lambda s, t: (s, 0, t, 0))]
    args = [q]
    for k, vt in key_segments:
        n_keys = k.shape[2]
        in_specs += [pl.BlockSpec((None, N_KV_HEADS, n_keys, HEAD_DIM), lambda s, t: (s, 0, 0, 0)),
                     pl.BlockSpec((None, N_KV_HEADS, HEAD_DIM, n_keys), lambda s, t: (s, 0, 0, 0))]
        args += [k, vt]
    in_specs += [
        pl.BlockSpec((None, tile, D_MODEL), lambda s, t: (s, t, 0)),
        _mod_spec(layer, row0, row_stride),
        _const_spec((D_MODEL, D_MODEL)),
        _const_spec((1, HEAD_DIM)),
    ]
    args += [x, mod, w_o, q_gain]
    return pl.pallas_call(
        functools.partial(_attn_kernel, n_segments=len(key_segments), key_chunk=ATTN_KEY_CHUNK,
                          inner=inner),
        out_shape=jax.ShapeDtypeStruct(x.shape, F32),
        grid=(n_seq, seq_len // tile),
        in_specs=in_specs,
        out_specs=pl.BlockSpec((None, tile, D_MODEL), lambda s, t: (s, t, 0)),
        scratch_shapes=[pltpu.VMEM((N_HEADS, tile, HEAD_DIM), BF16),
                        pltpu.VMEM((N_KV_HEADS, 8, HEAD_DIM), F32),
                        pltpu.SMEM((N_KV_HEADS,), jnp.int32)],
        compiler_params=_params(("arbitrary", "arbitrary")),
        name="attention",
    )(*args)


def _rope_tables(seq_len):
    rows = seq_len // GRID_W
    r = np.repeat(np.arange(rows), GRID_W).astype(np.float32)
    col = np.tile(np.arange(GRID_W), rows).astype(np.float32)
    expo = -np.arange(0, AXIS_DIM, 2, dtype=np.float32) / np.float32(AXIS_DIM)
    inv = np.power(np.float32(ROPE_THETA), expo).astype(np.float32)
    ar = r[:, None] * inv
    ac = col[:, None] * inv
    ang = np.concatenate([ar, ar, ac, ac], axis=-1)
    cos, sin = np.cos(ang), np.sin(ang)
    first = (np.arange(HEAD_DIM) % AXIS_DIM) < AXIS_DIM // 2
    return (jnp.asarray(_rope_lane_order(cos), F32),
            jnp.asarray(_rope_lane_order(np.where(first, -sin, sin)), F32))


def _rope_weights(w_qkv):
    qk = (N_HEADS + N_KV_HEADS) * HEAD_DIM
    lead = w_qkv.shape[:-1]
    w_qk = _rope_lane_order(w_qkv[..., :qk].reshape(*lead, N_HEADS + N_KV_HEADS, HEAD_DIM))
    return jnp.concatenate([w_qk.reshape(*lead, qk), w_qkv[..., qk:]], axis=-1)


def _run_layer(x, l, mod, grp, wts, ffn_w, ffn_f32, odd_w, odd_f32, ride):
    row0, row_stride = grp["row0"], grp["row_stride"]
    i = l // 2
    g1 = wts["norm_g"][l, 0].reshape(1, D_MODEL)
    g2 = wts["norm_g"][l, 1].reshape(1, D_MODEL)
    new_kv = None
    if l % 2 == 0:
        tile = min(EVEN_TILE, x.shape[1])
        casts = []
        if l == 0 and 0 not in ffn_w:
            steps = x.shape[0] * (x.shape[1] // tile)
            if ride and all(_can_ride(w, steps) for w in ffn_f32 + odd_f32):
                casts = [(w, 0, "plain") for w in ffn_f32]
                for i_odd in range(DEPTH // 2):
                    casts += [(odd_f32[0], i_odd, "plain"), (odd_f32[0], i_odd, "rope"),
                              (odd_f32[1], i_odd, "plain")]
            else:
                ffn_w[0] = tuple(w[0].astype(BF16) for w in ffn_f32)
                for i_odd in range(DEPTH // 2):
                    w_qkv_bf16 = odd_f32[0][i_odd].astype(BF16)
                    odd_w[i_odd] = (w_qkv_bf16, _rope_weights(w_qkv_bf16),
                                    odd_f32[1][i_odd].astype(BF16))
        x, cast_out = _even_sublayer(x, mod, l, row0, row_stride, g1, wts["w_in_even"][i],
                                     wts["pool_w"][i], wts["pool_scale"][i].reshape(1, POOL_WIDTH),
                                     wts["conv_w"][i], wts["conv_b"][i].reshape(1, CONV_WIDTH),
                                     wts["w_out_even"][i], tile, grp["inner"], casts)
        if casts:
            ffn_w[0] = tuple(cast_out[:2])
            for i_odd in range(DEPTH // 2):
                odd_w[i_odd] = tuple(cast_out[2 + 3 * i_odd:5 + 3 * i_odd])
    else:
        cache, rope_tables = grp["cache"], grp["rope_tables"]
        sfx = "" if rope_tables is None else "_rope"
        w_qkv, w_qkv_rope, w_o = odd_w[i]
        res = _qkv_proj(x, mod, l, row0, row_stride, g1, w_qkv if rope_tables is None else w_qkv_rope,
                        wts["q_gain" + sfx][i].reshape(1, HEAD_DIM),
                        wts["k_gain" + sfx][i].reshape(1, HEAD_DIM), rope_tables, cache, i,
                        cache is None, min(QKV_TILE, x.shape[1]))
        q, segments = res[0], [res[1:3]]
        if cache is None:
            new_kv = res[3:]
        else:
            segments.append(res[3:5])
        x = _attn_sublayer(q, segments, x, mod, l, row0, row_stride, w_o,
                           wts["q_gain"][i].reshape(1, HEAD_DIM), grp["attn_tile"], grp["inner"])
    flat = x.reshape(1, -1, D_MODEL) if row_stride == 0 else x
    tile = min(FFN_TILE, flat.shape[1])
    casts = []
    if l + 1 < DEPTH and l + 1 not in ffn_w:
        steps = flat.shape[0] * (flat.shape[1] // tile)
        if ride and all(_can_ride(w, steps) for w in ffn_f32):
            casts = [(w, l + 1, "plain") for w in ffn_f32]
        else:
            ffn_w[l + 1] = tuple(w[l + 1].astype(BF16) for w in ffn_f32)
    flat, cast_out = _ffn_sublayer(flat, mod, l, row0, row_stride, g2, *ffn_w[l],
                                   wts["final_g"].reshape(1, D_MODEL), l == DEPTH - 1, tile, casts)
    if casts:
        ffn_w[l + 1] = tuple(cast_out)
    return flat.reshape(x.shape), new_kv


def kernel(x_prompt, x_sample, cache_k, cache_v, c, c_ctx, w_ada, b_ada, norm_g, w_in_even, pool_w,
           pool_scale, conv_w, conv_b, w_out_even, w_qkv, q_gain, k_gain, w_o, w_ffn_in, w_ffn_out,
           final_g):
    n_ctx, ctx_len, _ = x_prompt.shape
    n_lat, lat_len, _ = x_sample.shape
    assert 1 + n_lat <= COND_ROWS

    cond = jnp.concatenate(
        [c_ctx[None, :], c, jnp.zeros((COND_ROWS - 1 - n_lat, D_MODEL), F32)], axis=0)
    mod = _ada(cond, w_ada, b_ada).reshape(DEPTH, COND_ROWS, 6, D_MODEL)

    wts = dict(
        norm_g=norm_g, final_g=final_g, pool_scale=pool_scale, conv_w=conv_w, conv_b=conv_b,
        q_gain=q_gain, k_gain=k_gain,
        w_in_even=w_in_even.astype(BF16), pool_w=pool_w.astype(BF16),
        w_out_even=w_out_even.astype(BF16),
        q_gain_rope=_rope_lane_order(q_gain), k_gain_rope=_rope_lane_order(k_gain),
    )

    pack = math.gcd(n_ctx, max(1, CTX_PACK_ROWS // ctx_len))
    ctx_grp = dict(row0=0, row_stride=0, cache=None, rope_tables=None, attn_tile=pack * ctx_len,
                   inner=ctx_len)
    lat_grp = dict(row0=1, row_stride=1, cache=(cache_k, cache_v), rope_tables=_rope_tables(lat_len),
                   attn_tile=ATTN_TILE, inner=None)
    ffn_w, ffn_f32 = {}, (w_ffn_in, w_ffn_out)
    odd_w, odd_f32 = {}, (w_qkv, w_o)
    y_prompt = x_prompt.reshape(n_ctx // pack, pack * ctx_len, D_MODEL)
    y_sample, ctx_kv = x_sample, []
    for l in range(DEPTH):
        y_sample, _ = _run_layer(y_sample, l, mod, lat_grp, wts, ffn_w, ffn_f32, odd_w, odd_f32, True)
        y_prompt, kv = _run_layer(y_prompt, l, mod, ctx_grp, wts, ffn_w, ffn_f32, odd_w, odd_f32,
                                  False)
        if kv is not None:
            ctx_kv.append(kv)
    kv_shape = (n_ctx, 1, ctx_len, N_KV_HEADS, HEAD_DIM)
    new_cache_k = jnp.concatenate([k.reshape(kv_shape) for k, _ in ctx_kv], axis=1)
    new_cache_v = jnp.concatenate([v.reshape(kv_shape) for _, v in ctx_kv], axis=1)
    return y_prompt.reshape(x_prompt.shape), y_sample, new_cache_k, new_cache_v
```

```python
import functools
import math

import jax
import jax.numpy as jnp
import numpy as np
from jax import lax
from jax.experimental import pallas as pl
from jax.experimental.pallas import tpu as pltpu

D_MODEL = 1024
DEPTH = 2
GRID_W = 64
EPS = 1e-6
POOL_WIDTH = D_MODEL // 2
POOL_GROUPS = 4
POOL_GROUP_DIM = POOL_WIDTH // POOL_GROUPS
POOL_WINDOWS = (2, 4, 8, 16)
CONV_WIDTH = D_MODEL // 2
EVEN_IN = POOL_WIDTH + 3 * CONV_WIDTH
HEAD_DIM = 128
N_HEADS = D_MODEL // HEAD_DIM
N_KV_HEADS = N_HEADS // 4
QKV_WIDTH = (N_HEADS + 2 * N_KV_HEADS) * HEAD_DIM
ROPE_THETA = 10000.0
AXIS_DIM = HEAD_DIM // 2
D_FF = ((8 * D_MODEL // 3 + 255) // 256) * 256
SM_SCALE = HEAD_DIM ** -0.5
LOG2E = math.log2(math.e)

COND_ROWS = 8
HALO = 16
ADA_BLOCK_N = 3072
FF_CHUNKS = ((0, 1024), (1024, 1024), (2048, 768))
CTX_PACK_ROWS = 1024
QKV_TILE = 1024
QKV_SUB_ROWS = 128
EVEN_TILE = 1024
EVEN_SUB_ROWS = 512
FFN_TILE = 1024
FFN_SUB_ROWS = 256
FFN_LOOKAHEAD = 1
ATTN_TILE = 1024
ATTN_Q_ROWS = 256
ATTN_KEY_CHUNK = 512
ATTN_LOOKAHEAD = 2
BOUND_SLACK = 1.1
MAX_SAFE_SHIFT = 60.0
V7X_VMEM_BYTES = 64 * 1024 * 1024
VMEM_LIMIT_BYTES = V7X_VMEM_BYTES - 8 * 1024 * 1024

BF16 = jnp.bfloat16
F32 = jnp.float32


def _dot(a, b):
    return jnp.dot(a, b, preferred_element_type=F32)


def _dot_nt(a, b):
    return lax.dot_general(a, b, (((1,), (1,)), ((), ())), preferred_element_type=F32)


def _silu(a):
    return a * (1.0 / (1.0 + jnp.exp(-a)))


def _rms(x):
    return x * lax.rsqrt(jnp.mean(x * x, axis=-1, keepdims=True) + EPS)


def _modulate(x, g, shift, scale):
    return (_rms(x) * g) * (1.0 + scale) + shift


def _params(sem):
    return pltpu.CompilerParams(dimension_semantics=sem, vmem_limit_bytes=VMEM_LIMIT_BYTES)


def _const_spec(shape):
    n = len(shape)
    return pl.BlockSpec(shape, lambda *_: (0,) * n, pipeline_mode=pl.Buffered(1))


def _riding_casts(casts, n_seq, n_t):
    steps = n_seq * n_t
    in_specs, out_specs, out_shapes = [], [], []
    for w, layer, _ in casts:
        rows = w.shape[1] // steps
        in_specs.append(pl.BlockSpec((None, rows, w.shape[2]),
                                     lambda s, t, layer=layer: (layer, s * n_t + t, 0)))
        out_specs.append(pl.BlockSpec((rows, w.shape[2]), lambda s, t: (s * n_t + t, 0)))
        out_shapes.append(jax.ShapeDtypeStruct(w.shape[1:], BF16))
    return in_specs, out_specs, out_shapes


def _can_ride(w, steps):
    return w.shape[1] % steps == 0 and (w.shape[1] // steps) % 16 == 0


def _rope_lane_order_cols(w):
    quarter_w = AXIS_DIM // 2
    lane = lax.broadcasted_iota(jnp.int32, w.shape, 1)
    quarter = (lane % HEAD_DIM) // quarter_w
    qk = lane < (N_HEADS + N_KV_HEADS) * HEAD_DIM
    from_right = pltpu.roll(w, w.shape[1] - quarter_w, 1)
    from_left = pltpu.roll(w, quarter_w, 1)
    swapped = jnp.where(quarter == 1, from_right, jnp.where(quarter == 2, from_left, w))
    return jnp.where(qk, swapped, w)


def _with_casts(body, n_in, n_out, kinds):
    n_cast = len(kinds)

    def run(*refs):
        ins, cast_ins = refs[:n_in], refs[n_in:n_in + n_cast]
        outs = refs[n_in + n_cast:n_in + n_cast + n_out]
        cast_outs = refs[n_in + n_cast + n_out:n_in + 2 * n_cast + n_out]
        for src, dst, kind in zip(cast_ins, cast_outs, kinds):
            w = src[...]
            dst[...] = (_rope_lane_order_cols(w) if kind == "rope" else w).astype(dst.dtype)
        body(*ins, *outs, *refs[n_in + 2 * n_cast + n_out:])
    return run


def _ada_kernel(cond_ref, w_ref, b_ref, o_ref):
    s = _silu(cond_ref[...]).astype(BF16)
    o_ref[...] = _dot(s, w_ref[...].astype(BF16)) + b_ref[...]


def _ada(cond, w_ada, b_ada):
    n_out = w_ada.shape[-1]
    return pl.pallas_call(
        _ada_kernel,
        out_shape=jax.ShapeDtypeStruct((DEPTH, COND_ROWS, n_out), F32),
        grid=(DEPTH, n_out // ADA_BLOCK_N),
        in_specs=[
            pl.BlockSpec((COND_ROWS, D_MODEL), lambda l, j: (0, 0)),
            pl.BlockSpec((None, D_MODEL, ADA_BLOCK_N), lambda l, j: (l, 0, j)),
            pl.BlockSpec((None, 1, ADA_BLOCK_N), lambda l, j: (l, 0, j)),
        ],
        out_specs=pl.BlockSpec((None, COND_ROWS, ADA_BLOCK_N), lambda l, j: (l, 0, j)),
        compiler_params=_params(("arbitrary", "arbitrary")),
        name="ada",
    )(cond, w_ada, b_ada.reshape(DEPTH, 1, n_out))


def _mod_spec(layer, row0, row_stride):
    return pl.BlockSpec((None, None, 6, D_MODEL),
                        lambda s, t: (layer, row0 + row_stride * s, 0, 0))


def _even_kernel(x_ref, xp_ref, xn_ref, mod_ref, g_ref, w_in_ref, pool_w_ref, pool_scale_ref,
                 conv_w_ref, conv_b_ref, w_out_ref, o_ref, *, tile, seq_len, packed):
    t = pl.program_id(1)
    n_t = pl.num_programs(1)
    sub = seq_len if packed else min(EVEN_SUB_ROWS, tile)
    n_sub = tile // sub
    win = sub + 2 * HALO
    mod = mod_ref[...]
    shift, scale, gate = mod[0:1], mod[1:2], mod[2:3]
    g = g_ref[...]
    cw = conv_w_ref[...]

    def h_rows(lo, hi):
        pieces = []
        if lo < HALO:
            pieces.append(xp_ref[...])
        pieces.append(x_ref[max(lo, HALO) - HALO:min(hi, HALO + tile) - HALO, :])
        if hi > HALO + tile:
            pieces.append(xn_ref[...])
        return jnp.concatenate([_modulate(r, g, shift, scale).astype(BF16) for r in pieces], axis=0)

    def project(s):
        if packed:
            rows = x_ref[s * sub:(s + 1) * sub, :]
            return _dot(_modulate(rows, g, shift, scale).astype(BF16), w_in_ref[...])
        lo = 0 if s == 0 else s * sub + 2 * HALO
        return _dot(h_rows(lo, (s + 1) * sub + 2 * HALO), w_in_ref[...])

    def edge_masked(a, s):
        if packed:
            return a
        parts = [a[:HALO], a[HALO:HALO + sub], a[HALO + sub:]]
        if s == 0:
            parts[0] = jnp.where(t > 0, parts[0], 0.0)
        if s == n_sub - 1:
            parts[2] = jnp.where(t < n_t - 1, parts[2], 0.0)
        return jnp.concatenate(parts, axis=0)

    def shifted(a, k):
        return pltpu.roll(a, k % win, 0)

    def mixers(s, p):
        first = 0 if packed else t * tile + s * sub
        pos = first + lax.broadcasted_iota(jnp.int32, (sub, 1), 0)
        u = edge_masked(p[:, :POOL_WIDTH], s)
        ya = []
        for i, w in enumerate(POOL_WINDOWS):
            ug = u[:, i * POOL_GROUP_DIM:(i + 1) * POOL_GROUP_DIM]
            fwd, n = ug, 1
            while n < w // 2:
                fwd = fwd + shifted(fwd, -n)
                n *= 2
            wsum = shifted(fwd, w // 2) + fwd
            cnt = (jnp.clip(pos + w // 2, 0, seq_len)
                   - jnp.clip(pos - w // 2, 0, seq_len)).astype(F32)
            pooled = wsum[HALO:HALO + sub] / cnt - ug[HALO:HALO + sub]
            ya.append(_dot(pooled.astype(BF16), pool_w_ref[i]))
        ya = jnp.concatenate(ya, axis=-1) * pool_scale_ref[...]
        c0 = POOL_WIDTH
        bg = p[HALO:HALO + sub, c0:c0 + CONV_WIDTH]
        z = edge_masked(p[:, c0 + CONV_WIDTH:c0 + 2 * CONV_WIDTH] * p[:, c0 + 2 * CONV_WIDTH:], s)
        conv = (shifted(z, 1) * cw[0:1] + z * cw[1:2] + shifted(z, -1) * cw[2:3])[HALO:HALO + sub]
        yb = bg * (conv + conv_b_ref[...])
        y = _dot(jnp.concatenate([ya, yb], axis=-1).astype(BF16), w_out_ref[...])
        rows = slice(s * sub, (s + 1) * sub)
        o_ref[rows, :] = x_ref[rows, :] + gate * y

    block = project(0)
    window = None
    margin = jnp.zeros((HALO, EVEN_IN), F32)
    for s in range(n_sub):
        if packed:
            window = jnp.concatenate([margin, block, margin], axis=0)
        else:
            window = block if s == 0 else jnp.concatenate([window[sub:], block], axis=0)
        if s + 1 < n_sub:
            block = project(s + 1)
        mixers(s, window)


def _even_sublayer(x, mod, layer, row0, row_stride, g, w_in, pool_w, pool_scale, conv_w, conv_b,
                   w_out, tile, inner=None, casts=()):
    n_seq, n_rows, _ = x.shape
    packed = inner is not None and inner < n_rows
    assert not packed or tile % inner == 0
    n_t = n_rows // tile
    hb = tile // HALO
    n_hb = n_rows // HALO
    cast_in, cast_out, cast_shapes = _riding_casts(casts, n_seq, n_t)
    kern = _with_casts(functools.partial(_even_kernel, tile=tile, packed=packed,
                                         seq_len=inner if packed else n_rows), 11, 1,
                       [kind for _, _, kind in casts])
    res = pl.pallas_call(
        kern,
        out_shape=(jax.ShapeDtypeStruct(x.shape, F32), *cast_shapes),
        grid=(n_seq, n_t),
        in_specs=[
            pl.BlockSpec((None, tile, D_MODEL), lambda s, t: (s, t, 0)),
            pl.BlockSpec((None, HALO, D_MODEL), lambda s, t: (s, jnp.maximum(t * hb - 1, 0), 0)),
            pl.BlockSpec((None, HALO, D_MODEL),
                         lambda s, t: (s, jnp.minimum((t + 1) * hb, n_hb - 1), 0)),
            _mod_spec(layer, row0, row_stride),
            _const_spec((1, D_MODEL)),
            _const_spec((D_MODEL, EVEN_IN)),
            _const_spec((POOL_GROUPS, POOL_GROUP_DIM, POOL_GROUP_DIM)),
            _const_spec((1, POOL_WIDTH)),
            _const_spec((3, CONV_WIDTH)),
            _const_spec((1, CONV_WIDTH)),
            _const_spec((POOL_WIDTH + CONV_WIDTH, D_MODEL)),
            *cast_in,
        ],
        out_specs=(pl.BlockSpec((None, tile, D_MODEL), lambda s, t: (s, t, 0)), *cast_out),
        compiler_params=_params(("arbitrary", "arbitrary")),
        name="even_mixer",
    )(x, x, x, mod, g, w_in, pool_w, pool_scale, conv_w, conv_b, w_out, *[c[0] for c in casts])
    return res[0], list(res[1:])


def _ffn_kernel(x_ref, mod_ref, g_ref, w_in_ref, w_out_ref, fg_ref, o_ref, *, final_norm):
    mod = mod_ref[...]
    shift, scale, gate = mod[3:4], mod[4:5], mod[5:6]
    g = g_ref[...]
    n_sub = x_ref.shape[0] // FFN_SUB_ROWS
    items = [(sub, c) for sub in range(n_sub) for c in range(len(FF_CHUNKS))]
    hs, ups = {}, {}

    def issue_up(j):
        sub, c = items[j]
        if c == 0:
            rows = x_ref[sub * FFN_SUB_ROWS:(sub + 1) * FFN_SUB_ROWS, :]
            hs[sub] = _modulate(rows, g, shift, scale).astype(BF16)
        c0, cw = FF_CHUNKS[c]
        ups[j] = (_dot(hs[sub], w_in_ref[:, c0:c0 + cw]),
                  _dot(hs[sub], w_in_ref[:, D_FF + c0:D_FF + c0 + cw]))

    for j in range(min(FFN_LOOKAHEAD, len(items))):
        issue_up(j)
    for i, (sub, c) in enumerate(items):
        if i + FFN_LOOKAHEAD < len(items):
            issue_up(i + FFN_LOOKAHEAD)
        a, b = ups.pop(i)
        c0, cw = FF_CHUNKS[c]
        part = _dot((_silu(a) * b).astype(BF16), w_out_ref[c0:c0 + cw, :])
        acc = part if c == 0 else acc + part
        if c == len(FF_CHUNKS) - 1:
            rows = slice(sub * FFN_SUB_ROWS, (sub + 1) * FFN_SUB_ROWS)
            out = x_ref[rows, :] + gate * acc
            if final_norm:
                out = _rms(out) * fg_ref[...]
            o_ref[rows, :] = out


def _ffn_sublayer(x, mod, layer, row0, row_stride, g, w_in, w_out, final_g, final_norm, tile,
                  casts=()):
    n_seq, seq_len, _ = x.shape
    cast_in, cast_out, cast_shapes = _riding_casts(casts, n_seq, seq_len // tile)
    kern = _with_casts(functools.partial(_ffn_kernel, final_norm=final_norm), 6, 1,
                       [kind for _, _, kind in casts])
    res = pl.pallas_call(
        kern,
        out_shape=(jax.ShapeDtypeStruct(x.shape, F32), *cast_shapes),
        grid=(n_seq, seq_len // tile),
        in_specs=[
            pl.BlockSpec((None, tile, D_MODEL), lambda s, t: (s, t, 0)),
            _mod_spec(layer, row0, row_stride),
            _const_spec((1, D_MODEL)),
            _const_spec((D_MODEL, 2 * D_FF)),
            _const_spec((D_FF, D_MODEL)),
            _const_spec((1, D_MODEL)),
            *cast_in,
        ],
        out_specs=(pl.BlockSpec((None, tile, D_MODEL), lambda s, t: (s, t, 0)), *cast_out),
        compiler_params=_params(("arbitrary", "arbitrary")),
        name="ffn",
    )(x, mod, g, w_in, w_out, final_g, *[c[0] for c in casts])
    return res[0], list(res[1:])


def _rope_lane_order(a):
    lead = a.shape[:-1]
    return a.reshape(*lead, 2, 2, AXIS_DIM // 2).swapaxes(-3, -2).reshape(*lead, HEAD_DIM)


def _rope(a, cos, sin_signed):
    return a * cos + pltpu.roll(a, HEAD_DIM // 2, 1) * sin_signed


def _pair_mean_matrix():
    r = lax.broadcasted_iota(jnp.int32, (2 * HEAD_DIM, 2 * HEAD_DIM), 0) // HEAD_DIM
    c = lax.broadcasted_iota(jnp.int32, (2 * HEAD_DIM, 2 * HEAD_DIM), 1) // HEAD_DIM
    return jnp.where(r == c, 1.0 / HEAD_DIM, 0.0).astype(BF16)


def _rope_lane_matrix():
    src = lax.broadcasted_iota(jnp.int32, (HEAD_DIM, HEAD_DIM), 0)
    n = lax.broadcasted_iota(jnp.int32, (HEAD_DIM, HEAD_DIM), 1)
    quarter = AXIS_DIM // 2
    half, axis, c = n // AXIS_DIM, (n // quarter) % 2, n % quarter
    return jnp.where(src == axis * AXIS_DIM + half * quarter + c, 1.0, 0.0).astype(BF16)


def _qkv_kernel(*refs, rope, cached, keep_f32):
    it = iter(refs)
    x_ref, mod_ref, g_ref, w_ref, qg_ref, kg_ref = (next(it) for _ in range(6))
    rope_refs = [next(it) for _ in range(2)] if rope else None
    cache_refs = [next(it) for _ in range(2)] if cached else None
    q_ref, kh_ref, vt_ref = (next(it) for _ in range(3))
    cache_out_refs = [next(it) for _ in range(2)] if cached else None
    f32_refs = [next(it) for _ in range(2)] if keep_f32 else None

    if cached:
        @pl.when(pl.program_id(1) == 0)
        def _():
            for kv in range(N_KV_HEADS):
                k = cache_refs[0][:, kv, :].astype(BF16)
                if rope:
                    k = _dot(k, _rope_lane_matrix()).astype(BF16)
                cache_out_refs[0][kv] = k
                cache_out_refs[1][kv] = cache_refs[1][:, kv, :].T.astype(BF16)

    mod = mod_ref[...]
    shift, scale = mod[0:1], mod[1:2]
    g = g_ref[...]
    gains = [qg_ref[...] * (SM_SCALE * LOG2E)] * N_HEADS + [kg_ref[...]] * N_KV_HEADS
    pair_mean = _pair_mean_matrix()
    v0 = (N_HEADS + N_KV_HEADS) * HEAD_DIM
    sub_rows = min(QKV_SUB_ROWS, x_ref.shape[0])
    n_sub = x_ref.shape[0] // sub_rows

    def rows_of(sub):
        return slice(sub * sub_rows, (sub + 1) * sub_rows)

    def projection(sub):
        h = _modulate(x_ref[rows_of(sub), :], g, shift, scale).astype(BF16)
        return _dot(h, w_ref[...])

    def finish(sub, p):
        rows = rows_of(sub)
        if rope:
            tables = [r[rows, :] for r in rope_refs]
        for pair in range((N_HEADS + N_KV_HEADS) // 2):
            blk = p[:, pair * 2 * HEAD_DIM:(pair + 1) * 2 * HEAD_DIM]
            ms = _dot((blk * blk).astype(BF16), pair_mean)
            normed = blk * lax.rsqrt(ms + EPS)
            for half in range(2):
                hd = 2 * pair + half
                a = normed[:, half * HEAD_DIM:(half + 1) * HEAD_DIM] * gains[hd]
                if rope:
                    a = _rope(a, *tables)
                if hd < N_HEADS:
                    q_ref[hd, rows, :] = a.astype(BF16)
                else:
                    kv = hd - N_HEADS
                    v = p[:, v0 + kv * HEAD_DIM:v0 + (kv + 1) * HEAD_DIM]
                    kh_ref[kv, rows, :] = a.astype(BF16)
                    vt_ref[kv, :, rows] = v.T.astype(BF16)
                    if keep_f32:
                        f32_refs[0][rows, kv, :] = a
                        f32_refs[1][rows, kv, :] = v

    p_next = projection(0)
    for sub in range(n_sub):
        p_cur = p_next
        if sub + 1 < n_sub:
            p_next = projection(sub + 1)
        finish(sub, p_cur)


def _qkv_proj(x, mod, layer, row0, row_stride, g, w_qkv, q_gain, k_gain, rope_tables, cache,
              cache_layer, keep_f32, tile):
    n_seq, seq_len, _ = x.shape
    rope = rope_tables is not None
    cached = cache is not None
    in_specs = [
        pl.BlockSpec((None, tile, D_MODEL), lambda s, t: (s, t, 0)),
        _mod_spec(layer, row0, row_stride),
        _const_spec((1, D_MODEL)),
        _const_spec((D_MODEL, QKV_WIDTH)),
        _const_spec((1, HEAD_DIM)),
        _const_spec((1, HEAD_DIM)),
    ]
    args = [x, mod, g, w_qkv, q_gain, k_gain]
    if rope:
        in_specs += [pl.BlockSpec((tile, HEAD_DIM), lambda s, t: (t, 0))] * 2
        args += list(rope_tables)
    out_shape = [jax.ShapeDtypeStruct((n_seq, N_HEADS, seq_len, HEAD_DIM), BF16),
                 jax.ShapeDtypeStruct((n_seq, N_KV_HEADS, seq_len, HEAD_DIM), BF16),
                 jax.ShapeDtypeStruct((n_seq, N_KV_HEADS, HEAD_DIM, seq_len), BF16)]
    out_specs = [pl.BlockSpec((None, N_HEADS, tile, HEAD_DIM), lambda s, t: (s, 0, t, 0)),
                 pl.BlockSpec((None, N_KV_HEADS, tile, HEAD_DIM), lambda s, t: (s, 0, t, 0)),
                 pl.BlockSpec((None, N_KV_HEADS, HEAD_DIM, tile), lambda s, t: (s, 0, 0, t))]
    if cached:
        past = cache[0].shape[2]
        in_specs += [pl.BlockSpec((None, None, past, N_KV_HEADS, HEAD_DIM),
                                  lambda s, t: (s, cache_layer, 0, 0, 0))] * 2
        args += list(cache)
        out_shape += [jax.ShapeDtypeStruct((n_seq, N_KV_HEADS, past, HEAD_DIM), BF16),
                      jax.ShapeDtypeStruct((n_seq, N_KV_HEADS, HEAD_DIM, past), BF16)]
        out_specs += [pl.BlockSpec((None, N_KV_HEADS, past, HEAD_DIM), lambda s, t: (s, 0, 0, 0)),
                      pl.BlockSpec((None, N_KV_HEADS, HEAD_DIM, past), lambda s, t: (s, 0, 0, 0))]
    if keep_f32:
        out_shape += [jax.ShapeDtypeStruct((n_seq, 1, seq_len, N_KV_HEADS, HEAD_DIM), F32)] * 2
        out_specs += [pl.BlockSpec((None, None, tile, N_KV_HEADS, HEAD_DIM),
                                   lambda s, t: (s, 0, t, 0, 0))] * 2
    return pl.pallas_call(
        functools.partial(_qkv_kernel, rope=rope, cached=cached, keep_f32=keep_f32),
        out_shape=tuple(out_shape),
        grid=(n_seq, seq_len // tile),
        in_specs=in_specs,
        out_specs=tuple(out_specs),
        compiler_params=_params(("arbitrary", "arbitrary")),
        name="qkv_proj",
    )(*args)


def _attn_kernel(*refs, n_segments, key_chunk, inner):
    q_ref = refs[0]
    segments = [(refs[1 + 2 * i], refs[2 + 2 * i]) for i in range(n_segments)]
    x_ref, mod_ref, w_o_ref, qg_ref, o_ref, heads_scr, shift_scr, safe_scr = refs[1 + 2 * n_segments:]
    group = N_HEADS // N_KV_HEADS
    tile = q_ref.shape[1]
    q_rows = min(ATTN_Q_ROWS, tile) if inner is None else inner
    n_parts = tile // q_rows

    def chunks_of(part):
        if inner is None:
            return [(seg, c0, min(key_chunk, k_ref.shape[1] - c0))
                    for seg, (k_ref, _) in enumerate(segments)
                    for c0 in range(0, k_ref.shape[1], key_chunk)]
        return [(0, part * inner + c0, min(key_chunk, inner - c0))
                for c0 in range(0, inner, key_chunk)]

    @pl.when(pl.program_id(1) == 0)
    def _():
        qg = qg_ref[...]
        q_sq = HEAD_DIM * (SM_SCALE * LOG2E) ** 2 * jnp.max(qg * qg)
        for kv in range(N_KV_HEADS):
            k_sq = None
            for k_ref, _ in segments:
                k = k_ref[kv].astype(F32)
                seg_sq = jnp.max(jnp.sum(k * k, axis=1, keepdims=True))
                k_sq = seg_sq if k_sq is None else jnp.maximum(k_sq, seg_sq)
            bound_sq = q_sq * k_sq * BOUND_SLACK
            shift_scr[kv] = jnp.full(shift_scr.shape[1:], jnp.sqrt(bound_sq), F32)
            safe_scr[kv] = (bound_sq <= MAX_SAFE_SHIFT ** 2).astype(jnp.int32)

    def bounded():
        items = [(part, hd, ci, chunk, ci == len(chunks_of(part)) - 1)
                 for part in range(n_parts) for hd in range(N_HEADS)
                 for ci, chunk in enumerate(chunks_of(part))]
        scores = {}

        def issue_scores(j):
            part, hd, _, (seg, c0, rows), _ = items[j]
            keys = segments[seg][0][hd // group, c0:c0 + rows, :]
            q = q_ref[hd, part * q_rows:(part + 1) * q_rows, :]
            scores[j] = _dot_nt(keys, q)

        for j in range(min(ATTN_LOOKAHEAD, len(items))):
            issue_scores(j)
        for i, (part, hd, ci, (seg, c0, rows), last) in enumerate(items):
            if i + ATTN_LOOKAHEAD < len(items):
                issue_scores(i + ATTN_LOOKAHEAD)
            p = jnp.exp2(scores.pop(i) - shift_scr[hd // group][0:1, 0:1])
            ps = jnp.sum(p, axis=0, keepdims=True)
            pv = _dot(segments[seg][1][hd // group, :, c0:c0 + rows], p.astype(BF16))
            denom, acc = (ps, pv) if ci == 0 else (denom + ps, acc + pv)
            if last:
                heads_scr[hd, part * q_rows:(part + 1) * q_rows, :] = (
                    (acc * (1.0 / denom)).T.astype(BF16))

    def exact():
        def one_head(hd, carry):
            kv = hd // group
            for part in range(n_parts):
                rows_q = slice(part * q_rows, (part + 1) * q_rows)
                q = q_ref[hd, rows_q, :]
                if inner is None:
                    key_sets = [(k_ref[kv], vt_ref[kv]) for k_ref, vt_ref in segments]
                else:
                    keys = slice(part * inner, (part + 1) * inner)
                    key_sets = [(segments[0][0][kv, keys, :], segments[0][1][kv, :, keys])]
                scores = [_dot_nt(k, q) for k, _ in key_sets]
                m = None
                for s in scores:
                    sm = jnp.max(s, axis=0, keepdims=True)
                    m = sm if m is None else jnp.maximum(m, sm)
                denom = o_t = None
                for s, (_, vt) in zip(scores, key_sets):
                    p = jnp.exp2(s - m)
                    ps = jnp.sum(p, axis=0, keepdims=True)
                    pv = _dot(vt, p.astype(BF16))
                    denom, o_t = (ps, pv) if denom is None else (denom + ps, o_t + pv)
                heads_scr[hd, rows_q, :] = (o_t * (1.0 / denom)).T.astype(BF16)
            return carry

        lax.fori_loop(0, N_HEADS, one_head, 0)

    safe = safe_scr[0]
    for kv in range(1, N_KV_HEADS):
        safe = jnp.minimum(safe, safe_scr[kv])
    pl.when(safe == 1)(bounded)
    pl.when(safe != 1)(exact)

    attn = jnp.concatenate([heads_scr[hd] for hd in range(N_HEADS)], axis=-1)
    o_ref[...] = x_ref[...] + mod_ref[2:3] * _dot(attn, w_o_ref[...])


def _attn_sublayer(q, key_segments, x, mod, layer, row0, row_stride, w_o, q_gain, tile, inner=None):
    n_seq, seq_len, _ = x.shape
    if inner is not None and inner == seq_len:
        inner = None
    assert inner is None or (len(key_segments) == 1 and tile % inner == 0)
    in_specs = [pl.BlockSpec((None, N_HEADS, tile, HEAD_DIM), lambda s, t: (s, 0, t, 0))]
    args = [q]
    for k, vt in key_segments:
        n_keys = k.shape[2]
        in_specs += [pl.BlockSpec((None, N_KV_HEADS, n_keys, HEAD_DIM), lambda s, t: (s, 0, 0, 0)),
                     pl.BlockSpec((None, N_KV_HEADS, HEAD_DIM, n_keys), lambda s, t: (s, 0, 0, 0))]
        args += [k, vt]
    in_specs += [
        pl.BlockSpec((None, tile, D_MODEL), lambda s, t: (s, t, 0)),
        _mod_spec(layer, row0, row_stride),
        _const_spec((D_MODEL, D_MODEL)),
        _const_spec((1, HEAD_DIM)),
    ]
    args += [x, mod, w_o, q_gain]
    return pl.pallas_call(
        functools.partial(_attn_kernel, n_segments=len(key_segments), key_chunk=ATTN_KEY_CHUNK,
                          inner=inner),
        out_shape=jax.ShapeDtypeStruct(x.shape, F32),
        grid=(n_seq, seq_len // tile),
        in_specs=in_specs,
        out_specs=pl.BlockSpec((None, tile, D_MODEL), lambda s, t: (s, t, 0)),
        scratch_shapes=[pltpu.VMEM((N_HEADS, tile, HEAD_DIM), BF16),
                        pltpu.VMEM((N_KV_HEADS, 8, HEAD_DIM), F32),
                        pltpu.SMEM((N_KV_HEADS,), jnp.int32)],
        compiler_params=_params(("arbitrary", "arbitrary")),
        name="attention",
    )(*args)


def _rope_tables(seq_len):
    rows = seq_len // GRID_W
    r = np.repeat(np.arange(rows), GRID_W).astype(np.float32)
    col = np.tile(np.arange(GRID_W), rows).astype(np.float32)
    expo = -np.arange(0, AXIS_DIM, 2, dtype=np.float32) / np.float32(AXIS_DIM)
    inv = np.power(np.float32(ROPE_THETA), expo).astype(np.float32)
    ar = r[:, None] * inv
    ac = col[:, None] * inv
    ang = np.concatenate([ar, ar, ac, ac], axis=-1)
    cos, sin = np.cos(ang), np.sin(ang)
    first = (np.arange(HEAD_DIM) % AXIS_DIM) < AXIS_DIM // 2
    return (jnp.asarray(_rope_lane_order(cos), F32),
            jnp.asarray(_rope_lane_order(np.where(first, -sin, sin)), F32))


def _rope_weights(w_qkv):
    qk = (N_HEADS + N_KV_HEADS) * HEAD_DIM
    lead = w_qkv.shape[:-1]
    w_qk = _rope_lane_order(w_qkv[..., :qk].reshape(*lead, N_HEADS + N_KV_HEADS, HEAD_DIM))
    return jnp.concatenate([w_qk.reshape(*lead, qk), w_qkv[..., qk:]], axis=-1)


def _run_layer(x, l, mod, grp, wts, ffn_w, ffn_f32, odd_w, odd_f32, ride):
    row0, row_stride = grp["row0"], grp["row_stride"]
    i = l // 2
    g1 = wts["norm_g"][l, 0].reshape(1, D_MODEL)
    g2 = wts["norm_g"][l, 1].reshape(1, D_MODEL)
    new_kv = None
    if l % 2 == 0:
        tile = min(EVEN_TILE, x.shape[1])
        casts = []
        if l == 0 and 0 not in ffn_w:
            steps = x.shape[0] * (x.shape[1] // tile)
            if ride and all(_can_ride(w, steps) for w in ffn_f32 + odd_f32):
                casts = [(w, 0, "plain") for w in ffn_f32]
                for i_odd in range(DEPTH // 2):
                    casts += [(odd_f32[0], i_odd, "plain"), (odd_f32[0], i_odd, "rope"),
                              (odd_f32[1], i_odd, "plain")]
            else:
                ffn_w[0] = tuple(w[0].astype(BF16) for w in ffn_f32)
                for i_odd in range(DEPTH // 2):
                    w_qkv_bf16 = odd_f32[0][i_odd].astype(BF16)
                    odd_w[i_odd] = (w_qkv_bf16, _rope_weights(w_qkv_bf16),
                                    odd_f32[1][i_odd].astype(BF16))
        x, cast_out = _even_sublayer(x, mod, l, row0, row_stride, g1, wts["w_in_even"][i],
                                     wts["pool_w"][i], wts["pool_scale"][i].reshape(1, POOL_WIDTH),
                                     wts["conv_w"][i], wts["conv_b"][i].reshape(1, CONV_WIDTH),
                                     wts["w_out_even"][i], tile, grp["inner"], casts)
        if casts:
            ffn_w[0] = tuple(cast_out[:2])
            for i_odd in range(DEPTH // 2):
                odd_w[i_odd] = tuple(cast_out[2 + 3 * i_odd:5 + 3 * i_odd])
    else:
        cache, rope_tables = grp["cache"], grp["rope_tables"]
        sfx = "" if rope_tables is None else "_rope"
        w_qkv, w_qkv_rope, w_o = odd_w[i]
        res = _qkv_proj(x, mod, l, row0, row_stride, g1, w_qkv if rope_tables is None else w_qkv_rope,
                        wts["q_gain" + sfx][i].reshape(1, HEAD_DIM),
                        wts["k_gain" + sfx][i].reshape(1, HEAD_DIM), rope_tables, cache, i,
                        cache is None, min(QKV_TILE, x.shape[1]))
        q, segments = res[0], [res[1:3]]
        if cache is None:
            new_kv = res[3:]
        else:
            segments.append(res[3:5])
        x = _attn_sublayer(q, segments, x, mod, l, row0, row_stride, w_o,
                           wts["q_gain"][i].reshape(1, HEAD_DIM), grp["attn_tile"], grp["inner"])
    flat = x.reshape(1, -1, D_MODEL) if row_stride == 0 else x
    tile = min(FFN_TILE, flat.shape[1])
    casts = []
    if l + 1 < DEPTH and l + 1 not in ffn_w:
        steps = flat.shape[0] * (flat.shape[1] // tile)
        if ride and all(_can_ride(w, steps) for w in ffn_f32):
            casts = [(w, l + 1, "plain") for w in ffn_f32]
        else:
            ffn_w[l + 1] = tuple(w[l + 1].astype(BF16) for w in ffn_f32)
    flat, cast_out = _ffn_sublayer(flat, mod, l, row0, row_stride, g2, *ffn_w[l],
                                   wts["final_g"].reshape(1, D_MODEL), l == DEPTH - 1, tile, casts)
    if casts:
        ffn_w[l + 1] = tuple(cast_out)
    return flat.reshape(x.shape), new_kv


def kernel(x_prompt, x_sample, cache_k, cache_v, c, c_ctx, w_ada, b_ada, norm_g, w_in_even, pool_w,
           pool_scale, conv_w, conv_b, w_out_even, w_qkv, q_gain, k_gain, w_o, w_ffn_in, w_ffn_out,
           final_g):
    n_ctx, ctx_len, _ = x_prompt.shape
    n_lat, lat_len, _ = x_sample.shape
    assert 1 + n_lat <= COND_ROWS

    cond = jnp.concatenate(
        [c_ctx[None, :], c, jnp.zeros((COND_ROWS - 1 - n_lat, D_MODEL), F32)], axis=0)
    mod = _ada(cond, w_ada, b_ada).reshape(DEPTH, COND_ROWS, 6, D_MODEL)

    wts = dict(
        norm_g=norm_g, final_g=final_g, pool_scale=pool_scale, conv_w=conv_w, conv_b=conv_b,
        q_gain=q_gain, k_gain=k_gain,
        w_in_even=w_in_even.astype(BF16), pool_w=pool_w.astype(BF16),
        w_out_even=w_out_even.astype(BF16),
        q_gain_rope=_rope_lane_order(q_gain), k_gain_rope=_rope_lane_order(k_gain),
    )

    pack = math.gcd(n_ctx, max(1, CTX_PACK_ROWS // ctx_len))
    ctx_grp = dict(row0=0, row_stride=0, cache=None, rope_tables=None, attn_tile=pack * ctx_len,
                   inner=ctx_len)
    lat_grp = dict(row0=1, row_stride=1, cache=(cache_k, cache_v), rope_tables=_rope_tables(lat_len),
                   attn_tile=ATTN_TILE, inner=None)
    ffn_w, ffn_f32 = {}, (w_ffn_in, w_ffn_out)
    odd_w, odd_f32 = {}, (w_qkv, w_o)
    y_prompt = x_prompt.reshape(n_ctx // pack, pack * ctx_len, D_MODEL)
    y_sample, ctx_kv = x_sample, []
    for l in range(DEPTH):
        y_sample, _ = _run_layer(y_sample, l, mod, lat_grp, wts, ffn_w, ffn_f32, odd_w, odd_f32, True)
        y_prompt, kv = _run_layer(y_prompt, l, mod, ctx_grp, wts, ffn_w, ffn_f32, odd_w, odd_f32,
                                  False)
        if kv is not None:
            ctx_kv.append(kv)
    kv_shape = (n_ctx, 1, ctx_len, N_KV_HEADS, HEAD_DIM)
    new_cache_k = jnp.concatenate([k.reshape(kv_shape) for k, _ in ctx_kv], axis=1)
    new_cache_v = jnp.concatenate([v.reshape(kv_shape) for _, v in ctx_kv], axis=1)
    return y_prompt.reshape(x_prompt.shape), y_sample, new_cache_k, new_cache_v
```

```python
import functools
import math

import jax
import jax.numpy as jnp
import numpy as np
from jax import lax
from jax.experimental import pallas as pl
from jax.experimental.pallas import tpu as pltpu

D_MODEL = 1024
DEPTH = 2
GRID_W = 64
EPS = 1e-6
POOL_WIDTH = D_MODEL // 2
POOL_GROUPS = 4
POOL_GROUP_DIM = POOL_WIDTH // POOL_GROUPS
POOL_WINDOWS = (2, 4, 8, 16)
CONV_WIDTH = D_MODEL // 2
EVEN_IN = POOL_WIDTH + 3 * CONV_WIDTH
HEAD_DIM = 128
N_HEADS = D_MODEL // HEAD_DIM
N_KV_HEADS = N_HEADS // 4
QKV_WIDTH = (N_HEADS + 2 * N_KV_HEADS) * HEAD_DIM
ROPE_THETA = 10000.0
AXIS_DIM = HEAD_DIM // 2
D_FF = ((8 * D_MODEL // 3 + 255) // 256) * 256
SM_SCALE = HEAD_DIM ** -0.5
LOG2E = math.log2(math.e)

COND_ROWS = 8
HALO = 16
ADA_BLOCK_N = 3072
FF_CHUNKS = ((0, 1024), (1024, 1024), (2048, 768))
CTX_PACK_ROWS = 1024
QKV_TILE = 2048
QKV_SUB_ROWS = 128
EVEN_TILE = 1024
EVEN_SUB_ROWS = 512
FFN_TILE = 1024
FFN_SUB_ROWS = 256
FFN_LOOKAHEAD = 1
ATTN_TILE = 512
ATTN_Q_ROWS = 256
ATTN_KEY_CHUNK = 512
ATTN_LOOKAHEAD = 2
BOUND_SLACK = 1.1
MAX_SAFE_SHIFT = 60.0
V7X_VMEM_BYTES = 64 * 1024 * 1024
VMEM_LIMIT_BYTES = V7X_VMEM_BYTES - 8 * 1024 * 1024

BF16 = jnp.bfloat16
F32 = jnp.float32


def _dot(a, b):
    return jnp.dot(a, b, preferred_element_type=F32)


def _dot_nt(a, b):
    return lax.dot_general(a, b, (((1,), (1,)), ((), ())), preferred_element_type=F32)


def _silu(a):
    return a * (1.0 / (1.0 + jnp.exp(-a)))


def _rms(x):
    return x * lax.rsqrt(jnp.mean(x * x, axis=-1, keepdims=True) + EPS)


def _modulate(x, g, shift, scale):
    return (_rms(x) * g) * (1.0 + scale) + shift


def _params(sem):
    return pltpu.CompilerParams(dimension_semantics=sem, vmem_limit_bytes=VMEM_LIMIT_BYTES)


def _const_spec(shape):
    n = len(shape)
    return pl.BlockSpec(shape, lambda *_: (0,) * n, pipeline_mode=pl.Buffered(1))


def _riding_casts(casts, n_seq, n_t):
    steps = n_seq * n_t
    in_specs, out_specs, out_shapes = [], [], []
    for w, layer, _ in casts:
        rows = w.shape[1] // steps
        in_specs.append(pl.BlockSpec((None, rows, w.shape[2]),
                                     lambda s, t, layer=layer: (layer, s * n_t + t, 0)))
        out_specs.append(pl.BlockSpec((rows, w.shape[2]), lambda s, t: (s * n_t + t, 0)))
        out_shapes.append(jax.ShapeDtypeStruct(w.shape[1:], BF16))
    return in_specs, out_specs, out_shapes


def _can_ride(w, steps):
    return w.shape[1] % steps == 0 and (w.shape[1] // steps) % 16 == 0


def _rope_lane_order_cols(w):
    quarter_w = AXIS_DIM // 2
    lane = lax.broadcasted_iota(jnp.int32, w.shape, 1)
    quarter = (lane % HEAD_DIM) // quarter_w
    qk = lane < (N_HEADS + N_KV_HEADS) * HEAD_DIM
    from_right = pltpu.roll(w, w.shape[1] - quarter_w, 1)
    from_left = pltpu.roll(w, quarter_w, 1)
    swapped = jnp.where(quarter == 1, from_right, jnp.where(quarter == 2, from_left, w))
    return jnp.where(qk, swapped, w)


def _with_casts(body, n_in, n_out, kinds):
    n_cast = len(kinds)

    def run(*refs):
        ins, cast_ins = refs[:n_in], refs[n_in:n_in + n_cast]
        outs = refs[n_in + n_cast:n_in + n_cast + n_out]
        cast_outs = refs[n_in + n_cast + n_out:n_in + 2 * n_cast + n_out]
        for src, dst, kind in zip(cast_ins, cast_outs, kinds):
            w = src[...]
            dst[...] = (_rope_lane_order_cols(w) if kind == "rope" else w).astype(dst.dtype)
        body(*ins, *outs, *refs[n_in + 2 * n_cast + n_out:])
    return run


def _ada_kernel(cond_ref, w_ref, b_ref, o_ref):
    s = _silu(cond_ref[...]).astype(BF16)
    o_ref[...] = _dot(s, w_ref[...].astype(BF16)) + b_ref[...]


def _ada(cond, w_ada, b_ada):
    n_out = w_ada.shape[-1]
    return pl.pallas_call(
        _ada_kernel,
        out_shape=jax.ShapeDtypeStruct((DEPTH, COND_ROWS, n_out), F32),
        grid=(DEPTH, n_out // ADA_BLOCK_N),
        in_specs=[
            pl.BlockSpec((COND_ROWS, D_MODEL), lambda l, j: (0, 0)),
            pl.BlockSpec((None, D_MODEL, ADA_BLOCK_N), lambda l, j: (l, 0, j)),
            pl.BlockSpec((None, 1, ADA_BLOCK_N), lambda l, j: (l, 0, j)),
        ],
        out_specs=pl.BlockSpec((None, COND_ROWS, ADA_BLOCK_N), lambda l, j: (l, 0, j)),
        compiler_params=_params(("arbitrary", "arbitrary")),
        name="ada",
    )(cond, w_ada, b_ada.reshape(DEPTH, 1, n_out))


def _mod_spec(layer, row0, row_stride):
    return pl.BlockSpec((None, None, 6, D_MODEL),
                        lambda s, t: (layer, row0 + row_stride * s, 0, 0))


def _even_kernel(x_ref, xp_ref, xn_ref, mod_ref, g_ref, w_in_ref, pool_w_ref, pool_scale_ref,
                 conv_w_ref, conv_b_ref, w_out_ref, o_ref, *, tile, seq_len, packed):
    t = pl.program_id(1)
    n_t = pl.num_programs(1)
    sub = seq_len if packed else min(EVEN_SUB_ROWS, tile)
    n_sub = tile // sub
    win = sub + 2 * HALO
    mod = mod_ref[...]
    shift, scale, gate = mod[0:1], mod[1:2], mod[2:3]
    g = g_ref[...]
    cw = conv_w_ref[...]

    def h_rows(lo, hi):
        pieces = []
        if lo < HALO:
            pieces.append(xp_ref[...])
        pieces.append(x_ref[max(lo, HALO) - HALO:min(hi, HALO + tile) - HALO, :])
        if hi > HALO + tile:
            pieces.append(xn_ref[...])
        return jnp.concatenate([_modulate(r, g, shift, scale).astype(BF16) for r in pieces], axis=0)

    def project(s):
        if packed:
            rows = x_ref[s * sub:(s + 1) * sub, :]
            return _dot(_modulate(rows, g, shift, scale).astype(BF16), w_in_ref[...])
        lo = 0 if s == 0 else s * sub + 2 * HALO
        return _dot(h_rows(lo, (s + 1) * sub + 2 * HALO), w_in_ref[...])

    def edge_masked(a, s):
        if packed:
            return a
        parts = [a[:HALO], a[HALO:HALO + sub], a[HALO + sub:]]
        if s == 0:
            parts[0] = jnp.where(t > 0, parts[0], 0.0)
        if s == n_sub - 1:
            parts[2] = jnp.where(t < n_t - 1, parts[2], 0.0)
        return jnp.concatenate(parts, axis=0)

    def shifted(a, k):
        return pltpu.roll(a, k % win, 0)

    def mixers(s, p):
        first = 0 if packed else t * tile + s * sub
        pos = first + lax.broadcasted_iota(jnp.int32, (sub, 1), 0)
        u = edge_masked(p[:, :POOL_WIDTH], s)
        ya = []
        for i, w in enumerate(POOL_WINDOWS):
            ug = u[:, i * POOL_GROUP_DIM:(i + 1) * POOL_GROUP_DIM]
            fwd, n = ug, 1
            while n < w // 2:
                fwd = fwd + shifted(fwd, -n)
                n *= 2
            wsum = shifted(fwd, w // 2) + fwd
            cnt = (jnp.clip(pos + w // 2, 0, seq_len)
                   - jnp.clip(pos - w // 2, 0, seq_len)).astype(F32)
            pooled = wsum[HALO:HALO + sub] / cnt - ug[HALO:HALO + sub]
            ya.append(_dot(pooled.astype(BF16), pool_w_ref[i]))
        ya = jnp.concatenate(ya, axis=-1) * pool_scale_ref[...]
        c0 = POOL_WIDTH
        bg = p[HALO:HALO + sub, c0:c0 + CONV_WIDTH]
        z = edge_masked(p[:, c0 + CONV_WIDTH:c0 + 2 * CONV_WIDTH] * p[:, c0 + 2 * CONV_WIDTH:], s)
        conv = (shifted(z, 1) * cw[0:1] + z * cw[1:2] + shifted(z, -1) * cw[2:3])[HALO:HALO + sub]
        yb = bg * (conv + conv_b_ref[...])
        y = _dot(jnp.concatenate([ya, yb], axis=-1).astype(BF16), w_out_ref[...])
        rows = slice(s * sub, (s + 1) * sub)
        o_ref[rows, :] = x_ref[rows, :] + gate * y

    block = project(0)
    window = None
    margin = jnp.zeros((HALO, EVEN_IN), F32)
    for s in range(n_sub):
        if packed:
            window = jnp.concatenate([margin, block, margin], axis=0)
        else:
            window = block if s == 0 else jnp.concatenate([window[sub:], block], axis=0)
        if s + 1 < n_sub:
            block = project(s + 1)
        mixers(s, window)


def _even_sublayer(x, mod, layer, row0, row_stride, g, w_in, pool_w, pool_scale, conv_w, conv_b,
                   w_out, tile, inner=None, casts=()):
    n_seq, n_rows, _ = x.shape
    packed = inner is not None and inner < n_rows
    assert not packed or tile % inner == 0
    n_t = n_rows // tile
    hb = tile // HALO
    n_hb = n_rows // HALO
    cast_in, cast_out, cast_shapes = _riding_casts(casts, n_seq, n_t)
    kern = _with_casts(functools.partial(_even_kernel, tile=tile, packed=packed,
                                         seq_len=inner if packed else n_rows), 11, 1,
                       [kind for _, _, kind in casts])
    res = pl.pallas_call(
        kern,
        out_shape=(jax.ShapeDtypeStruct(x.shape, F32), *cast_shapes),
        grid=(n_seq, n_t),
        in_specs=[
            pl.BlockSpec((None, tile, D_MODEL), lambda s, t: (s, t, 0)),
            pl.BlockSpec((None, HALO, D_MODEL), lambda s, t: (s, jnp.maximum(t * hb - 1, 0), 0)),
            pl.BlockSpec((None, HALO, D_MODEL),
                         lambda s, t: (s, jnp.minimum((t + 1) * hb, n_hb - 1), 0)),
            _mod_spec(layer, row0, row_stride),
            _const_spec((1, D_MODEL)),
            _const_spec((D_MODEL, EVEN_IN)),
            _const_spec((POOL_GROUPS, POOL_GROUP_DIM, POOL_GROUP_DIM)),
            _const_spec((1, POOL_WIDTH)),
            _const_spec((3, CONV_WIDTH)),
            _const_spec((1, CONV_WIDTH)),
            _const_spec((POOL_WIDTH + CONV_WIDTH, D_MODEL)),
            *cast_in,
        ],
        out_specs=(pl.BlockSpec((None, tile, D_MODEL), lambda s, t: (s, t, 0)), *cast_out),
        compiler_params=_params(("arbitrary", "arbitrary")),
        name="even_mixer",
    )(x, x, x, mod, g, w_in, pool_w, pool_scale, conv_w, conv_b, w_out, *[c[0] for c in casts])
    return res[0], list(res[1:])


def _ffn_kernel(x_ref, mod_ref, g_ref, w_in_ref, w_out_ref, fg_ref, o_ref, *, final_norm):
    mod = mod_ref[...]
    shift, scale, gate = mod[3:4], mod[4:5], mod[5:6]
    g = g_ref[...]
    n_sub = x_ref.shape[0] // FFN_SUB_ROWS
    items = [(sub, c) for sub in range(n_sub) for c in range(len(FF_CHUNKS))]
    hs, ups = {}, {}

    def issue_up(j):
        sub, c = items[j]
        if c == 0:
            rows = x_ref[sub * FFN_SUB_ROWS:(sub + 1) * FFN_SUB_ROWS, :]
            hs[sub] = _modulate(rows, g, shift, scale).astype(BF16)
        c0, cw = FF_CHUNKS[c]
        ups[j] = (_dot(hs[sub], w_in_ref[:, c0:c0 + cw]),
                  _dot(hs[sub], w_in_ref[:, D_FF + c0:D_FF + c0 + cw]))

    for j in range(min(FFN_LOOKAHEAD, len(items))):
        issue_up(j)
    for i, (sub, c) in enumerate(items):
        if i + FFN_LOOKAHEAD < len(items):
            issue_up(i + FFN_LOOKAHEAD)
        a, b = ups.pop(i)
        c0, cw = FF_CHUNKS[c]
        part = _dot((_silu(a) * b).astype(BF16), w_out_ref[c0:c0 + cw, :])
        acc = part if c == 0 else acc + part
        if c == len(FF_CHUNKS) - 1:
            rows = slice(sub * FFN_SUB_ROWS, (sub + 1) * FFN_SUB_ROWS)
            out = x_ref[rows, :] + gate * acc
            if final_norm:
                out = _rms(out) * fg_ref[...]
            o_ref[rows, :] = out


def _ffn_sublayer(x, mod, layer, row0, row_stride, g, w_in, w_out, final_g, final_norm, tile,
                  casts=()):
    n_seq, seq_len, _ = x.shape
    cast_in, cast_out, cast_shapes = _riding_casts(casts, n_seq, seq_len // tile)
    kern = _with_casts(functools.partial(_ffn_kernel, final_norm=final_norm), 6, 1,
                       [kind for _, _, kind in casts])
    res = pl.pallas_call(
        kern,
        out_shape=(jax.ShapeDtypeStruct(x.shape, F32), *cast_shapes),
        grid=(n_seq, seq_len // tile),
        in_specs=[
            pl.BlockSpec((None, tile, D_MODEL), lambda s, t: (s, t, 0)),
            _mod_spec(layer, row0, row_stride),
            _const_spec((1, D_MODEL)),
            _const_spec((D_MODEL, 2 * D_FF)),
            _const_spec((D_FF, D_MODEL)),
            _const_spec((1, D_MODEL)),
            *cast_in,
        ],
        out_specs=(pl.BlockSpec((None, tile, D_MODEL), lambda s, t: (s, t, 0)), *cast_out),
        compiler_params=_params(("arbitrary", "arbitrary")),
        name="ffn",
    )(x, mod, g, w_in, w_out, final_g, *[c[0] for c in casts])
    return res[0], list(res[1:])


def _rope_lane_order(a):
    lead = a.shape[:-1]
    return a.reshape(*lead, 2, 2, AXIS_DIM // 2).swapaxes(-3, -2).reshape(*lead, HEAD_DIM)


def _rope(a, cos, sin_signed):
    return a * cos + pltpu.roll(a, HEAD_DIM // 2, 1) * sin_signed


def _pair_mean_matrix():
    r = lax.broadcasted_iota(jnp.int32, (2 * HEAD_DIM, 2 * HEAD_DIM), 0) // HEAD_DIM
    c = lax.broadcasted_iota(jnp.int32, (2 * HEAD_DIM, 2 * HEAD_DIM), 1) // HEAD_DIM
    return jnp.where(r == c, 1.0 / HEAD_DIM, 0.0).astype(BF16)


def _rope_lane_matrix():
    src = lax.broadcasted_iota(jnp.int32, (HEAD_DIM, HEAD_DIM), 0)
    n = lax.broadcasted_iota(jnp.int32, (HEAD_DIM, HEAD_DIM), 1)
    quarter = AXIS_DIM // 2
    half, axis, c = n // AXIS_DIM, (n // quarter) % 2, n % quarter
    return jnp.where(src == axis * AXIS_DIM + half * quarter + c, 1.0, 0.0).astype(BF16)


def _qkv_kernel(*refs, rope, cached, keep_f32):
    it = iter(refs)
    x_ref, mod_ref, g_ref, w_ref, qg_ref, kg_ref = (next(it) for _ in range(6))
    rope_refs = [next(it) for _ in range(2)] if rope else None
    cache_refs = [next(it) for _ in range(2)] if cached else None
    q_ref, kh_ref, vt_ref = (next(it) for _ in range(3))
    cache_out_refs = [next(it) for _ in range(2)] if cached else None
    f32_refs = [next(it) for _ in range(2)] if keep_f32 else None

    if cached:
        @pl.when(pl.program_id(1) == 0)
        def _():
            for kv in range(N_KV_HEADS):
                k = cache_refs[0][:, kv, :].astype(BF16)
                if rope:
                    k = _dot(k, _rope_lane_matrix()).astype(BF16)
                cache_out_refs[0][kv] = k
                cache_out_refs[1][kv] = cache_refs[1][:, kv, :].T.astype(BF16)

    mod = mod_ref[...]
    shift, scale = mod[0:1], mod[1:2]
    g = g_ref[...]
    gains = [qg_ref[...] * (SM_SCALE * LOG2E)] * N_HEADS + [kg_ref[...]] * N_KV_HEADS
    pair_mean = _pair_mean_matrix()
    v0 = (N_HEADS + N_KV_HEADS) * HEAD_DIM
    sub_rows = min(QKV_SUB_ROWS, x_ref.shape[0])
    n_sub = x_ref.shape[0] // sub_rows

    def rows_of(sub):
        return slice(sub * sub_rows, (sub + 1) * sub_rows)

    def projection(sub):
        h = _modulate(x_ref[rows_of(sub), :], g, shift, scale).astype(BF16)
        return _dot(h, w_ref[...])

    def finish(sub, p):
        rows = rows_of(sub)
        if rope:
            tables = [r[rows, :] for r in rope_refs]
        for pair in range((N_HEADS + N_KV_HEADS) // 2):
            blk = p[:, pair * 2 * HEAD_DIM:(pair + 1) * 2 * HEAD_DIM]
            ms = _dot((blk * blk).astype(BF16), pair_mean)
            normed = blk * lax.rsqrt(ms + EPS)
            for half in range(2):
                hd = 2 * pair + half
                a = normed[:, half * HEAD_DIM:(half + 1) * HEAD_DIM] * gains[hd]
                if rope:
                    a = _rope(a, *tables)
                if hd < N_HEADS:
                    q_ref[hd, rows, :] = a.astype(BF16)
                else:
                    kv = hd - N_HEADS
                    v = p[:, v0 + kv * HEAD_DIM:v0 + (kv + 1) * HEAD_DIM]
                    kh_ref[kv, rows, :] = a.astype(BF16)
                    vt_ref[kv, :, rows] = v.T.astype(BF16)
                    if keep_f32:
                        f32_refs[0][rows, kv, :] = a
                        f32_refs[1][rows, kv, :] = v

    p_next = projection(0)
    for sub in range(n_sub):
        p_cur = p_next
        if sub + 1 < n_sub:
            p_next = projection(sub + 1)
        finish(sub, p_cur)


def _qkv_proj(x, mod, layer, row0, row_stride, g, w_qkv, q_gain, k_gain, rope_tables, cache,
              cache_layer, keep_f32, tile):
    n_seq, seq_len, _ = x.shape
    rope = rope_tables is not None
    cached = cache is not None
    in_specs = [
        pl.BlockSpec((None, tile, D_MODEL), lambda s, t: (s, t, 0)),
        _mod_spec(layer, row0, row_stride),
        _const_spec((1, D_MODEL)),
        _const_spec((D_MODEL, QKV_WIDTH)),
        _const_spec((1, HEAD_DIM)),
        _const_spec((1, HEAD_DIM)),
    ]
    args = [x, mod, g, w_qkv, q_gain, k_gain]
    if rope:
        in_specs += [pl.BlockSpec((tile, HEAD_DIM), lambda s, t: (t, 0))] * 2
        args += list(rope_tables)
    out_shape = [jax.ShapeDtypeStruct((n_seq, N_HEADS, seq_len, HEAD_DIM), BF16),
                 jax.ShapeDtypeStruct((n_seq, N_KV_HEADS, seq_len, HEAD_DIM), BF16),
                 jax.ShapeDtypeStruct((n_seq, N_KV_HEADS, HEAD_DIM, seq_len), BF16)]
    out_specs = [pl.BlockSpec((None, N_HEADS, tile, HEAD_DIM), lambda s, t: (s, 0, t, 0)),
                 pl.BlockSpec((None, N_KV_HEADS, tile, HEAD_DIM), lambda s, t: (s, 0, t, 0)),
                 pl.BlockSpec((None, N_KV_HEADS, HEAD_DIM, tile), lambda s, t: (s, 0, 0, t))]
    if cached:
        past = cache[0].shape[2]
        in_specs += [pl.BlockSpec((None, None, past, N_KV_HEADS, HEAD_DIM),
                                  lambda s, t: (s, cache_layer, 0, 0, 0))] * 2
        args += list(cache)
        out_shape += [jax.ShapeDtypeStruct((n_seq, N_KV_HEADS, past, HEAD_DIM), BF16),
                      jax.ShapeDtypeStruct((n_seq, N_KV_HEADS, HEAD_DIM, past), BF16)]
        out_specs += [pl.BlockSpec((None, N_KV_HEADS, past, HEAD_DIM), lambda s, t: (s, 0, 0, 0)),
                      pl.BlockSpec((None, N_KV_HEADS, HEAD_DIM, past), lambda s, t: (s, 0, 0, 0))]
    if keep_f32:
        out_shape += [jax.ShapeDtypeStruct((n_seq, 1, seq_len, N_KV_HEADS, HEAD_DIM), F32)] * 2
        out_specs += [pl.BlockSpec((None, None, tile, N_KV_HEADS, HEAD_DIM),
                                   lambda s, t: (s, 0, t, 0, 0))] * 2
    return pl.pallas_call(
        functools.partial(_qkv_kernel, rope=rope, cached=cached, keep_f32=keep_f32),
        out_shape=tuple(out_shape),
        grid=(n_seq, seq_len // tile),
        in_specs=in_specs,
        out_specs=tuple(out_specs),
        compiler_params=_params(("arbitrary", "arbitrary")),
        name="qkv_proj",
    )(*args)


def _attn_kernel(*refs, n_segments, key_chunk, inner):
    q_ref = refs[0]
    segments = [(refs[1 + 2 * i], refs[2 + 2 * i]) for i in range(n_segments)]
    x_ref, mod_ref, w_o_ref, qg_ref, o_ref, heads_scr, shift_scr, safe_scr = refs[1 + 2 * n_segments:]
    group = N_HEADS // N_KV_HEADS
    tile = q_ref.shape[1]
    q_rows = min(ATTN_Q_ROWS, tile) if inner is None else inner
    n_parts = tile // q_rows

    def chunks_of(part):
        if inner is None:
            return [(seg, c0, min(key_chunk, k_ref.shape[1] - c0))
                    for seg, (k_ref, _) in enumerate(segments)
                    for c0 in range(0, k_ref.shape[1], key_chunk)]
        return [(0, part * inner + c0, min(key_chunk, inner - c0))
                for c0 in range(0, inner, key_chunk)]

    @pl.when(pl.program_id(1) == 0)
    def _():
        qg = qg_ref[...]
        q_sq = HEAD_DIM * (SM_SCALE * LOG2E) ** 2 * jnp.max(qg * qg)
        for kv in range(N_KV_HEADS):
            k_sq = None
            for k_ref, _ in segments:
                k = k_ref[kv].astype(F32)
                seg_sq = jnp.max(jnp.sum(k * k, axis=1, keepdims=True))
                k_sq = seg_sq if k_sq is None else jnp.maximum(k_sq, seg_sq)
            bound_sq = q_sq * k_sq * BOUND_SLACK
            shift_scr[kv] = jnp.full(shift_scr.shape[1:], jnp.sqrt(bound_sq), F32)
            safe_scr[kv] = (bound_sq <= MAX_SAFE_SHIFT ** 2).astype(jnp.int32)

    def bounded():
        items = [(part, hd, ci, chunk, ci == len(chunks_of(part)) - 1)
                 for part in range(n_parts) for hd in range(N_HEADS)
                 for ci, chunk in enumerate(chunks_of(part))]
        scores = {}

        def issue_scores(j):
            part, hd, _, (seg, c0, rows), _ = items[j]
            keys = segments[seg][0][hd // group, c0:c0 + rows, :]
            q = q_ref[hd, part * q_rows:(part + 1) * q_rows, :]
            scores[j] = _dot_nt(keys, q)

        for j in range(min(ATTN_LOOKAHEAD, len(items))):
            issue_scores(j)
        for i, (part, hd, ci, (seg, c0, rows), last) in enumerate(items):
            if i + ATTN_LOOKAHEAD < len(items):
                issue_scores(i + ATTN_LOOKAHEAD)
            p = jnp.exp2(scores.pop(i) - shift_scr[hd // group][0:1, 0:1])
            ps = jnp.sum(p, axis=0, keepdims=True)
            pv = _dot(segments[seg][1][hd // group, :, c0:c0 + rows], p.astype(BF16))
            denom, acc = (ps, pv) if ci == 0 else (denom + ps, acc + pv)
            if last:
                heads_scr[hd, part * q_rows:(part + 1) * q_rows, :] = (
                    (acc * (1.0 / denom)).T.astype(BF16))

    def exact():
        def one_head(hd, carry):
            kv = hd // group
            for part in range(n_parts):
                rows_q = slice(part * q_rows, (part + 1) * q_rows)
                q = q_ref[hd, rows_q, :]
                if inner is None:
                    key_sets = [(k_ref[kv], vt_ref[kv]) for k_ref, vt_ref in segments]
                else:
                    keys = slice(part * inner, (part + 1) * inner)
                    key_sets = [(segments[0][0][kv, keys, :], segments[0][1][kv, :, keys])]
                scores = [_dot_nt(k, q) for k, _ in key_sets]
                m = None
                for s in scores:
                    sm = jnp.max(s, axis=0, keepdims=True)
                    m = sm if m is None else jnp.maximum(m, sm)
                denom = o_t = None
                for s, (_, vt) in zip(scores, key_sets):
                    p = jnp.exp2(s - m)
                    ps = jnp.sum(p, axis=0, keepdims=True)
                    pv = _dot(vt, p.astype(BF16))
                    denom, o_t = (ps, pv) if denom is None else (denom + ps, o_t + pv)
                heads_scr[hd, rows_q, :] = (o_t * (1.0 / denom)).T.astype(BF16)
            return carry

        lax.fori_loop(0, N_HEADS, one_head, 0)

    safe = safe_scr[0]
    for kv in range(1, N_KV_HEADS):
        safe = jnp.minimum(safe, safe_scr[kv])
    pl.when(safe == 1)(bounded)
    pl.when(safe != 1)(exact)

    attn = jnp.concatenate([heads_scr[hd] for hd in range(N_HEADS)], axis=-1)
    o_ref[...] = x_ref[...] + mod_ref[2:3] * _dot(attn, w_o_ref[...])


def _attn_sublayer(q, key_segments, x, mod, layer, row0, row_stride, w_o, q_gain, tile, inner=None):
    n_seq, seq_len, _ = x.shape
    if inner is not None and inner == seq_len:
        inner = None
    assert inner is None or (len(key_segments) == 1 and tile % inner == 0)
    in_specs = [pl.BlockSpec((None, N_HEADS, tile, HEAD_DIM), lambda s, t: (s, 0, t, 0))]
    args = [q]
    for k, vt in key_segments:
        n_keys = k.shape[2]
        in_specs += [pl.BlockSpec((None, N_KV_HEADS, n_keys, HEAD_DIM), lambda s, t: (s, 0, 0, 0)),
                     pl.BlockSpec((None, N_KV_HEADS, HEAD_DIM, n_keys), lambda s, t: (s, 0, 0, 0))]
        args += [k, vt]
    in_specs += [
        pl.BlockSpec((None, tile, D_MODEL), lambda s, t: (s, t, 0)),
        _mod_spec(layer, row0, row_stride),
        _const_spec((D_MODEL, D_MODEL)),
        _const_spec((1, HEAD_DIM)),
    ]
    args += [x, mod, w_o, q_gain]
    return pl.pallas_call(
        functools.partial(_attn_kernel, n_segments=len(key_segments), key_chunk=ATTN_KEY_CHUNK,
                          inner=inner),
        out_shape=jax.ShapeDtypeStruct(x.shape, F32),
        grid=(n_seq, seq_len // tile),
        in_specs=in_specs,
        out_specs=pl.BlockSpec((None, tile, D_MODEL), lambda s, t: (s, t, 0)),
        scratch_shapes=[pltpu.VMEM((N_HEADS, tile, HEAD_DIM), BF16),
                        pltpu.VMEM((N_KV_HEADS, 8, HEAD_DIM), F32),
                        pltpu.SMEM((N_KV_HEADS,), jnp.int32)],
        compiler_params=_params(("arbitrary", "arbitrary")),
        name="attention",
    )(*args)


def _rope_tables(seq_len):
    rows = seq_len // GRID_W
    r = np.repeat(np.arange(rows), GRID_W).astype(np.float32)
    col = np.tile(np.arange(GRID_W), rows).astype(np.float32)
    expo = -np.arange(0, AXIS_DIM, 2, dtype=np.float32) / np.float32(AXIS_DIM)
    inv = np.power(np.float32(ROPE_THETA), expo).astype(np.float32)
    ar = r[:, None] * inv
    ac = col[:, None] * inv
    ang = np.concatenate([ar, ar, ac, ac], axis=-1)
    cos, sin = np.cos(ang), np.sin(ang)
    first = (np.arange(HEAD_DIM) % AXIS_DIM) < AXIS_DIM // 2
    return (jnp.asarray(_rope_lane_order(cos), F32),
            jnp.asarray(_rope_lane_order(np.where(first, -sin, sin)), F32))


def _rope_weights(w_qkv):
    qk = (N_HEADS + N_KV_HEADS) * HEAD_DIM
    lead = w_qkv.shape[:-1]
    w_qk = _rope_lane_order(w_qkv[..., :qk].reshape(*lead, N_HEADS + N_KV_HEADS, HEAD_DIM))
    return jnp.concatenate([w_qk.reshape(*lead, qk), w_qkv[..., qk:]], axis=-1)


def _run_layer(x, l, mod, grp, wts, ffn_w, ffn_f32, odd_w, odd_f32, ride):
    row0, row_stride = grp["row0"], grp["row_stride"]
    i = l // 2
    g1 = wts["norm_g"][l, 0].reshape(1, D_MODEL)
    g2 = wts["norm_g"][l, 1].reshape(1, D_MODEL)
    new_kv = None
    if l % 2 == 0:
        tile = min(EVEN_TILE, x.shape[1])
        casts = []
        if l == 0 and 0 not in ffn_w:
            steps = x.shape[0] * (x.shape[1] // tile)
            if ride and all(_can_ride(w, steps) for w in ffn_f32 + odd_f32):
                casts = [(w, 0, "plain") for w in ffn_f32]
                for i_odd in range(DEPTH // 2):
                    casts += [(odd_f32[0], i_odd, "plain"), (odd_f32[0], i_odd, "rope"),
                              (odd_f32[1], i_odd, "plain")]
            else:
                ffn_w[0] = tuple(w[0].astype(BF16) for w in ffn_f32)
                for i_odd in range(DEPTH // 2):
                    w_qkv_bf16 = odd_f32[0][i_odd].astype(BF16)
                    odd_w[i_odd] = (w_qkv_bf16, _rope_weights(w_qkv_bf16),
                                    odd_f32[1][i_odd].astype(BF16))
        x, cast_out = _even_sublayer(x, mod, l, row0, row_stride, g1, wts["w_in_even"][i],
                                     wts["pool_w"][i], wts["pool_scale"][i].reshape(1, POOL_WIDTH),
                                     wts["conv_w"][i], wts["conv_b"][i].reshape(1, CONV_WIDTH),
                                     wts["w_out_even"][i], tile, grp["inner"], casts)
        if casts:
            ffn_w[0] = tuple(cast_out[:2])
            for i_odd in range(DEPTH // 2):
                odd_w[i_odd] = tuple(cast_out[2 + 3 * i_odd:5 + 3 * i_odd])
    else:
        cache, rope_tables = grp["cache"], grp["rope_tables"]
        sfx = "" if rope_tables is None else "_rope"
        w_qkv, w_qkv_rope, w_o = odd_w[i]
        res = _qkv_proj(x, mod, l, row0, row_stride, g1, w_qkv if rope_tables is None else w_qkv_rope,
                        wts["q_gain" + sfx][i].reshape(1, HEAD_DIM),
                        wts["k_gain" + sfx][i].reshape(1, HEAD_DIM), rope_tables, cache, i,
                        cache is None, min(QKV_TILE, x.shape[1]))
        q, segments = res[0], [res[1:3]]
        if cache is None:
            new_kv = res[3:]
        else:
            segments.append(res[3:5])
        x = _attn_sublayer(q, segments, x, mod, l, row0, row_stride, w_o,
                           wts["q_gain"][i].reshape(1, HEAD_DIM), grp["attn_tile"], grp["inner"])
    flat = x.reshape(1, -1, D_MODEL) if row_stride == 0 else x
    tile = min(FFN_TILE, flat.shape[1])
    casts = []
    if l + 1 < DEPTH and l + 1 not in ffn_w:
        steps = flat.shape[0] * (flat.shape[1] // tile)
        if ride and all(_can_ride(w, steps) for w in ffn_f32):
            casts = [(w, l + 1, "plain") for w in ffn_f32]
        else:
            ffn_w[l + 1] = tuple(w[l + 1].astype(BF16) for w in ffn_f32)
    flat, cast_out = _ffn_sublayer(flat, mod, l, row0, row_stride, g2, *ffn_w[l],
                                   wts["final_g"].reshape(1, D_MODEL), l == DEPTH - 1, tile, casts)
    if casts:
        ffn_w[l + 1] = tuple(cast_out)
    return flat.reshape(x.shape), new_kv


def kernel(x_prompt, x_sample, cache_k, cache_v, c, c_ctx, w_ada, b_ada, norm_g, w_in_even, pool_w,
           pool_scale, conv_w, conv_b, w_out_even, w_qkv, q_gain, k_gain, w_o, w_ffn_in, w_ffn_out,
           final_g):
    n_ctx, ctx_len, _ = x_prompt.shape
    n_lat, lat_len, _ = x_sample.shape
    assert 1 + n_lat <= COND_ROWS

    cond = jnp.concatenate(
        [c_ctx[None, :], c, jnp.zeros((COND_ROWS - 1 - n_lat, D_MODEL), F32)], axis=0)
    mod = _ada(cond, w_ada, b_ada).reshape(DEPTH, COND_ROWS, 6, D_MODEL)

    wts = dict(
        norm_g=norm_g, final_g=final_g, pool_scale=pool_scale, conv_w=conv_w, conv_b=conv_b,
        q_gain=q_gain, k_gain=k_gain,
        w_in_even=w_in_even.astype(BF16), pool_w=pool_w.astype(BF16),
        w_out_even=w_out_even.astype(BF16),
        q_gain_rope=_rope_lane_order(q_gain), k_gain_rope=_rope_lane_order(k_gain),
    )

    pack = math.gcd(n_ctx, max(1, CTX_PACK_ROWS // ctx_len))
    ctx_grp = dict(row0=0, row_stride=0, cache=None, rope_tables=None, attn_tile=pack * ctx_len,
                   inner=ctx_len)
    lat_grp = dict(row0=1, row_stride=1, cache=(cache_k, cache_v), rope_tables=_rope_tables(lat_len),
                   attn_tile=ATTN_TILE, inner=None)
    ffn_w, ffn_f32 = {}, (w_ffn_in, w_ffn_out)
    odd_w, odd_f32 = {}, (w_qkv, w_o)
    y_prompt = x_prompt.reshape(n_ctx // pack, pack * ctx_len, D_MODEL)
    y_sample, ctx_kv = x_sample, []
    for l in range(DEPTH):
        y_sample, _ = _run_layer(y_sample, l, mod, lat_grp, wts, ffn_w, ffn_f32, odd_w, odd_f32, True)
        y_prompt, kv = _run_layer(y_prompt, l, mod, ctx_grp, wts, ffn_w, ffn_f32, odd_w, odd_f32,
                                  False)
        if kv is not None:
            ctx_kv.append(kv)
    kv_shape = (n_ctx, 1, ctx_len, N_KV_HEADS, HEAD_DIM)
    new_cache_k = jnp.concatenate([k.reshape(kv_shape) for k, _ in ctx_kv], axis=1)
    new_cache_v = jnp.concatenate([v.reshape(kv_shape) for _, v in ctx_kv], axis=1)
    return y_prompt.reshape(x_prompt.shape), y_sample, new_cache_k, new_cache_v
```
